```python
import math
import jax
import jax.numpy as jnp
from jax import lax
import numpy as np

D_MODEL = 2048
BATCH = 4
SEQ = 2048
DEPTH = 2

GRID_W = 64
CTX_LEN = 256
EPS = 1e-6
ROPE_THETA = 10000.0
HEAD_DIM = 128

SSD_HEADS = 32
SSD_HEAD_DIM = 64
SSD_INNER = SSD_HEADS * SSD_HEAD_DIM
SSD_GROUPS = 4
SSD_STATE = 128
SSD_GN = SSD_GROUPS * SSD_STATE
SSD_XBC = SSD_INNER + 2 * SSD_GN
SSD_CONV = 5
SSD_CHUNK = 128
SSD_PROJ = SSD_INNER + SSD_XBC + 2 * SSD_HEADS

NA_HEADS = 16
NA_HEAD_DIM = HEAD_DIM
NA_WIDTH = NA_HEADS * NA_HEAD_DIM
NA_WIN_ROWS = 8
NA_WIN_COLS = 16

EVEN_IN = SSD_PROJ + 3 * NA_WIDTH
EVEN_MIX = SSD_INNER + NA_WIDTH

GQA_HEADS = 16
GQA_KV_HEADS = 4
GQA_GROUP = GQA_HEADS // GQA_KV_HEADS
GQA_Q = GQA_HEADS * HEAD_DIM
GQA_KV = GQA_KV_HEADS * HEAD_DIM
DIFF_HEADS = 8
DIFF_QK = DIFF_HEADS * 2 * HEAD_DIM
DIFF_V_DIM = 2 * HEAD_DIM
DIFF_V = DIFF_HEADS * DIFF_V_DIM

ODD_Q = GQA_Q + DIFF_QK
ODD_IN = ODD_Q + 2 * GQA_KV + DIFF_QK + DIFF_V
ODD_MIX = GQA_Q + DIFF_V
Q_BLOCK = 128

MOE_GROUPS = 8
MOE_EXPERTS_PER_GROUP = 8
MOE_EXPERTS = MOE_GROUPS * MOE_EXPERTS_PER_GROUP
MOE_HIDDEN = 512
MOE_TOPK = 2
MOE_BLOCK = 128

kernel_name = 'hybrid_ssd_natten_gqa_diffattn_hmoe'


def rms_norm(x, w):
    xf = x.astype(jnp.float32)
    y = xf * lax.rsqrt(jnp.mean(xf * xf, axis=-1, keepdims=True) + EPS)
    return (y * w.astype(jnp.float32)).astype(x.dtype)


def axial_rope(n_tokens):
    t = jnp.arange(n_tokens, dtype=jnp.int32)
    row = (t // GRID_W).astype(jnp.float32)
    col = (t % GRID_W).astype(jnp.float32)
    axis_dim = HEAD_DIM // 2
    inv = 1.0 / (ROPE_THETA ** (jnp.arange(0, axis_dim, 2, dtype=jnp.float32) / axis_dim))
    ang = jnp.concatenate([row[:, None] * inv[None], col[:, None] * inv[None]], axis=-1)
    return jnp.cos(ang), jnp.sin(ang)


def apply_rope(x, cos, sin):
    xf = x.astype(jnp.float32).reshape(*x.shape[:-1], -1, 2)
    x0, x1 = xf[..., 0], xf[..., 1]
    c, s = cos[:, None, :], sin[:, None, :]
    out = jnp.stack([x0 * c - x1 * s, x0 * s + x1 * c], axis=-1)
    return out.reshape(x.shape).astype(x.dtype)


def softmax_attend(q, k, v):
    s = jnp.einsum('bqhgd,bkhd->bhgqk', q, k).astype(jnp.float32) * (q.shape[-1] ** -0.5)
    p = jax.nn.softmax(s, axis=-1).astype(v.dtype)
    o = jnp.einsum('bhgqk,bkhd->bqhgd', p, v)
    return o.reshape(*q.shape[:2], -1)


def diff_attend(q, k, v, lam, subln_w, lambda_init):
    s = jnp.einsum('bqhcd,bkhcd->bhcqk', q, k).astype(jnp.float32) * (q.shape[-1] ** -0.5)
    p = jax.nn.softmax(s, axis=-1)
    w = p[:, :, 0] - lam * p[:, :, 1]
    o = jnp.einsum('bhqk,bkhe->bqhe', w, v.astype(jnp.float32))
    o = rms_norm(o, subln_w) * (1.0 - lambda_init)
    return o.reshape(*q.shape[:2], -1).astype(v.dtype)


def centred_depthwise_conv(x, w, b):
    ch = x.shape[-1]
    y = lax.conv_general_dilated(x, w[:, None, :], window_strides=(1,),
                                 padding=[(SSD_CONV // 2, SSD_CONV // 2)],
                                 dimension_numbers=('NWC', 'WIO', 'NWC'),
                                 feature_group_count=ch)
    return y + b


def segsum(a):
    n = a.shape[-1]
    cs = jnp.cumsum(a, axis=-1)
    diff = cs[..., :, None] - cs[..., None, :]
    return jnp.where(jnp.tril(jnp.ones((n, n), dtype=bool)), diff, -jnp.inf)


def ssd_scan(xs, dt, a_head, bm, cm, init, need_y):
    f32 = jnp.float32
    bsz, L, H, P = xs.shape
    G, N = bm.shape[-2:]
    E = H // G
    nc = L // SSD_CHUNK
    X = (xs.astype(f32) * dt[..., None]).reshape(bsz, nc, SSD_CHUNK, G, E, P)
    a = (dt * a_head).reshape(bsz, nc, SSD_CHUNK, G, E).transpose(0, 3, 4, 1, 2)
    Bc = bm.astype(f32).reshape(bsz, nc, SSD_CHUNK, G, N)
    Cc = cm.astype(f32).reshape(bsz, nc, SSD_CHUNK, G, N)
    a_cs = jnp.cumsum(a, axis=-1)
    decay_to_end = jnp.exp(a_cs[..., -1:] - a_cs)
    chunk_states = jnp.einsum('bclgn,bgecl,bclgep->bcgepn', Bc, decay_to_end, X)
    states = jnp.concatenate([init.reshape(bsz, 1, G, E, P, N), chunk_states], axis=1)
    chunk_decay = jnp.exp(segsum(jnp.pad(a_cs[..., -1], ((0, 0), (0, 0), (0, 0), (1, 0)))))
    states = jnp.einsum('bgezc,bcgepn->bzgepn', chunk_decay, states)
    final = states[:, -1].reshape(bsz, H, P, N)
    if not need_y:
        return None, final
    cb = jnp.einsum('bclgn,bcsgn->bgcls', Cc, Bc)
    y_diag = jnp.einsum('bgcls,bgecls,bcsgep->bclgep', cb, jnp.exp(segsum(a)), X)
    y_off = jnp.einsum('bclgn,bcgepn,bgecl->bclgep', Cc, states[:, :-1], jnp.exp(a_cs))
    return (y_diag + y_off).reshape(bsz, L, H, P), final


def ssd_branch(p_lat, p_ctx, conv_w, conv_b, dt_bias, a_log, d_skip, norm_w, need_ctx):
    f32 = jnp.float32
    A = -jnp.exp(a_log.astype(f32))

    def prep(p):
        bsz, L, _ = p.shape
        z = p[..., :SSD_INNER]
        xbc = jax.nn.silu(centred_depthwise_conv(p[..., SSD_INNER:SSD_INNER + SSD_XBC], conv_w, conv_b))
        dt = jax.nn.softplus(p[..., SSD_INNER + SSD_XBC:].astype(f32).reshape(bsz, L, 2, SSD_HEADS)
                             + dt_bias.astype(f32))
        xs = xbc[..., :SSD_INNER].reshape(bsz, L, SSD_HEADS, SSD_HEAD_DIM)
        bm = xbc[..., SSD_INNER:SSD_INNER + SSD_GN].reshape(bsz, L, SSD_GROUPS, SSD_STATE)
        cm = xbc[..., SSD_INNER + SSD_GN:].reshape(bsz, L, SSD_GROUPS, SSD_STATE)
        return z, xs, dt, bm, cm

    def flip(t):
        return jnp.flip(t, axis=1)

    def bidir(xs, dt, bm, cm, init_f, init_b, need_y):
        y_f, fin_f = ssd_scan(xs, dt[:, :, 0], A[0], bm, cm, init_f, need_y)
        y_b, fin_b = ssd_scan(flip(xs), flip(dt[:, :, 1]), A[1], flip(bm), flip(cm), init_b, need_y)
        y = (y_f + flip(y_b) + d_skip.astype(f32)[:, None] * xs.astype(f32)) if need_y else None
        return y, fin_f, fin_b

    def gate_out(y, z):
        bsz, L = z.shape[:2]
        gsz = SSD_INNER // SSD_GROUPS
        g = y.reshape(bsz, L, SSD_GROUPS, gsz) * jax.nn.silu(z.astype(f32)).reshape(bsz, L, SSD_GROUPS, gsz)
        return rms_norm(g, norm_w.reshape(SSD_GROUPS, gsz)).reshape(bsz, L, SSD_INNER).astype(z.dtype)

    zc, xc, dtc, bc, cc = prep(p_ctx)
    zero = jnp.zeros((xc.shape[0], SSD_HEADS, SSD_HEAD_DIM, SSD_STATE), f32)
    y_c, st_f, st_b = bidir(xc, dtc, bc, cc, zero, zero, need_ctx)
    zl, xl, dtl, bl, cl = prep(p_lat)
    y_l, _, _ = bidir(xl, dtl, bl, cl, st_f, st_b, True)
    return gate_out(y_l, zl), (gate_out(y_c, zc) if need_ctx else None)


def na_branch(p_lat, p_ctx, q_norm_w, k_norm_w, rpb, need_ctx):
    f32 = jnp.float32
    B, S, _ = p_lat.shape
    rows = S // GRID_W
    wr = min(NA_WIN_ROWS, rows)
    scale = NA_HEAD_DIM ** -0.5

    def qkv(p):
        shp = (B, p.shape[1], NA_HEADS, NA_HEAD_DIM)
        q = rms_norm(p[..., :NA_WIDTH].reshape(shp), q_norm_w)
        k = rms_norm(p[..., NA_WIDTH:2 * NA_WIDTH].reshape(shp), k_norm_w)
        v = p[..., 2 * NA_WIDTH:].reshape(shp)
        return q, k, v

    ql, kl, vl = qkv(p_lat)
    qc, kc, vc = qkv(p_ctx)
    qg = ql.reshape(B, rows, GRID_W, NA_HEADS, NA_HEAD_DIM)
    kg = kl.reshape(B, rows, GRID_W, NA_HEADS, NA_HEAD_DIM)
    vg = vl.reshape(B, rows, GRID_W, NA_HEADS, NA_HEAD_DIM)
    qcol = jnp.arange(GRID_W)
    col_start = jnp.clip(qcol - NA_WIN_COLS // 2, 0, GRID_W - NA_WIN_COLS)
    col_mask = (qcol[None, :] >= col_start[:, None]) & (qcol[None, :] < col_start[:, None] + NA_WIN_COLS)
    dcol_idx = jnp.clip(qcol[None, :] - qcol[:, None] + NA_WIN_COLS - 1, 0, 2 * NA_WIN_COLS - 2)
    rpb_cols = rpb[:, :, dcol_idx]
    n_win = wr * GRID_W

    def row_block(r):
        r0 = jnp.clip(r - wr // 2, 0, rows - wr)
        q_r = lax.dynamic_index_in_dim(qg, r, axis=1, keepdims=False)
        k_w = lax.dynamic_slice_in_dim(kg, r0, wr, axis=1)
        v_w = lax.dynamic_slice_in_dim(vg, r0, wr, axis=1)
        bias = jnp.take(rpb_cols, r0 + jnp.arange(wr) - r + NA_WIN_ROWS - 1, axis=1)
        s_win = (jnp.einsum('bqhd,bikhd->bhqik', q_r, k_w).astype(f32) * scale
                 + bias.transpose(0, 2, 1, 3).astype(f32))
        s_win = jnp.where(col_mask[:, None, :], s_win, -jnp.inf)
        s_ctx = jnp.einsum('bqhd,bthd->bhqt', q_r, kc).astype(f32) * scale
        p = jax.nn.softmax(jnp.concatenate([s_win.reshape(B, NA_HEADS, GRID_W, n_win), s_ctx], axis=-1),
                           axis=-1).astype(vl.dtype)
        o = (jnp.einsum('bhqik,bikhd->bqhd', p[..., :n_win].reshape(B, NA_HEADS, GRID_W, wr, GRID_W), v_w)
             + jnp.einsum('bhqt,bthd->bqhd', p[..., n_win:], vc))
        return o

    o = lax.map(row_block, jnp.arange(rows))
    o_lat = o.swapaxes(0, 1).reshape(B, S, NA_WIDTH)
    o_ctx = softmax_attend(qc[:, :, :, None], kc, vc) if need_ctx else None
    return o_lat, o_ctx


def ssd_na_layer(a_lat, a_ctx, in_w, conv_w, conv_b, dt_bias, a_log, d_skip, ssd_norm_w,
                 na_q_norm, na_k_norm, na_rpb, out_w, need_ctx):
    p_lat = a_lat @ in_w
    p_ctx = a_ctx @ in_w
    s_lat, s_ctx = ssd_branch(p_lat[..., :SSD_PROJ], p_ctx[..., :SSD_PROJ], conv_w, conv_b,
                              dt_bias, a_log, d_skip, ssd_norm_w, need_ctx)
    n_lat, n_ctx = na_branch(p_lat[..., SSD_PROJ:], p_ctx[..., SSD_PROJ:], na_q_norm, na_k_norm,
                             na_rpb, need_ctx)
    m_lat = jnp.concatenate([s_lat, n_lat.astype(s_lat.dtype)], axis=-1) @ out_w
    m_ctx = (jnp.concatenate([s_ctx, n_ctx.astype(s_ctx.dtype)], axis=-1) @ out_w) if need_ctx else None
    return m_lat, m_ctx


def gqa_diff_layer(a_lat, a_ctx, in_w, gq_norm, gk_norm, dq_norm, dk_norm, lam_vecs, subln_w, out_w,
                   lambda_init, cos, sin, need_ctx):
    B, S, _ = a_lat.shape
    T = a_ctx.shape[1]
    lv = lam_vecs.astype(jnp.float32)
    lam = jnp.exp(jnp.dot(lv[0], lv[1])) - jnp.exp(jnp.dot(lv[2], lv[3])) + lambda_init

    def queries(pq):
        L = pq.shape[1]
        gq = rms_norm(pq[..., :GQA_Q].reshape(B, L, GQA_HEADS, HEAD_DIM), gq_norm)
        dq = rms_norm(pq[..., GQA_Q:ODD_Q].reshape(B, L, 2 * DIFF_HEADS, HEAD_DIM), dq_norm)
        return gq, dq

    def keys_values(pkv):
        L = pkv.shape[1]
        o1, o2 = GQA_KV, 2 * GQA_KV
        o3 = o2 + DIFF_QK
        gk = rms_norm(pkv[..., :o1].reshape(B, L, GQA_KV_HEADS, HEAD_DIM), gk_norm)
        gv = pkv[..., o1:o2].reshape(B, L, GQA_KV_HEADS, HEAD_DIM)
        dk = rms_norm(pkv[..., o2:o3].reshape(B, L, 2 * DIFF_HEADS, HEAD_DIM), dk_norm)
        dv = pkv[..., o3:].reshape(B, L, DIFF_HEADS, DIFF_V_DIM)
        return gk, gv, dk, dv

    p_lat = a_lat @ in_w
    gq, dq = queries(p_lat)
    gk, gv, dk, dv = keys_values(p_lat[..., ODD_Q:])
    gq, dq, gk, dk = [apply_rope(t, cos, sin) for t in (gq, dq, gk, dk)]
    cgk, cgv, cdk, cdv = keys_values(a_ctx @ in_w[:, ODD_Q:])
    gk_all = jnp.concatenate([cgk, gk], axis=1)
    gv_all = jnp.concatenate([cgv, gv], axis=1)
    dk_all = jnp.concatenate([cdk, dk], axis=1).reshape(B, T + S, DIFF_HEADS, 2, HEAD_DIM)
    dv_all = jnp.concatenate([cdv, dv], axis=1)
    nb = S // Q_BLOCK
    gq_b = gq.reshape(B, nb, Q_BLOCK, GQA_KV_HEADS, GQA_GROUP, HEAD_DIM).swapaxes(0, 1)
    dq_b = dq.reshape(B, nb, Q_BLOCK, DIFF_HEADS, 2, HEAD_DIM).swapaxes(0, 1)

    def query_block(qs):
        gqb, dqb = qs
        return (softmax_attend(gqb, gk_all, gv_all),
                diff_attend(dqb, dk_all, dv_all, lam, subln_w, lambda_init))

    og, od = lax.map(query_block, (gq_b, dq_b))
    mix = jnp.concatenate([og.swapaxes(0, 1).reshape(B, S, GQA_Q),
                           od.swapaxes(0, 1).reshape(B, S, DIFF_V)], axis=-1)
    m_lat = mix @ out_w
    m_ctx = None
    if need_ctx:
        cgq, cdq = queries(a_ctx @ in_w[:, :ODD_Q])
        c_mix = jnp.concatenate([
            softmax_attend(cgq.reshape(B, T, GQA_KV_HEADS, GQA_GROUP, HEAD_DIM), cgk, cgv),
            diff_attend(cdq.reshape(B, T, DIFF_HEADS, 2, HEAD_DIM),
                        cdk.reshape(B, T, DIFF_HEADS, 2, HEAD_DIM), cdv, lam, subln_w, lambda_init)], axis=-1)
        m_ctx = c_mix @ out_w
    return m_lat, m_ctx


def routed_experts(t, e_idx, gate, w1, w3, w2):
    N, D = t.shape
    E = w1.shape[0]
    A = N * MOE_TOPK
    flat_e = e_idx.reshape(-1)
    flat_tok = jnp.repeat(jnp.arange(N, dtype=jnp.int32), MOE_TOPK)
    flat_gate = gate.reshape(-1)
    order = jnp.argsort(flat_e)
    se = flat_e[order]
    counts = jnp.bincount(flat_e, length=E)
    padded = (counts + MOE_BLOCK - 1) // MOE_BLOCK * MOE_BLOCK
    pad_end = jnp.cumsum(padded)
    pad_start = pad_end - padded
    raw_start = jnp.cumsum(counts) - counts
    dest = pad_start[se] + jnp.arange(A) - raw_start[se]
    n_blocks = -(-A // MOE_BLOCK) + E
    slot_tok = jnp.full((n_blocks * MOE_BLOCK,), N, jnp.int32).at[dest].set(flat_tok[order])
    block_e = jnp.minimum(jnp.searchsorted(pad_end, jnp.arange(n_blocks) * MOE_BLOCK, side='right'), E - 1)
    t_pad = jnp.concatenate([t, jnp.zeros((1, D), t.dtype)], axis=0)
    xb = t_pad[slot_tok].reshape(n_blocks, MOE_BLOCK, D)

    def expert_block(args):
        xblk, e = args
        h = jax.nn.silu(xblk @ w1[e]) * (xblk @ w3[e])
        return h @ w2[e]

    yb = lax.map(expert_block, (xb, block_e)).reshape(n_blocks * MOE_BLOCK, D)
    y_assign = yb[dest].astype(jnp.float32) * flat_gate[order][:, None]
    return jax.ops.segment_sum(y_assign, flat_tok[order], num_segments=N).astype(t.dtype)


def hier_moe(t, group_w, expert_w, w1, w3, w2):
    N = t.shape[0]
    g_logits = (t @ group_w).astype(jnp.float32)
    g_prob = jax.nn.softmax(g_logits, axis=-1)
    _, g_idx = lax.top_k(g_logits, 1)
    g_gate = jnp.take_along_axis(g_prob, g_idx, axis=-1)
    e_logits = (t @ expert_w).astype(jnp.float32).reshape(N, MOE_GROUPS, MOE_EXPERTS_PER_GROUP)
    e_in_g = jnp.take_along_axis(e_logits, g_idx[:, :, None], axis=1)[:, 0]
    top_v, top_i = lax.top_k(e_in_g, MOE_TOPK)
    gate = jax.nn.softmax(top_v, axis=-1) * g_gate
    e_idx = g_idx * MOE_EXPERTS_PER_GROUP + top_i
    return routed_experts(t, e_idx, gate, w1, w3, w2)


def setup_inputs(seed: int = 0) -> dict:
    key = jax.random.key(seed)
    ks = jax.random.split(key, 40)
    f32 = jnp.float32
    D = D_MODEL
    n_even = (DEPTH + 1) // 2
    n_odd = DEPTH // 2

    def nrm(k, shape, scale):
        return jax.random.normal(k, shape, f32) * scale

    dt0 = jnp.exp(jax.random.uniform(ks[10], (n_even, 2, SSD_HEADS), f32, math.log(1e-3), math.log(1e-1)))
    return {
        'x': nrm(ks[0], (BATCH, SEQ, D), 1.0),
        'c': nrm(ks[1], (BATCH, D), 1.0),
        'ctx': nrm(ks[2], (BATCH, CTX_LEN, D), 1.0),
        'c_ctx': nrm(ks[3], (D,), 1.0),
        'ada_w': nrm(ks[4], (DEPTH, D, 6 * D), 0.5 * D ** -0.5),
        'ada_b': nrm(ks[5], (DEPTH, 6 * D), 0.02),
        'norm_w': 1.0 + nrm(ks[6], (DEPTH, 2, D), 0.02),
        'ev_in_w': nrm(ks[7], (n_even, D, EVEN_IN), D ** -0.5),
        'ev_conv_w': nrm(ks[8], (n_even, SSD_CONV, SSD_XBC), SSD_CONV ** -0.5),
        'ev_conv_b': nrm(ks[9], (n_even, SSD_XBC), 0.02),
        'ev_dt_bias': dt0 + jnp.log(-jnp.expm1(-dt0)),
        'ev_a_log': jnp.log(jax.random.uniform(ks[11], (n_even, 2, SSD_HEADS), f32, 1.0, 16.0)),
        'ev_d_skip': 1.0 + nrm(ks[12], (n_even, SSD_HEADS), 0.02),
        'ev_ssd_norm_w': 1.0 + nrm(ks[13], (n_even, SSD_INNER), 0.02),
        'ev_na_q_norm': 1.0 + nrm(ks[14], (n_even, NA_HEAD_DIM), 0.02),
        'ev_na_k_norm': 1.0 + nrm(ks[15], (n_even, NA_HEAD_DIM), 0.02),
        'ev_na_rpb': nrm(ks[16], (n_even, NA_HEADS, 2 * NA_WIN_ROWS - 1, 2 * NA_WIN_COLS - 1), 0.1),
        'ev_out_w': nrm(ks[17], (n_even, EVEN_MIX, D), EVEN_MIX ** -0.5),
        'od_in_w': nrm(ks[18], (n_odd, D, ODD_IN), D ** -0.5),
        'od_gqa_q_norm': 1.0 + nrm(ks[19], (n_odd, HEAD_DIM), 0.02),
        'od_gqa_k_norm': 1.0 + nrm(ks[20], (n_odd, HEAD_DIM), 0.02),
        'od_diff_q_norm': 1.0 + nrm(ks[21], (n_odd, HEAD_DIM), 0.02),
        'od_diff_k_norm': 1.0 + nrm(ks[22], (n_odd, HEAD_DIM), 0.02),
        'od_lambda': nrm(ks[23], (n_odd, 4, HEAD_DIM), 0.1),
        'od_diff_subln': 1.0 + nrm(ks[24], (n_odd, DIFF_V_DIM), 0.02),
        'od_out_w': nrm(ks[25], (n_odd, ODD_MIX, D), ODD_MIX ** -0.5),
        'moe_group_w': nrm(ks[26], (DEPTH, D, MOE_GROUPS), D ** -0.5),
        'moe_expert_w': nrm(ks[27], (DEPTH, D, MOE_EXPERTS), D ** -0.5),
        'moe_w1': nrm(ks[28], (DEPTH, MOE_EXPERTS, D, MOE_HIDDEN), D ** -0.5),
        'moe_w3': nrm(ks[29], (DEPTH, MOE_EXPERTS, D, MOE_HIDDEN), D ** -0.5),
        'moe_w2': nrm(ks[30], (DEPTH, MOE_EXPERTS, MOE_HIDDEN, D), MOE_HIDDEN ** -0.5),
    }


def reference(x, c, ctx, c_ctx, ada_w, ada_b, norm_w,
              ev_in_w, ev_conv_w, ev_conv_b, ev_dt_bias, ev_a_log, ev_d_skip, ev_ssd_norm_w,
              ev_na_q_norm, ev_na_k_norm, ev_na_rpb, ev_out_w,
              od_in_w, od_gqa_q_norm, od_gqa_k_norm, od_diff_q_norm, od_diff_k_norm,
              od_lambda, od_diff_subln, od_out_w,
              moe_group_w, moe_expert_w, moe_w1, moe_w3, moe_w2):
    B, S, D = x.shape
    T = ctx.shape[1]
    cos, sin = axial_rope(S)
    h_lat, h_ctx = x, ctx
    for l in range(DEPTH):
        last = l == DEPTH - 1
        i = l // 2
        mod_lat = jax.nn.silu(c) @ ada_w[l] + ada_b[l]
        mod_ctx = jax.nn.silu(c_ctx) @ ada_w[l] + ada_b[l]
        sh1, sc1, g1, sh2, sc2, g2 = jnp.split(mod_lat[:, None, :], 6, axis=-1)
        csh1, csc1, cg1, csh2, csc2, cg2 = jnp.split(mod_ctx, 6)
        a_lat = rms_norm(h_lat, norm_w[l, 0]) * (1 + sc1) + sh1
        a_ctx = rms_norm(h_ctx, norm_w[l, 0]) * (1 + csc1) + csh1
        if l % 2 == 0:
            m_lat, m_ctx = ssd_na_layer(a_lat, a_ctx, ev_in_w[i], ev_conv_w[i], ev_conv_b[i], ev_dt_bias[i],
                                        ev_a_log[i], ev_d_skip[i], ev_ssd_norm_w[i], ev_na_q_norm[i],
                                        ev_na_k_norm[i], ev_na_rpb[i], ev_out_w[i], not last)
        else:
            lambda_init = 0.8 - 0.6 * math.exp(-0.3 * l)
            m_lat, m_ctx = gqa_diff_layer(a_lat, a_ctx, od_in_w[i], od_gqa_q_norm[i], od_gqa_k_norm[i],
                                          od_diff_q_norm[i], od_diff_k_norm[i], od_lambda[i],
                                          od_diff_subln[i], od_out_w[i], lambda_init, cos, sin, not last)
        h_lat = h_lat + g1 * m_lat.astype(h_lat.dtype)
        f_lat = rms_norm(h_lat, norm_w[l, 1]) * (1 + sc2) + sh2
        if last:
            y = hier_moe(f_lat.reshape(B * S, D), moe_group_w[l], moe_expert_w[l],
                         moe_w1[l], moe_w3[l], moe_w2[l])
            h_lat = h_lat + g2 * y.reshape(B, S, D)
        else:
            h_ctx = h_ctx + cg1 * m_ctx.astype(h_ctx.dtype)
            f_ctx = rms_norm(h_ctx, norm_w[l, 1]) * (1 + csc2) + csh2
            y = hier_moe(jnp.concatenate([f_lat.reshape(B * S, D), f_ctx.reshape(B * T, D)], axis=0),
                         moe_group_w[l], moe_expert_w[l], moe_w1[l], moe_w3[l], moe_w2[l])
            h_lat = h_lat + g2 * y[:B * S].reshape(B, S, D)
            h_ctx = h_ctx + cg2 * y[B * S:].reshape(B, T, D)
    return h_lat
```

```python
import functools
import math

import jax
import jax.numpy as jnp
from jax import lax
from jax.experimental import pallas as pl
from jax.experimental.pallas import tpu as pltpu

F32 = jnp.float32
BF16 = jnp.bfloat16
HIGHEST = lax.Precision.HIGHEST

D_MODEL = 2048
GRID_W = 64
EPS = 1e-6
ROPE_THETA = 10000.0
HEAD_DIM = 128

SSD_HEADS = 32
SSD_HEAD_DIM = 64
SSD_INNER = SSD_HEADS * SSD_HEAD_DIM
SSD_GROUPS = 4
SSD_STATE = 128
SSD_GN = SSD_GROUPS * SSD_STATE
SSD_XBC = SSD_INNER + 2 * SSD_GN
SSD_CONV = 5
SSD_CHUNK = 128
SSD_GROUP_W = SSD_INNER // SSD_GROUPS

NA_HEADS = 16
NA_WIDTH = NA_HEADS * HEAD_DIM
NA_WIN_ROWS = 8
NA_WIN_COLS = 16
NA_QROWS = 4
NA_KROWS = 12

GQA_HEADS = 16
GQA_KV_HEADS = 4
GQA_GROUP = GQA_HEADS // GQA_KV_HEADS
GQA_Q = GQA_HEADS * HEAD_DIM
GQA_KV = GQA_KV_HEADS * HEAD_DIM
DIFF_HEADS = 8
DIFF_QK = DIFF_HEADS * 2 * HEAD_DIM
DIFF_V_DIM = 2 * HEAD_DIM
DIFF_V = DIFF_HEADS * DIFF_V_DIM
ODD_Q = GQA_Q + DIFF_QK

MOE_GROUPS = 8
MOE_EXPERTS_PER_GROUP = 8
MOE_EXPERTS = MOE_GROUPS * MOE_EXPERTS_PER_GROUP
MOE_HIDDEN = 512
MOE_TOPK = 2
MOE_BLOCK = 128

ROW_BLOCK = 256
LANES = 128
MASK_VALUE = -1e30
VMEM_LIMIT = 52 * 1024 * 1024


def _cparams(*sem):
    return pltpu.CompilerParams(dimension_semantics=sem, vmem_limit_bytes=VMEM_LIMIT)


def _mod_row(i, tile, n_lat, seq, n_batch):
    return jnp.where(i < n_lat // tile, i // (seq // tile), n_batch)


def _ada_kernel(c_ref, w_ref, b_ref, o_ref):
    c = c_ref[...]
    s = c * jax.nn.sigmoid(c)
    o_ref[...] = jnp.dot(s, w_ref[...], precision=HIGHEST, preferred_element_type=F32) + b_ref[...]


def ada_modulation(cc, ada_w, ada_b):
    n_layers, d, n_out = ada_w.shape
    rows = cc.shape[0]
    tn = 1024
    return pl.pallas_call(
        _ada_kernel,
        grid=(n_layers, n_out // tn),
        in_specs=[pl.BlockSpec((rows, d), lambda l, j: (0, 0)),
                  pl.BlockSpec((None, d, tn), lambda l, j: (l, 0, j)),
                  pl.BlockSpec((None, 1, tn), lambda l, j: (l, 0, j))],
        out_specs=pl.BlockSpec((None, rows, tn), lambda l, j: (l, 0, j)),
        out_shape=jax.ShapeDtypeStruct((n_layers, rows, n_out), F32),
        compiler_params=_cparams("parallel", "parallel"),
        name="ada_modulation",
    )(cc, ada_w, ada_b.reshape(n_layers, 1, n_out))


def _normmod_kernel(h_ref, w_ref, sc_ref, sh_ref, *rest, with_router):
    x = h_ref[...]
    y = x * lax.rsqrt(jnp.mean(x * x, axis=-1, keepdims=True) + EPS) * w_ref[...]
    a = y * (1.0 + sc_ref[...]) + sh_ref[...]
    if with_router:
        rw_ref, o_ref, lg_ref = rest
        lg_ref[...] = jnp.dot(a, rw_ref[...], precision=HIGHEST, preferred_element_type=F32)
    else:
        (o_ref,) = rest
    o_ref[...] = a.astype(o_ref.dtype)


def norm_modulate(h, w, mod3, sc_chunk, sh_chunk, n_rows, geom, router_w=None):
    n_lat, seq, n_batch = geom
    d = h.shape[1]
    t = ROW_BLOCK
    row = functools.partial(_mod_row, tile=t, n_lat=n_lat, seq=seq, n_batch=n_batch)
    in_specs = [pl.BlockSpec((t, d), lambda i: (i, 0)),
                pl.BlockSpec((1, d), lambda i: (0, 0)),
                pl.BlockSpec((None, 1, d), lambda i: (row(i), 0, sc_chunk)),
                pl.BlockSpec((None, 1, d), lambda i: (row(i), 0, sh_chunk))]
    args = [h, w.reshape(1, d), mod3, mod3]
    out_specs = pl.BlockSpec((t, d), lambda i: (i, 0))
    out_shape = jax.ShapeDtypeStruct((n_rows, d), BF16)
    if router_w is not None:
        in_specs.append(pl.BlockSpec(router_w.shape, lambda i: (0, 0)))
        args.append(router_w)
        out_specs = [out_specs, pl.BlockSpec((t, router_w.shape[1]), lambda i: (i, 0))]
        out_shape = [out_shape, jax.ShapeDtypeStruct((n_rows, router_w.shape[1]), F32)]
    return pl.pallas_call(
        functools.partial(_normmod_kernel, with_router=router_w is not None),
        grid=(n_rows // t,),
        in_specs=in_specs, out_specs=out_specs, out_shape=out_shape,
        compiler_params=_cparams("parallel"),
        name="norm_modulate",
    )(*args)


def _matmul_kernel(*refs, n_in, residual):
    acc = None
    for x_ref, w_ref in zip(refs[:n_in], refs[n_in:2 * n_in]):
        part = jnp.dot(x_ref[...], w_ref[...], preferred_element_type=F32)
        acc = part if acc is None else acc + part
    if residual:
        h_ref, g_ref, o_ref = refs[2 * n_in:]
        o_ref[...] = h_ref[...] + g_ref[...] * acc
    else:
        (o_ref,) = refs[2 * n_in:]
        o_ref[...] = acc.astype(o_ref.dtype)


def matmul(xs, ws, n_rows, out_dtype, tn, row_off=0, residual=None):
    n_out = ws[0].shape[1]
    tm = next(t for t in (1024, 512, ROW_BLOCK) if n_rows % t == 0 and row_off % t == 0)
    off = row_off // tm
    in_specs = ([pl.BlockSpec((tm, x.shape[1]), lambda i, j: (i + off, 0)) for x in xs]
                + [pl.BlockSpec((w.shape[0], tn), lambda i, j: (0, j)) for w in ws])
    args = list(xs) + list(ws)
    if residual is not None:
        h, mod3, gate_chunk, (n_lat, seq, n_batch) = residual
        row = functools.partial(_mod_row, tile=tm, n_lat=n_lat, seq=seq, n_batch=n_batch)
        nj = mod3.shape[2] // 6 // tn
        in_specs += [pl.BlockSpec((tm, tn), lambda i, j: (i, j)),
                     pl.BlockSpec((None, 1, tn), lambda i, j: (row(i), 0, gate_chunk * nj + j))]
        args += [h, mod3]
    return pl.pallas_call(
        functools.partial(_matmul_kernel, n_in=len(xs), residual=residual is not None),
        grid=(n_rows // tm, n_out // tn),
        in_specs=in_specs,
        out_specs=pl.BlockSpec((tm, tn), lambda i, j: (i, j)),
        out_shape=jax.ShapeDtypeStruct((n_rows, n_out), out_dtype),
        compiler_params=_cparams("parallel", "parallel"),
        name="matmul_residual" if residual is not None else "matmul",
    )(*args)


CONV_PAD = 8


def _conv_silu_kernel(x_ref, w_ref, b_ref, o_ref, pad_ref):
    seq, width = o_ref.shape
    zeros = jnp.zeros((CONV_PAD, width), F32)
    pad_ref[pl.ds(0, CONV_PAD), :] = zeros
    pad_ref[pl.ds(CONV_PAD + seq, CONV_PAD), :] = zeros
    pad_ref[pl.ds(CONV_PAD, seq), :] = x_ref[...].astype(F32)
    rows = 256
    for base in range(0, seq, rows):
        acc = jnp.broadcast_to(b_ref[...], (rows, width))
        for k in range(SSD_CONV):
            acc = acc + w_ref[pl.ds(k, 1), :] * pad_ref[pl.ds(base + CONV_PAD - SSD_CONV // 2 + k, rows), :]
        o_ref[pl.ds(base, rows), :] = (acc * jax.nn.sigmoid(acc)).astype(o_ref.dtype)


def conv_silu(p_zx, conv_w, conv_b, n_seq, seq, row_off):
    tc = 512
    col0 = SSD_INNER // tc
    off = row_off // seq
    return pl.pallas_call(
        _conv_silu_kernel,
        grid=(n_seq, SSD_XBC // tc),
        in_specs=[pl.BlockSpec((seq, tc), lambda b, j: (b + off, col0 + j)),
                  pl.BlockSpec((SSD_CONV, tc), lambda b, j: (0, j)),
                  pl.BlockSpec((1, tc), lambda b, j: (0, j))],
        out_specs=pl.BlockSpec((seq, tc), lambda b, j: (b, j)),
        out_shape=jax.ShapeDtypeStruct((n_seq * seq, SSD_XBC), BF16),
        scratch_shapes=[pltpu.VMEM((seq + 2 * CONV_PAD, tc), F32)],
        compiler_params=_cparams("parallel", "parallel"),
        name="conv_silu",
    )(p_zx, conv_w, conv_b.reshape(1, SSD_XBC))


def _ssd_dt_kernel(p_ref, bias_ref, a_ref, dt_ref, acs_ref, acst_ref):
    n = p_ref.shape[0]
    pre = p_ref[...] + bias_ref[...]
    dt = jnp.maximum(pre, 0.0) + jnp.log1p(jnp.exp(-jnp.abs(pre)))
    a = dt * a_ref[...]
    li = lax.broadcasted_iota(jnp.int32, (n, n), 0)
    si = lax.broadcasted_iota(jnp.int32, (n, n), 1)
    lower = (si <= li).astype(F32)
    upper = (si >= li).astype(F32)
    fwd = jnp.dot(lower, a, precision=HIGHEST, preferred_element_type=F32)
    bwd = jnp.dot(upper, a, precision=HIGHEST, preferred_element_type=F32)
    lane = lax.broadcasted_iota(jnp.int32, fwd.shape, 1)
    acs = jnp.where(lane < SSD_HEADS, fwd, bwd)
    dt_ref[...] = dt
    acs_ref[...] = acs
    acst_ref[...] = acs.T


def ssd_dt(p_dt, dt_bias, a_log):
    rows = p_dt.shape[0]
    pad = LANES - 2 * SSD_HEADS
    bias = jnp.pad(dt_bias.reshape(1, 2 * SSD_HEADS), ((0, 0), (0, pad)))
    a_neg = jnp.pad(-jnp.exp(a_log.astype(F32)).reshape(1, 2 * SSD_HEADS), ((0, 0), (0, pad)))
    c = SSD_CHUNK
    return pl.pallas_call(
        _ssd_dt_kernel,
        grid=(rows // c,),
        in_specs=[pl.BlockSpec((c, LANES), lambda i: (i, 0)),
                  pl.BlockSpec((1, LANES), lambda i: (0, 0)),
                  pl.BlockSpec((1, LANES), lambda i: (0, 0))],
        out_specs=[pl.BlockSpec((c, LANES), lambda i: (i, 0)),
                   pl.BlockSpec((c, LANES), lambda i: (i, 0)),
                   pl.BlockSpec((LANES, c), lambda i: (0, i))],
        out_shape=[jax.ShapeDtypeStruct((rows, LANES), F32),
                   jax.ShapeDtypeStruct((rows, LANES), F32),
                   jax.ShapeDtypeStruct((LANES, rows), F32)],
        compiler_params=_cparams("parallel"),
        name="ssd_dt",
    )(p_dt, bias, a_neg)


def _ssd_scan_kernel(x_ref, b_ref, c_ref, dt_ref, acs_ref, acsr_ref, y_ref, state_ref, *, reverse):
    n = x_ref.shape[0]
    heads = dt_ref.shape[1]
    pair_w = 2 * SSD_HEAD_DIM

    @pl.when(pl.program_id(2) == 0)
    def _():
        state_ref[...] = jnp.zeros_like(state_ref)

    x = x_ref[...].astype(F32)
    bm = b_ref[...]
    cm = c_ref[...]
    dt = dt_ref[...]
    acs = acs_ref[...]
    acs_row = acsr_ref[...]
    end = 0 if reverse else n - 1
    total = acs[end:end + 1, :]
    e_in = jnp.exp(acs)
    e_out = jnp.exp(total - acs)
    e_tot = jnp.exp(total)

    li = lax.broadcasted_iota(jnp.int32, (n, n), 0)
    si = lax.broadcasted_iota(jnp.int32, (n, n), 1)
    visible = (si >= li) if reverse else (si <= li)
    lo = lax.broadcasted_iota(jnp.int32, (1, pair_w), 1) < SSD_HEAD_DIM

    cb = lax.dot_general(cm, bm, (((1,), (1,)), ((), ())), preferred_element_type=F32)
    state = state_ref[...]
    y_off = jnp.dot(cm, state.astype(BF16), preferred_element_type=F32)

    xd_parts, decay_parts = [], []
    for i in range(heads // 2):
        j0, j1 = 2 * i, 2 * i + 1
        cols = slice(i * pair_w, (i + 1) * pair_w)
        xp = x[:, cols] * jnp.where(lo, dt[:, j0:j0 + 1], dt[:, j1:j1 + 1])
        y_diag = None
        for j, keep in ((j0, lo), (j1, jnp.logical_not(lo))):
            diff = acs[:, j:j + 1] - acs_row[j:j + 1, :]
            m = (cb * jnp.exp(jnp.where(visible, diff, MASK_VALUE))).astype(BF16)
            part = jnp.dot(m, jnp.where(keep, xp, 0.0).astype(BF16), preferred_element_type=F32)
            y_diag = part if y_diag is None else y_diag + part
        y_ref[:, cols] = y_diag + y_off[:, cols] * jnp.where(lo, e_in[:, j0:j0 + 1], e_in[:, j1:j1 + 1])
        xd_parts.append((xp * jnp.where(lo, e_out[:, j0:j0 + 1], e_out[:, j1:j1 + 1])).astype(BF16))
        decay_parts.append(jnp.where(lo, e_tot[:, j0:j0 + 1], e_tot[:, j1:j1 + 1]))
    xd = jnp.concatenate(xd_parts, axis=1)
    decay = jnp.concatenate(decay_parts, axis=1)
    bt = bm.astype(F32).T.astype(BF16)
    state_ref[...] = state * decay + jnp.dot(bt, xd, preferred_element_type=F32)


def ssd_scan(xbc, dt_col, acs_col, acs_row, n_batch, seq, ctx_len, reverse):
    c = SSD_CHUNK
    lat_chunks, ctx_chunks = seq // c, ctx_len // c
    ctx_base = n_batch * lat_chunks
    n_steps = lat_chunks + ctx_chunks
    gw = SSD_GROUP_W
    heads = SSD_HEADS // SSD_GROUPS

    def chunk(b, k):
        kc = (ctx_chunks - 1 - k) if reverse else k
        kl = (n_steps - 1 - k) if reverse else (k - ctx_chunks)
        return jnp.where(k < ctx_chunks, ctx_base + b * ctx_chunks + kc, b * lat_chunks + kl)

    b_col0 = SSD_INNER // SSD_STATE
    c_col0 = (SSD_INNER + SSD_GN) // SSD_STATE
    return pl.pallas_call(
        functools.partial(_ssd_scan_kernel, reverse=reverse),
        grid=(n_batch, SSD_GROUPS, n_steps),
        in_specs=[pl.BlockSpec((c, gw), lambda b, g, k: (chunk(b, k), g)),
                  pl.BlockSpec((c, SSD_STATE), lambda b, g, k: (chunk(b, k), b_col0 + g)),
                  pl.BlockSpec((c, SSD_STATE), lambda b, g, k: (chunk(b, k), c_col0 + g)),
                  pl.BlockSpec((None, c, heads), lambda b, g, k: (g, chunk(b, k), 0)),
                  pl.BlockSpec((None, c, heads), lambda b, g, k: (g, chunk(b, k), 0)),
                  pl.BlockSpec((None, heads, c), lambda b, g, k: (g, 0, chunk(b, k)))],
        out_specs=pl.BlockSpec((c, gw), lambda b, g, k: (chunk(b, k), g)),
        out_shape=jax.ShapeDtypeStruct((xbc.shape[0], SSD_INNER), F32),
        scratch_shapes=[pltpu.VMEM((SSD_STATE, gw), F32)],
        compiler_params=_cparams("parallel", "parallel", "arbitrary"),
        name="ssd_scan_bwd" if reverse else "ssd_scan_fwd",
    )(xbc, xbc, xbc, dt_col, acs_col, acs_row)


def _ssd_gate_kernel(yf_ref, yb_ref, x_ref, z_ref, d_ref, w_ref, o_ref):
    z = z_ref[...].astype(F32)
    y = yf_ref[...] + yb_ref[...] + d_ref[...] * x_ref[...].astype(F32)
    g = y * (z * jax.nn.sigmoid(z))
    o = g * lax.rsqrt(jnp.mean(g * g, axis=-1, keepdims=True) + EPS) * w_ref[...]
    o_ref[...] = o.astype(o_ref.dtype)


def ssd_gate(y_f, y_b, xbc, p_zx, d_skip, norm_w):
    rows = y_f.shape[0]
    t, gw = ROW_BLOCK, SSD_GROUP_W
    d_cols = jnp.repeat(d_skip.astype(F32), SSD_HEAD_DIM).reshape(1, SSD_INNER)
    spec = pl.BlockSpec((t, gw), lambda i, g: (i, g))
    vec = pl.BlockSpec((1, gw), lambda i, g: (0, g))
    return pl.pallas_call(
        _ssd_gate_kernel,
        grid=(rows // t, SSD_GROUPS),
        in_specs=[spec, spec, spec, spec, vec, vec],
        out_specs=spec,
        out_shape=jax.ShapeDtypeStruct((rows, SSD_INNER), BF16),
        compiler_params=_cparams("parallel", "parallel"),
        name="ssd_gate",
    )(y_f, y_b, xbc, p_zx, d_cols, norm_w.reshape(1, SSD_INNER))


def _head_norm_kernel(x_ref, w_ref, *rest, scale, rope):
    if rope:
        cos_ref, sin_ref, o_ref = rest
        even = lax.broadcasted_iota(jnp.int32, (1, HEAD_DIM), 1) % 2 == 0
    else:
        (o_ref,) = rest
    for hd in range(x_ref.shape[1] // HEAD_DIM):
        cols = slice(hd * HEAD_DIM, (hd + 1) * HEAD_DIM)
        x = x_ref[:, cols].astype(F32)
        y = x * lax.rsqrt(jnp.mean(x * x, axis=-1, keepdims=True) + EPS) * w_ref[:, cols]
        if rope:
            partner = jnp.where(even, pltpu.roll(y, HEAD_DIM - 1, 1), pltpu.roll(y, 1, 1))
            y = y * cos_ref[...] + partner * sin_ref[...]
        if scale != 1.0:
            y = y * scale
        o_ref[:, cols] = y.astype(o_ref.dtype)


def head_norm(p, col_off, width, w_cols, n_rows, row_off=0, scale=1.0, rope=None, seq=None):
    t, tc = ROW_BLOCK, min(width, 1024)
    r_off, c_off = row_off // t, col_off // tc
    in_specs = [pl.BlockSpec((t, tc), lambda i, j: (i + r_off, j + c_off)),
                pl.BlockSpec((1, tc), lambda i, j: (0, j))]
    args = [p, w_cols]
    if rope is not None:
        per_seq = seq // t
        tab = pl.BlockSpec((t, HEAD_DIM), lambda i, j: (i % per_seq, 0))
        in_specs += [tab, tab]
        args += list(rope)
    return pl.pallas_call(
        functools.partial(_head_norm_kernel, scale=scale, rope=rope is not None),
        grid=(n_rows // t, width // tc),
        in_specs=in_specs,
        out_specs=pl.BlockSpec((t, tc), lambda i, j: (i, j)),
        out_shape=jax.ShapeDtypeStruct((n_rows, width), BF16),
        compiler_params=_cparams("parallel", "parallel"),
        name="head_norm",
    )(*args)


def _rope_tables(seq):
    t = jnp.arange(seq, dtype=jnp.int32)
    row = (t // GRID_W).astype(F32)
    col = (t % GRID_W).astype(F32)
    axis_dim = HEAD_DIM // 2
    inv = 1.0 / (ROPE_THETA ** (jnp.arange(0, axis_dim, 2, dtype=F32) / axis_dim))
    ang = jnp.concatenate([row[:, None] * inv[None], col[:, None] * inv[None]], axis=-1)
    cos = jnp.repeat(jnp.cos(ang), 2, axis=-1)
    sin = jnp.stack([-jnp.sin(ang), jnp.sin(ang)], axis=-1).reshape(seq, HEAD_DIM)
    return cos, sin


def _attend(q, pieces):
    scores = []
    for k, _, bias in pieces:
        s = lax.dot_general(q, k, (((1,), (1,)), ((), ())), preferred_element_type=F32)
        scores.append(s if bias is None else s + bias)
    m = functools.reduce(jnp.maximum, [jnp.max(s, axis=-1, keepdims=True) for s in scores])
    denom, out = None, None
    for s, (_, v, _) in zip(scores, pieces):
        p = jnp.exp(s - m)
        ps = jnp.sum(p, axis=-1, keepdims=True)
        pv = jnp.dot(p.astype(v.dtype), v, preferred_element_type=F32)
        denom = ps if denom is None else denom + ps
        out = pv if out is None else out + pv
    return out / denom


def _na_kernel(q_ref, k0_ref, k1_ref, k2_ref, kc_ref, v0_ref, v1_ref, v2_ref, vc_ref, bias_ref, o_ref):
    t = q_ref.shape[0]
    pieces = [(k_ref[...], v_ref[...], bias_ref[:, i * t:(i + 1) * t])
              for i, (k_ref, v_ref) in enumerate(((k0_ref, v0_ref), (k1_ref, v1_ref), (k2_ref, v2_ref)))]
    pieces.append((kc_ref[...], vc_ref[...], None))
    o_ref[...] = _attend(q_ref[...], pieces).astype(o_ref.dtype)


def _na_bias_table(rpb, rows):
    n_blocks = rows // NA_QROWS
    w0_of = lambda j: min(max(NA_QROWS * j - NA_WIN_ROWS // 2, 0), rows - NA_KROWS)
    geoms, pattern_of = [], []
    for j in range(n_blocks):
        r0s = tuple(min(max(NA_QROWS * j + a - NA_WIN_ROWS // 2, 0), rows - NA_WIN_ROWS) - w0_of(j)
                    for a in range(NA_QROWS))
        key = (NA_QROWS * j - w0_of(j), r0s)
        if key not in geoms:
            geoms.append(key)
        pattern_of.append(geoms.index(key))
    a = jnp.arange(NA_QROWS * GRID_W)
    c = jnp.arange(NA_KROWS * GRID_W)
    qc, kr, kc = a % GRID_W, c // GRID_W, c % GRID_W
    cs = jnp.clip(qc - NA_WIN_COLS // 2, 0, GRID_W - NA_WIN_COLS)
    col_ok = (kc[None, :] >= cs[:, None]) & (kc[None, :] < cs[:, None] + NA_WIN_COLS)
    dc = jnp.clip(kc[None, :] - qc[:, None] + NA_WIN_COLS - 1, 0, 2 * NA_WIN_COLS - 2)
    tables = []
    for q_row0, r0s in geoms:
        qr = q_row0 + a // GRID_W
        r0 = jnp.asarray(r0s)[a // GRID_W]
        ok = col_ok & (kr[None, :] >= r0[:, None]) & (kr[None, :] < r0[:, None] + NA_WIN_ROWS)
        dr = jnp.clip(kr[None, :] - qr[:, None] + NA_WIN_ROWS - 1, 0, 2 * NA_WIN_ROWS - 2)
        tables.append(jnp.where(ok[None], rpb[:, dr, dc].astype(F32), MASK_VALUE))
    tables.append(jnp.full_like(tables[0], MASK_VALUE))
    pattern_of.append(len(geoms))
    w0_blocks = [w0_of(j) // NA_QROWS for j in range(n_blocks)] + [0]
    return jnp.stack(tables, axis=1), pattern_of, w0_blocks


def _lookup(values, j):
    out = jnp.int32(values[-1])
    for idx in range(len(values) - 2, -1, -1):
        out = jnp.where(j == idx, jnp.int32(values[idx]), out)
    return out


def na_attention(qn, kn, p_na, rpb, n_batch, seq, ctx_len):
    t = NA_QROWS * GRID_W
    assert t == ROW_BLOCK == ctx_len
    rows = seq // GRID_W
    n_blocks = seq // t
    bias, pattern_of, w0_blocks = _na_bias_table(rpb, rows)
    v_col0 = 2 * NA_WIDTH // HEAD_DIM
    ctx_base = n_batch * n_blocks

    def q_blk(j, b):
        return jnp.where(j < n_blocks, b * n_blocks + j, ctx_base + b)

    def kv_blk(piece):
        return lambda h, j, b: (b * n_blocks + _lookup(w0_blocks, j) + piece, h)

    def kv_blk_v(piece):
        return lambda h, j, b: (b * n_blocks + _lookup(w0_blocks, j) + piece, v_col0 + h)

    blk = lambda fn: pl.BlockSpec((t, HEAD_DIM), fn)
    in_specs = ([blk(lambda h, j, b: (q_blk(j, b), h))]
                + [blk(kv_blk(i)) for i in range(3)] + [blk(lambda h, j, b: (ctx_base + b, h))]
                + [blk(kv_blk_v(i)) for i in range(3)] + [blk(lambda h, j, b: (ctx_base + b, v_col0 + h))]
                + [pl.BlockSpec((None, None, t, 3 * t), lambda h, j, b: (h, _lookup(pattern_of, j), 0, 0))])
    return pl.pallas_call(
        _na_kernel,
        grid=(NA_HEADS, n_blocks + 1, n_batch),
        in_specs=in_specs,
        out_specs=blk(lambda h, j, b: (q_blk(j, b), h)),
        out_shape=jax.ShapeDtypeStruct((qn.shape[0], NA_WIDTH), BF16),
        compiler_params=_cparams("parallel", "parallel", "parallel"),
        name="na_attention",
    )(qn, kn, kn, kn, kn, p_na, p_na, p_na, p_na, bias)


def _gqa_kernel(q_ref, kl_ref, kc_ref, vl_ref, vc_ref, o_ref):
    t = q_ref.shape[0]
    q = jnp.concatenate([q_ref[:, g * HEAD_DIM:(g + 1) * HEAD_DIM] for g in range(GQA_GROUP)], axis=0)
    o = _attend(q, [(kc_ref[...], vc_ref[...], None), (kl_ref[...], vl_ref[...], None)])
    for g in range(GQA_GROUP):
        o_ref[:, g * HEAD_DIM:(g + 1) * HEAD_DIM] = o[g * t:(g + 1) * t].astype(o_ref.dtype)


def gqa_attention(qn, kn, p_kv, v_col_off, n_batch, seq, ctx_len):
    t = ROW_BLOCK
    q_tiles = seq // t
    ctx_base = n_batch * seq // ctx_len
    v0 = v_col_off // HEAD_DIM
    gw = GQA_GROUP * HEAD_DIM
    return pl.pallas_call(
        _gqa_kernel,
        grid=(n_batch, GQA_KV_HEADS, q_tiles),
        in_specs=[pl.BlockSpec((t, gw), lambda b, h, i: (b * q_tiles + i, h)),
                  pl.BlockSpec((seq, HEAD_DIM), lambda b, h, i: (b, h)),
                  pl.BlockSpec((ctx_len, HEAD_DIM), lambda b, h, i: (ctx_base + b, h)),
                  pl.BlockSpec((seq, HEAD_DIM), lambda b, h, i: (b, v0 + h)),
                  pl.BlockSpec((ctx_len, HEAD_DIM), lambda b, h, i: (ctx_base + b, v0 + h))],
        out_specs=pl.BlockSpec((t, gw), lambda b, h, i: (b * q_tiles + i, h)),
        out_shape=jax.ShapeDtypeStruct((n_batch * seq, GQA_Q), BF16),
        compiler_params=_cparams("parallel", "parallel", "parallel"),
        name="gqa_attention",
    )(qn, kn, kn, p_kv, p_kv)


def _diff_kernel(lam_ref, q_ref, kl_ref, kc_ref, vl_ref, vc_ref, w_ref, o_ref, *, out_scale):
    outs = []
    for comp in range(2):
        cols = slice(comp * HEAD_DIM, (comp + 1) * HEAD_DIM)
        outs.append(_attend(q_ref[:, cols], [(kc_ref[:, cols], vc_ref[...], None),
                                              (kl_ref[:, cols], vl_ref[...], None)]))
    o = outs[0] - lam_ref[0] * outs[1]
    o = o * lax.rsqrt(jnp.mean(o * o, axis=-1, keepdims=True) + EPS) * w_ref[...]
    o_ref[...] = (o * out_scale).astype(o_ref.dtype)


def diff_attention(qn, q_col_off, kn, p_kv, v_col_off, lam, subln_w, out_scale, n_batch, seq, ctx_len):
    t = ROW_BLOCK
    q_tiles = seq // t
    ctx_base = n_batch * seq // ctx_len
    v0 = v_col_off // DIFF_V_DIM
    w2 = 2 * HEAD_DIM
    q0 = q_col_off // w2
    return pl.pallas_call(
        functools.partial(_diff_kernel, out_scale=out_scale),
        grid=(n_batch, DIFF_HEADS, q_tiles),
        in_specs=[pl.BlockSpec(memory_space=pltpu.SMEM),
                  pl.BlockSpec((t, w2), lambda b, h, i: (b * q_tiles + i, q0 + h)),
                  pl.BlockSpec((seq, w2), lambda b, h, i: (b, h)),
                  pl.BlockSpec((ctx_len, w2), lambda b, h, i: (ctx_base + b, h)),
                  pl.BlockSpec((seq, DIFF_V_DIM), lambda b, h, i: (b, v0 + h)),
                  pl.BlockSpec((ctx_len, DIFF_V_DIM), lambda b, h, i: (ctx_base + b, v0 + h)),
                  pl.BlockSpec((1, DIFF_V_DIM), lambda b, h, i: (0, 0))],
        out_specs=pl.BlockSpec((t, DIFF_V_DIM), lambda b, h, i: (b * q_tiles + i, h)),
        out_shape=jax.ShapeDtypeStruct((n_batch * seq, DIFF_V), BF16),
        compiler_params=_cparams("parallel", "parallel", "parallel"),
        name="diff_attention",
    )(lam.reshape(1), qn, kn, kn, p_kv, p_kv, subln_w.reshape(1, DIFF_V_DIM))


def _moe_kernel(be_ref, nused_ref, x_ref, g_ref, w1_ref, w3_ref, w2_ref, o_ref, w1b, w3b, w2b):
    i = pl.program_id(0)
    prev = be_ref[jnp.maximum(i - 1, 0)]

    @pl.when((i == 0) | (be_ref[i] != prev))
    def _():
        w1b[...] = w1_ref[...].astype(BF16)
        w3b[...] = w3_ref[...].astype(BF16)
        w2b[...] = w2_ref[...].astype(BF16)

    @pl.when(i < nused_ref[0])
    def _():
        x = x_ref[...]
        a = jnp.dot(x, w1b[...], preferred_element_type=F32)
        b = jnp.dot(x, w3b[...], preferred_element_type=F32)
        hid = (a * jax.nn.sigmoid(a)) * b
        y = jnp.dot(hid.astype(BF16), w2b[...], preferred_element_type=F32)
        o_ref[...] = y * g_ref[...]

    @pl.when(i >= nused_ref[0])
    def _():
        o_ref[...] = jnp.zeros_like(o_ref)


def moe_experts(x_sorted, gate_sorted, block_e, n_used, w1, w3, w2):
    n_blocks = block_e.shape[0]
    d, hid = w1.shape[1], w1.shape[2]
    grid_spec = pltpu.PrefetchScalarGridSpec(
        num_scalar_prefetch=2,
        grid=(n_blocks,),
        in_specs=[pl.BlockSpec((MOE_BLOCK, d), lambda i, be, nu: (i, 0)),
                  pl.BlockSpec((MOE_BLOCK, 1), lambda i, be, nu: (i, 0)),
                  pl.BlockSpec((None, d, hid), lambda i, be, nu: (be[i], 0, 0)),
                  pl.BlockSpec((None, d, hid), lambda i, be, nu: (be[i], 0, 0)),
                  pl.BlockSpec((None, hid, d), lambda i, be, nu: (be[i], 0, 0))],
        out_specs=pl.BlockSpec((MOE_BLOCK, d), lambda i, be, nu: (i, 0)),
        scratch_shapes=[pltpu.VMEM((d, hid), BF16), pltpu.VMEM((d, hid), BF16), pltpu.VMEM((hid, d), BF16)],
    )
    return pl.pallas_call(
        _moe_kernel,
        grid_spec=grid_spec,
        out_shape=jax.ShapeDtypeStruct((n_blocks * MOE_BLOCK, d), F32),
        compiler_params=_cparams("arbitrary"),
        name="moe_experts",
    )(block_e, n_used, x_sorted, gate_sorted, w1, w3, w2)


def _route(logits):
    n = logits.shape[0]
    g_logits = logits[:, :MOE_GROUPS]
    g_prob = jax.nn.softmax(g_logits, axis=-1)
    _, g_idx = lax.top_k(g_logits, 1)
    g_gate = jnp.take_along_axis(g_prob, g_idx, axis=-1)
    e_logits = logits[:, MOE_GROUPS:MOE_GROUPS + MOE_EXPERTS].reshape(n, MOE_GROUPS, MOE_EXPERTS_PER_GROUP)
    e_in_g = jnp.take_along_axis(e_logits, g_idx[:, :, None], axis=1)[:, 0]
    top_v, top_i = lax.top_k(e_in_g, MOE_TOPK)
    gate = jax.nn.softmax(top_v, axis=-1) * g_gate
    return g_idx * MOE_EXPERTS_PER_GROUP + top_i, gate


def _dispatch_plan(e_idx, gate):
    n = e_idx.shape[0]
    n_assign = n * MOE_TOPK
    flat_e = e_idx.reshape(-1).astype(jnp.int32)
    counts = jnp.sum(flat_e[:, None] == jnp.arange(MOE_EXPERTS, dtype=jnp.int32)[None, :], axis=0, dtype=jnp.int32)
    padded = (counts + MOE_BLOCK - 1) // MOE_BLOCK * MOE_BLOCK
    pad_end = jnp.cumsum(padded)
    pad_start = pad_end - padded
    raw_start = jnp.cumsum(counts) - counts
    order = jnp.argsort(flat_e)
    se = flat_e[order]
    dest_sorted = pad_start[se] + jnp.arange(n_assign, dtype=jnp.int32) - raw_start[se]
    n_blocks = -(-n_assign // MOE_BLOCK) + MOE_EXPERTS
    slots = n_blocks * MOE_BLOCK
    slot_tok = jnp.zeros((slots,), jnp.int32).at[dest_sorted].set((order // MOE_TOPK).astype(jnp.int32))
    slot_gate = jnp.zeros((slots,), F32).at[dest_sorted].set(gate.reshape(-1)[order])
    pos = jnp.zeros((n_assign,), jnp.int32).at[order].set(dest_sorted)
    n_used = (pad_end[-1] // MOE_BLOCK).astype(jnp.int32)
    blk = jnp.minimum(jnp.arange(n_blocks, dtype=jnp.int32), n_used - 1) * MOE_BLOCK
    block_e = jnp.minimum(jnp.searchsorted(pad_end, blk, side='right'), MOE_EXPERTS - 1).astype(jnp.int32)
    return slot_tok, slot_gate, pos.reshape(n, MOE_TOPK), block_e, n_used.reshape(1)


def _combine_kernel(h_ref, g_ref, y0_ref, y1_ref, o_ref):
    o_ref[...] = h_ref[...] + g_ref[...] * (y0_ref[...] + y1_ref[...])


def moe_combine(h, mod3, gate_chunk, y0, y1, geom):
    n_lat, seq, n_batch = geom
    n_rows, d = y0.shape
    t = ROW_BLOCK
    row = functools.partial(_mod_row, tile=t, n_lat=n_lat, seq=seq, n_batch=n_batch)
    spec = pl.BlockSpec((t, d), lambda i: (i, 0))
    return pl.pallas_call(
        _combine_kernel,
        grid=(n_rows // t,),
        in_specs=[spec, pl.BlockSpec((None, 1, d), lambda i: (row(i), 0, gate_chunk)), spec, spec],
        out_specs=spec,
        out_shape=jax.ShapeDtypeStruct((n_rows, d), F32),
        compiler_params=_cparams("parallel"),
        name="moe_combine",
    )(h, mod3, y0, y1)


def moe_layer(h, f, logits, mod3, gate_chunk, w1, w3, w2, geom):
    e_idx, gate = _route(logits)
    slot_tok, slot_gate, pos, block_e, n_used = _dispatch_plan(e_idx, gate)
    x_sorted = jnp.take(f, slot_tok, axis=0)
    yb = moe_experts(x_sorted, slot_gate[:, None], block_e, n_used, w1, w3, w2)
    y0 = jnp.take(yb, pos[:, 0], axis=0)
    y1 = jnp.take(yb, pos[:, 1], axis=0)
    return moe_combine(h, mod3, gate_chunk, y0, y1, geom)


def _even_layer_mix(a, in_w, conv_w, conv_b, dt_bias, a_log, d_skip, ssd_norm_w, q_norm, k_norm, rpb,
                    n_batch, seq, ctx_len):
    n_rows = a.shape[0]
    n_lat = n_batch * seq
    zx_end = SSD_INNER + SSD_XBC
    dt_end = zx_end + 2 * SSD_HEADS
    w_zx = in_w[:, :zx_end].astype(BF16)
    w_dt = jnp.pad(in_w[:, zx_end:dt_end], ((0, 0), (0, LANES - 2 * SSD_HEADS))).astype(BF16)
    w_na = in_w[:, dt_end:].astype(BF16)
    p_zx = matmul([a], [w_zx], n_rows, BF16, tn=512)
    p_dt = matmul([a], [w_dt], n_rows, F32, tn=LANES)
    p_na = matmul([a], [w_na], n_rows, BF16, tn=512)

    xbc = jnp.concatenate([conv_silu(p_zx, conv_w, conv_b, n_batch, seq, 0),
                           conv_silu(p_zx, conv_w, conv_b, n_batch, ctx_len, n_lat)], axis=0)
    dt, acs, acs_t = ssd_dt(p_dt, dt_bias, a_log)
    hg = SSD_HEADS // SSD_GROUPS
    col = lambda v: v[:, :2 * SSD_HEADS].reshape(n_rows, 2, SSD_GROUPS, hg).transpose(1, 2, 0, 3)
    dt_col, acs_col = col(dt), col(acs)
    acs_row = acs_t[:2 * SSD_HEADS].reshape(2, SSD_GROUPS, hg, n_rows)
    y_f = ssd_scan(xbc, dt_col[0], acs_col[0], acs_row[0], n_batch, seq, ctx_len, reverse=False)
    y_b = ssd_scan(xbc, dt_col[1], acs_col[1], acs_row[1], n_batch, seq, ctx_len, reverse=True)
    s_out = ssd_gate(y_f, y_b, xbc, p_zx, d_skip, ssd_norm_w)

    tile_w = lambda w: jnp.tile(w.astype(F32), NA_HEADS).reshape(1, NA_WIDTH)
    qn = head_norm(p_na, 0, NA_WIDTH, tile_w(q_norm), n_rows, scale=HEAD_DIM ** -0.5)
    kn = head_norm(p_na, NA_WIDTH, NA_WIDTH, tile_w(k_norm), n_rows)
    n_out = na_attention(qn, kn, p_na, rpb, n_batch, seq, ctx_len)
    return s_out, n_out


def _odd_layer_mix(a, in_w, gq_norm, gk_norm, dq_norm, dk_norm, lam_vecs, subln_w, lambda_init,
                   n_batch, seq, ctx_len):
    n_rows = a.shape[0]
    n_lat = n_batch * seq
    p_q = matmul([a], [in_w[:, :ODD_Q].astype(BF16)], n_lat, BF16, tn=512)
    p_kv = matmul([a], [in_w[:, ODD_Q:].astype(BF16)], n_rows, BF16, tn=512)
    rope = _rope_tables(seq)
    scale = HEAD_DIM ** -0.5
    rep = lambda w, n: jnp.tile(w.astype(F32), n).reshape(1, n * HEAD_DIM)
    q_w = jnp.concatenate([rep(gq_norm, GQA_HEADS), rep(dq_norm, 2 * DIFF_HEADS)], axis=1)
    qn = head_norm(p_q, 0, ODD_Q, q_w, n_lat, scale=scale, rope=rope, seq=seq)
    dk_off = 2 * GQA_KV
    dv_off = dk_off + DIFF_QK
    kv_norm = lambda off, width, w, n: jnp.concatenate(
        [head_norm(p_kv, off, width, rep(w, n), n_lat, rope=rope, seq=seq),
         head_norm(p_kv, off, width, rep(w, n), n_rows - n_lat, row_off=n_lat)], axis=0)
    gkn = kv_norm(0, GQA_KV, gk_norm, GQA_KV_HEADS)
    dkn = kv_norm(dk_off, DIFF_QK, dk_norm, 2 * DIFF_HEADS)
    lv = lam_vecs.astype(F32)
    lam = jnp.exp(jnp.dot(lv[0], lv[1])) - jnp.exp(jnp.dot(lv[2], lv[3])) + lambda_init
    og = gqa_attention(qn, gkn, p_kv, GQA_KV, n_batch, seq, ctx_len)
    od = diff_attention(qn, GQA_Q, dkn, p_kv, dv_off, lam, subln_w, 1.0 - lambda_init,
                        n_batch, seq, ctx_len)
    return og, od


def kernel(x, c, ctx, c_ctx, ada_w, ada_b, norm_w, ev_in_w, ev_conv_w, ev_conv_b, ev_dt_bias, ev_a_log, ev_d_skip, ev_ssd_norm_w, ev_na_q_norm, ev_na_k_norm, ev_na_rpb, ev_out_w, od_in_w, od_gqa_q_norm, od_gqa_k_norm, od_diff_q_norm, od_diff_k_norm, od_lambda, od_diff_subln, od_out_w, moe_group_w, moe_expert_w, moe_w1, moe_w3, moe_w2):
    n_batch, seq, d = x.shape
    ctx_len = ctx.shape[1]
    depth = ada_w.shape[0]
    assert depth == 2 and d == D_MODEL
    n_lat, n_ctx = n_batch * seq, n_batch * ctx_len
    n_all = n_lat + n_ctx
    geom = (n_lat, seq, n_batch)

    cc = jnp.concatenate([c, c_ctx[None, :], jnp.zeros((8 - n_batch - 1, d), F32)], axis=0)
    mod = ada_modulation(cc, ada_w, ada_b)
    router_w = lambda l: jnp.pad(jnp.concatenate([moe_group_w[l], moe_expert_w[l]], axis=1),
                                 ((0, 0), (0, LANES - MOE_GROUPS - MOE_EXPERTS)))

    h = jnp.concatenate([x.reshape(n_lat, d), ctx.reshape(n_ctx, d)], axis=0)

    mod3 = mod[0].reshape(8, 1, 6 * d)
    a = norm_modulate(h, norm_w[0, 0], mod3, 1, 0, n_all, geom)
    s_out, n_out = _even_layer_mix(a, ev_in_w[0], ev_conv_w[0], ev_conv_b[0], ev_dt_bias[0], ev_a_log[0],
                                   ev_d_skip[0], ev_ssd_norm_w[0], ev_na_q_norm[0], ev_na_k_norm[0],
                                   ev_na_rpb[0], n_batch, seq, ctx_len)
    out_w = ev_out_w[0].astype(BF16)
    h = matmul([s_out, n_out], [out_w[:SSD_INNER], out_w[SSD_INNER:]], n_all, F32, tn=512,
               residual=(h, mod3, 2, geom))
    f, logits = norm_modulate(h, norm_w[0, 1], mod3, 4, 3, n_all, geom, router_w=router_w(0))
    h = moe_layer(h, f, logits, mod3, 5, moe_w1[0], moe_w3[0], moe_w2[0], geom)

    mod3 = mod[1].reshape(8, 1, 6 * d)
    a = norm_modulate(h, norm_w[1, 0], mod3, 1, 0, n_all, geom)
    lambda_init = 0.8 - 0.6 * math.exp(-0.3 * 1)
    og, od = _odd_layer_mix(a, od_in_w[0], od_gqa_q_norm[0], od_gqa_k_norm[0], od_diff_q_norm[0],
                            od_diff_k_norm[0], od_lambda[0], od_diff_subln[0], lambda_init,
                            n_batch, seq, ctx_len)
    out_w = od_out_w[0].astype(BF16)
    h = matmul([og, od], [out_w[:GQA_Q], out_w[GQA_Q:]], n_lat, F32, tn=512,
               residual=(h, mod3, 2, geom))
    f, logits = norm_modulate(h, norm_w[1, 1], mod3, 4, 3, n_lat, geom, router_w=router_w(1))
    h = moe_layer(h, f, logits, mod3, 5, moe_w1[1], moe_w3[1], moe_w2[1], geom)
    return h.reshape(n_batch, seq, d)
```

```python
import functools
import math

import jax
import jax.numpy as jnp
from jax import lax
from jax.experimental import pallas as pl
from jax.experimental.pallas import tpu as pltpu

F32 = jnp.float32
BF16 = jnp.bfloat16
HIGHEST = lax.Precision.HIGHEST

D_MODEL = 2048
GRID_W = 64
EPS = 1e-6
ROPE_THETA = 10000.0
HEAD_DIM = 128

SSD_HEADS = 32
SSD_HEAD_DIM = 64
SSD_INNER = SSD_HEADS * SSD_HEAD_DIM
SSD_GROUPS = 4
SSD_STATE = 128
SSD_GN = SSD_GROUPS * SSD_STATE
SSD_XBC = SSD_INNER + 2 * SSD_GN
SSD_CONV = 5
SSD_CHUNK = 128
SSD_GROUP_W = SSD_INNER // SSD_GROUPS

NA_HEADS = 16
NA_WIDTH = NA_HEADS * HEAD_DIM
NA_WIN_ROWS = 8
NA_WIN_COLS = 16
NA_QROWS = 4
NA_KROWS = 12

GQA_HEADS = 16
GQA_KV_HEADS = 4
GQA_GROUP = GQA_HEADS // GQA_KV_HEADS
GQA_Q = GQA_HEADS * HEAD_DIM
GQA_KV = GQA_KV_HEADS * HEAD_DIM
DIFF_HEADS = 8
DIFF_QK = DIFF_HEADS * 2 * HEAD_DIM
DIFF_V_DIM = 2 * HEAD_DIM
DIFF_V = DIFF_HEADS * DIFF_V_DIM
ODD_Q = GQA_Q + DIFF_QK

MOE_GROUPS = 8
MOE_EXPERTS_PER_GROUP = 8
MOE_EXPERTS = MOE_GROUPS * MOE_EXPERTS_PER_GROUP
MOE_HIDDEN = 512
MOE_TOPK = 2
MOE_BLOCK = 128

ROW_BLOCK = 256
LANES = 128
MASK_VALUE = -1e30
VMEM_LIMIT = 52 * 1024 * 1024


def _cparams(*sem):
    return pltpu.CompilerParams(dimension_semantics=sem, vmem_limit_bytes=VMEM_LIMIT)


def _mod_row(i, tile, n_lat, seq, n_batch):
    return jnp.where(i < n_lat // tile, i // (seq // tile), n_batch)


def _ada_kernel(c_ref, w_ref, b_ref, o_ref):
    c = c_ref[...]
    s = c * jax.nn.sigmoid(c)
    o_ref[...] = jnp.dot(s, w_ref[...], precision=HIGHEST, preferred_element_type=F32) + b_ref[...]


def ada_modulation(cc, ada_w, ada_b):
    n_layers, d, n_out = ada_w.shape
    rows = cc.shape[0]
    tn = 1024
    return pl.pallas_call(
        _ada_kernel,
        grid=(n_layers, n_out // tn),
        in_specs=[pl.BlockSpec((rows, d), lambda l, j: (0, 0)),
                  pl.BlockSpec((None, d, tn), lambda l, j: (l, 0, j)),
                  pl.BlockSpec((None, 1, tn), lambda l, j: (l, 0, j))],
        out_specs=pl.BlockSpec((None, rows, tn), lambda l, j: (l, 0, j)),
        out_shape=jax.ShapeDtypeStruct((n_layers, rows, n_out), F32),
        compiler_params=_cparams("parallel", "parallel"),
        name="ada_modulation",
    )(cc, ada_w, ada_b.reshape(n_layers, 1, n_out))


def _normmod_kernel(h_ref, w_ref, sc_ref, sh_ref, *rest, with_router):
    x = h_ref[...]
    y = x * lax.rsqrt(jnp.mean(x * x, axis=-1, keepdims=True) + EPS) * w_ref[...]
    a = y * (1.0 + sc_ref[...]) + sh_ref[...]
    if with_router:
        rw_ref, o_ref, route_ref = rest
        logits = jnp.dot(a, rw_ref[...], precision=HIGHEST, preferred_element_type=F32)
        route_ref[...] = _route(logits)
    else:
        (o_ref,) = rest
    o_ref[...] = a.astype(o_ref.dtype)


ROUTE_EXPERT_LANE = 0
ROUTE_GATE_LANE = MOE_TOPK


def _route(logits):
    assert MOE_TOPK == 2
    neg = jnp.float32(-3.0e38)
    lane = lax.broadcasted_iota(jnp.int32, logits.shape, 1)
    is_group = lane < MOE_GROUPS
    gl = jnp.where(is_group, logits, neg)
    g_max = jnp.max(gl, axis=-1, keepdims=True)
    g_idx = jnp.min(jnp.where(gl == g_max, lane, LANES), axis=-1, keepdims=True)
    g_gate = 1.0 / jnp.sum(jnp.where(is_group, jnp.exp(logits - g_max), 0.0), axis=-1, keepdims=True)
    first = MOE_GROUPS + g_idx * MOE_EXPERTS_PER_GROUP
    in_group = (lane >= first) & (lane < first + MOE_EXPERTS_PER_GROUP)
    el = jnp.where(in_group, logits, neg)
    v1 = jnp.max(el, axis=-1, keepdims=True)
    i1 = jnp.min(jnp.where(in_group & (el == v1), lane, LANES), axis=-1, keepdims=True)
    rest = in_group & (lane != i1)
    el2 = jnp.where(rest, logits, neg)
    v2 = jnp.max(el2, axis=-1, keepdims=True)
    i2 = jnp.min(jnp.where(rest & (el2 == v2), lane, LANES), axis=-1, keepdims=True)
    e21 = jnp.exp(v2 - v1)
    p1 = 1.0 / (1.0 + e21)
    p2 = e21 / (1.0 + e21)
    vals = [(i1 - MOE_GROUPS).astype(F32), (i2 - MOE_GROUPS).astype(F32), p1 * g_gate, p2 * g_gate]
    out = jnp.zeros(logits.shape, F32)
    for k, v in enumerate(vals):
        out = jnp.where(lane == k, v, out)
    return out


def norm_modulate(h, w, mod3, sc_chunk, sh_chunk, n_rows, geom, router_w=None):
    n_lat, seq, n_batch = geom
    d = h.shape[1]
    t = ROW_BLOCK
    row = functools.partial(_mod_row, tile=t, n_lat=n_lat, seq=seq, n_batch=n_batch)
    in_specs = [pl.BlockSpec((t, d), lambda i: (i, 0)),
                pl.BlockSpec((1, d), lambda i: (0, 0)),
                pl.BlockSpec((None, 1, d), lambda i: (row(i), 0, sc_chunk)),
                pl.BlockSpec((None, 1, d), lambda i: (row(i), 0, sh_chunk))]
    args = [h, w.reshape(1, d), mod3, mod3]
    out_specs = pl.BlockSpec((t, d), lambda i: (i, 0))
    out_shape = jax.ShapeDtypeStruct((n_rows, d), BF16)
    if router_w is not None:
        in_specs.append(pl.BlockSpec(router_w.shape, lambda i: (0, 0)))
        args.append(router_w)
        out_specs = [out_specs, pl.BlockSpec((t, router_w.shape[1]), lambda i: (i, 0))]
        out_shape = [jax.ShapeDtypeStruct((n_rows, d), F32),
                     jax.ShapeDtypeStruct((n_rows, router_w.shape[1]), F32)]
    return pl.pallas_call(
        functools.partial(_normmod_kernel, with_router=router_w is not None),
        grid=(n_rows // t,),
        in_specs=in_specs, out_specs=out_specs, out_shape=out_shape,
        compiler_params=_cparams("parallel"),
        name="norm_modulate",
    )(*args)


def _matmul_kernel(*refs, n_in, residual):
    acc = None
    for x_ref, w_ref in zip(refs[:n_in], refs[n_in:2 * n_in]):
        part = jnp.dot(x_ref[...], w_ref[...], preferred_element_type=F32)
        acc = part if acc is None else acc + part
    if residual:
        h_ref, g_ref, o_ref = refs[2 * n_in:]
        o_ref[...] = h_ref[...] + g_ref[...] * acc
    else:
        (o_ref,) = refs[2 * n_in:]
        o_ref[...] = acc.astype(o_ref.dtype)


def matmul(xs, ws, n_rows, out_dtype, tn, row_off=0, residual=None):
    n_out = ws[0].shape[1]
    tm = next(t for t in (1024, 512, ROW_BLOCK) if n_rows % t == 0 and row_off % t == 0)
    off = row_off // tm
    in_specs = ([pl.BlockSpec((tm, x.shape[1]), lambda i, j: (i + off, 0)) for x in xs]
                + [pl.BlockSpec((w.shape[0], tn), lambda i, j: (0, j)) for w in ws])
    args = list(xs) + list(ws)
    if residual is not None:
        h, mod3, gate_chunk, (n_lat, seq, n_batch) = residual
        row = functools.partial(_mod_row, tile=tm, n_lat=n_lat, seq=seq, n_batch=n_batch)
        nj = mod3.shape[2] // 6 // tn
        in_specs += [pl.BlockSpec((tm, tn), lambda i, j: (i, j)),
                     pl.BlockSpec((None, 1, tn), lambda i, j: (row(i), 0, gate_chunk * nj + j))]
        args += [h, mod3]
    return pl.pallas_call(
        functools.partial(_matmul_kernel, n_in=len(xs), residual=residual is not None),
        grid=(n_rows // tm, n_out // tn),
        in_specs=in_specs,
        out_specs=pl.BlockSpec((tm, tn), lambda i, j: (i, j)),
        out_shape=jax.ShapeDtypeStruct((n_rows, n_out), out_dtype),
        compiler_params=_cparams("parallel", "parallel"),
        name="matmul_residual" if residual is not None else "matmul",
    )(*args)


CONV_PAD = 8


def _conv_silu_kernel(x_ref, w_ref, b_ref, o_ref, pad_ref):
    seq, width = o_ref.shape
    zeros = jnp.zeros((CONV_PAD, width), F32)
    pad_ref[pl.ds(0, CONV_PAD), :] = zeros
    pad_ref[pl.ds(CONV_PAD + seq, CONV_PAD), :] = zeros
    pad_ref[pl.ds(CONV_PAD, seq), :] = x_ref[...].astype(F32)
    rows = 256
    for base in range(0, seq, rows):
        acc = jnp.broadcast_to(b_ref[...], (rows, width))
        for k in range(SSD_CONV):
            acc = acc + w_ref[pl.ds(k, 1), :] * pad_ref[pl.ds(base + CONV_PAD - SSD_CONV // 2 + k, rows), :]
        o_ref[pl.ds(base, rows), :] = (acc * jax.nn.sigmoid(acc)).astype(o_ref.dtype)


def conv_silu(p_zx, conv_w, conv_b, n_seq, seq, row_off):
    tc = 512
    col0 = SSD_INNER // tc
    off = row_off // seq
    return pl.pallas_call(
        _conv_silu_kernel,
        grid=(n_seq, SSD_XBC // tc),
        in_specs=[pl.BlockSpec((seq, tc), lambda b, j: (b + off, col0 + j)),
                  pl.BlockSpec((SSD_CONV, tc), lambda b, j: (0, j)),
                  pl.BlockSpec((1, tc), lambda b, j: (0, j))],
        out_specs=pl.BlockSpec((seq, tc), lambda b, j: (b, j)),
        out_shape=jax.ShapeDtypeStruct((n_seq * seq, SSD_XBC), BF16),
        scratch_shapes=[pltpu.VMEM((seq + 2 * CONV_PAD, tc), F32)],
        compiler_params=_cparams("parallel", "parallel"),
        name="conv_silu",
    )(p_zx, conv_w, conv_b.reshape(1, SSD_XBC))


def _ssd_dt_kernel(p_ref, bias_ref, a_ref, dt_ref, acs_ref, acst_ref):
    n = p_ref.shape[0]
    pre = p_ref[...] + bias_ref[...]
    dt = jnp.maximum(pre, 0.0) + jnp.log1p(jnp.exp(-jnp.abs(pre)))
    a = dt * a_ref[...]
    li = lax.broadcasted_iota(jnp.int32, (n, n), 0)
    si = lax.broadcasted_iota(jnp.int32, (n, n), 1)
    lower = (si <= li).astype(F32)
    upper = (si >= li).astype(F32)
    fwd = jnp.dot(lower, a, precision=HIGHEST, preferred_element_type=F32)
    bwd = jnp.dot(upper, a, precision=HIGHEST, preferred_element_type=F32)
    lane = lax.broadcasted_iota(jnp.int32, fwd.shape, 1)
    acs = jnp.where(lane < SSD_HEADS, fwd, bwd)
    dt_ref[...] = dt
    acs_ref[...] = acs
    acst_ref[...] = acs.T


def ssd_dt(p_dt, dt_bias, a_log):
    rows = p_dt.shape[0]
    pad = LANES - 2 * SSD_HEADS
    bias = jnp.pad(dt_bias.reshape(1, 2 * SSD_HEADS), ((0, 0), (0, pad)))
    a_neg = jnp.pad(-jnp.exp(a_log.astype(F32)).reshape(1, 2 * SSD_HEADS), ((0, 0), (0, pad)))
    c = SSD_CHUNK
    return pl.pallas_call(
        _ssd_dt_kernel,
        grid=(rows // c,),
        in_specs=[pl.BlockSpec((c, LANES), lambda i: (i, 0)),
                  pl.BlockSpec((1, LANES), lambda i: (0, 0)),
                  pl.BlockSpec((1, LANES), lambda i: (0, 0))],
        out_specs=[pl.BlockSpec((c, LANES), lambda i: (i, 0)),
                   pl.BlockSpec((c, LANES), lambda i: (i, 0)),
                   pl.BlockSpec((LANES, c), lambda i: (0, i))],
        out_shape=[jax.ShapeDtypeStruct((rows, LANES), F32),
                   jax.ShapeDtypeStruct((rows, LANES), F32),
                   jax.ShapeDtypeStruct((LANES, rows), F32)],
        compiler_params=_cparams("parallel"),
        name="ssd_dt",
    )(p_dt, bias, a_neg)


def _ssd_scan_kernel(x_ref, b_ref, c_ref, dt_ref, acs_ref, acsr_ref, y_ref, state_ref, *, reverse):
    n = x_ref.shape[0]
    heads = dt_ref.shape[1]
    pair_w = 2 * SSD_HEAD_DIM

    @pl.when(pl.program_id(2) == 0)
    def _():
        state_ref[...] = jnp.zeros_like(state_ref)

    x = x_ref[...].astype(F32)
    bm = b_ref[...]
    cm = c_ref[...]
    dt = dt_ref[...]
    acs = acs_ref[...]
    acs_row = acsr_ref[...]
    end = 0 if reverse else n - 1
    total = acs[end:end + 1, :]
    e_in = jnp.exp(acs)
    e_out = jnp.exp(total - acs)
    e_tot = jnp.exp(total)

    li = lax.broadcasted_iota(jnp.int32, (n, n), 0)
    si = lax.broadcasted_iota(jnp.int32, (n, n), 1)
    visible = (si >= li) if reverse else (si <= li)
    lo = lax.broadcasted_iota(jnp.int32, (1, pair_w), 1) < SSD_HEAD_DIM

    cb = lax.dot_general(cm, bm, (((1,), (1,)), ((), ())), preferred_element_type=F32)
    state = state_ref[...]
    y_off = jnp.dot(cm, state.astype(BF16), preferred_element_type=F32)

    xd_parts, decay_parts = [], []
    for i in range(heads // 2):
        j0, j1 = 2 * i, 2 * i + 1
        cols = slice(i * pair_w, (i + 1) * pair_w)
        xp = x[:, cols] * jnp.where(lo, dt[:, j0:j0 + 1], dt[:, j1:j1 + 1])
        y_diag = None
        for j, keep in ((j0, lo), (j1, jnp.logical_not(lo))):
            diff = acs[:, j:j + 1] - acs_row[j:j + 1, :]
            m = (cb * jnp.exp(jnp.where(visible, diff, MASK_VALUE))).astype(BF16)
            part = jnp.dot(m, jnp.where(keep, xp, 0.0).astype(BF16), preferred_element_type=F32)
            y_diag = part if y_diag is None else y_diag + part
        y_ref[:, cols] = y_diag + y_off[:, cols] * jnp.where(lo, e_in[:, j0:j0 + 1], e_in[:, j1:j1 + 1])
        xd_parts.append((xp * jnp.where(lo, e_out[:, j0:j0 + 1], e_out[:, j1:j1 + 1])).astype(BF16))
        decay_parts.append(jnp.where(lo, e_tot[:, j0:j0 + 1], e_tot[:, j1:j1 + 1]))
    xd = jnp.concatenate(xd_parts, axis=1)
    decay = jnp.concatenate(decay_parts, axis=1)
    bt = bm.astype(F32).T.astype(BF16)
    state_ref[...] = state * decay + jnp.dot(bt, xd, preferred_element_type=F32)


def ssd_scan(xbc, dt_col, acs_col, acs_row, n_batch, seq, ctx_len, reverse):
    c = SSD_CHUNK
    lat_chunks, ctx_chunks = seq // c, ctx_len // c
    ctx_base = n_batch * lat_chunks
    n_steps = lat_chunks + ctx_chunks
    gw = SSD_GROUP_W
    heads = SSD_HEADS // SSD_GROUPS

    def chunk(b, k):
        kc = (ctx_chunks - 1 - k) if reverse else k
        kl = (n_steps - 1 - k) if reverse else (k - ctx_chunks)
        return jnp.where(k < ctx_chunks, ctx_base + b * ctx_chunks + kc, b * lat_chunks + kl)

    b_col0 = SSD_INNER // SSD_STATE
    c_col0 = (SSD_INNER + SSD_GN) // SSD_STATE
    return pl.pallas_call(
        functools.partial(_ssd_scan_kernel, reverse=reverse),
        grid=(n_batch, SSD_GROUPS, n_steps),
        in_specs=[pl.BlockSpec((c, gw), lambda b, g, k: (chunk(b, k), g)),
                  pl.BlockSpec((c, SSD_STATE), lambda b, g, k: (chunk(b, k), b_col0 + g)),
                  pl.BlockSpec((c, SSD_STATE), lambda b, g, k: (chunk(b, k), c_col0 + g)),
                  pl.BlockSpec((None, c, heads), lambda b, g, k: (g, chunk(b, k), 0)),
                  pl.BlockSpec((None, c, heads), lambda b, g, k: (g, chunk(b, k), 0)),
                  pl.BlockSpec((None, heads, c), lambda b, g, k: (g, 0, chunk(b, k)))],
        out_specs=pl.BlockSpec((c, gw), lambda b, g, k: (chunk(b, k), g)),
        out_shape=jax.ShapeDtypeStruct((xbc.shape[0], SSD_INNER), F32),
        scratch_shapes=[pltpu.VMEM((SSD_STATE, gw), F32)],
        compiler_params=_cparams("parallel", "parallel", "arbitrary"),
        name="ssd_scan_bwd" if reverse else "ssd_scan_fwd",
    )(xbc, xbc, xbc, dt_col, acs_col, acs_row)


def _ssd_gate_kernel(yf_ref, yb_ref, x_ref, z_ref, d_ref, w_ref, o_ref):
    z = z_ref[...].astype(F32)
    y = yf_ref[...] + yb_ref[...] + d_ref[...] * x_ref[...].astype(F32)
    g = y * (z * jax.nn.sigmoid(z))
    o = g * lax.rsqrt(jnp.mean(g * g, axis=-1, keepdims=True) + EPS) * w_ref[...]
    o_ref[...] = o.astype(o_ref.dtype)


def ssd_gate(y_f, y_b, xbc, p_zx, d_skip, norm_w):
    rows = y_f.shape[0]
    t, gw = ROW_BLOCK, SSD_GROUP_W
    d_cols = jnp.repeat(d_skip.astype(F32), SSD_HEAD_DIM).reshape(1, SSD_INNER)
    spec = pl.BlockSpec((t, gw), lambda i, g: (i, g))
    vec = pl.BlockSpec((1, gw), lambda i, g: (0, g))
    return pl.pallas_call(
        _ssd_gate_kernel,
        grid=(rows // t, SSD_GROUPS),
        in_specs=[spec, spec, spec, spec, vec, vec],
        out_specs=spec,
        out_shape=jax.ShapeDtypeStruct((rows, SSD_INNER), BF16),
        compiler_params=_cparams("parallel", "parallel"),
        name="ssd_gate",
    )(y_f, y_b, xbc, p_zx, d_cols, norm_w.reshape(1, SSD_INNER))


def _head_norm_kernel(x_ref, w_ref, *rest, scale, rope):
    if rope:
        cos_ref, sin_ref, o_ref = rest
        even = lax.broadcasted_iota(jnp.int32, (1, HEAD_DIM), 1) % 2 == 0
    else:
        (o_ref,) = rest
    for hd in range(x_ref.shape[1] // HEAD_DIM):
        cols = slice(hd * HEAD_DIM, (hd + 1) * HEAD_DIM)
        x = x_ref[:, cols].astype(F32)
        y = x * lax.rsqrt(jnp.mean(x * x, axis=-1, keepdims=True) + EPS) * w_ref[:, cols]
        if rope:
            partner = jnp.where(even, pltpu.roll(y, HEAD_DIM - 1, 1), pltpu.roll(y, 1, 1))
            y = y * cos_ref[...] + partner * sin_ref[...]
        if scale != 1.0:
            y = y * scale
        o_ref[:, cols] = y.astype(o_ref.dtype)


def head_norm(p, col_off, width, w_cols, n_rows, scale=1.0, rope=None, seq=None, n_lat=None):
    t, tc = ROW_BLOCK, min(width, 1024)
    c_off = col_off // tc
    in_specs = [pl.BlockSpec((t, tc), lambda i, j: (i, j + c_off)),
                pl.BlockSpec((1, tc), lambda i, j: (0, j))]
    args = [p, w_cols]
    if rope is not None:
        per_seq, lat_blocks = seq // t, n_lat // t
        tab = pl.BlockSpec((t, HEAD_DIM), lambda i, j: (jnp.where(i < lat_blocks, i % per_seq, per_seq), 0))
        in_specs += [tab, tab]
        args += list(rope)
    return pl.pallas_call(
        functools.partial(_head_norm_kernel, scale=scale, rope=rope is not None),
        grid=(n_rows // t, width // tc),
        in_specs=in_specs,
        out_specs=pl.BlockSpec((t, tc), lambda i, j: (i, j)),
        out_shape=jax.ShapeDtypeStruct((n_rows, width), BF16),
        compiler_params=_cparams("parallel", "parallel"),
        name="head_norm",
    )(*args)


def _rope_tables(seq):
    t = jnp.arange(seq, dtype=jnp.int32)
    row = (t // GRID_W).astype(F32)
    col = (t % GRID_W).astype(F32)
    axis_dim = HEAD_DIM // 2
    inv = 1.0 / (ROPE_THETA ** (jnp.arange(0, axis_dim, 2, dtype=F32) / axis_dim))
    ang = jnp.concatenate([row[:, None] * inv[None], col[:, None] * inv[None]], axis=-1)
    cos = jnp.repeat(jnp.cos(ang), 2, axis=-1)
    sin = jnp.stack([-jnp.sin(ang), jnp.sin(ang)], axis=-1).reshape(seq, HEAD_DIM)
    identity = jnp.ones((ROW_BLOCK, HEAD_DIM), F32)
    return jnp.concatenate([cos, identity], axis=0), jnp.concatenate([sin, 0.0 * identity], axis=0)


def _attend(q, pieces):
    scores = []
    for k, _, bias in pieces:
        s = lax.dot_general(q, k, (((1,), (1,)), ((), ())), preferred_element_type=F32)
        scores.append(s if bias is None else s + bias)
    m = functools.reduce(jnp.maximum, [jnp.max(s, axis=-1, keepdims=True) for s in scores])
    denom, out = None, None
    for s, (_, v, _) in zip(scores, pieces):
        p = jnp.exp(s - m)
        ps = jnp.sum(p, axis=-1, keepdims=True)
        pv = jnp.dot(p.astype(v.dtype), v, preferred_element_type=F32)
        denom = ps if denom is None else denom + ps
        out = pv if out is None else out + pv
    return out / denom


def _na_kernel(q_ref, k0_ref, k1_ref, k2_ref, kc_ref, v0_ref, v1_ref, v2_ref, vc_ref, bias_ref, o_ref):
    t = q_ref.shape[0]
    pieces = [(k_ref[...], v_ref[...], bias_ref[:, i * t:(i + 1) * t])
              for i, (k_ref, v_ref) in enumerate(((k0_ref, v0_ref), (k1_ref, v1_ref), (k2_ref, v2_ref)))]
    pieces.append((kc_ref[...], vc_ref[...], None))
    o_ref[...] = _attend(q_ref[...], pieces).astype(o_ref.dtype)


def _na_bias_table(rpb, rows):
    n_blocks = rows // NA_QROWS
    w0_of = lambda j: min(max(NA_QROWS * j - NA_WIN_ROWS // 2, 0), rows - NA_KROWS)
    geoms, pattern_of = [], []
    for j in range(n_blocks):
        r0s = tuple(min(max(NA_QROWS * j + a - NA_WIN_ROWS // 2, 0), rows - NA_WIN_ROWS) - w0_of(j)
                    for a in range(NA_QROWS))
        key = (NA_QROWS * j - w0_of(j), r0s)
        if key not in geoms:
            geoms.append(key)
        pattern_of.append(geoms.index(key))
    qc = jnp.arange(GRID_W)[:, None]
    kc = jnp.arange(GRID_W)[None, :]
    cs = jnp.clip(qc - NA_WIN_COLS // 2, 0, GRID_W - NA_WIN_COLS)
    col_ok = (kc >= cs) & (kc < cs + NA_WIN_COLS)
    dc = jnp.clip(kc - qc + NA_WIN_COLS - 1, 0, 2 * NA_WIN_COLS - 2)
    pick = (dc[None] == jnp.arange(2 * NA_WIN_COLS - 1)[:, None, None]).astype(F32)
    tiles = jnp.einsum('hrd,dqk->hrqk', rpb.astype(F32), pick, precision=HIGHEST)
    tiles = jnp.where(col_ok, tiles, MASK_VALUE)
    masked = jnp.full((rpb.shape[0], GRID_W, GRID_W), MASK_VALUE, F32)
    tables = []
    for q_row0, r0s in geoms:
        tile_rows = []
        for qi in range(NA_QROWS):
            row = [tiles[:, ki - (q_row0 + qi) + NA_WIN_ROWS - 1] if r0s[qi] <= ki < r0s[qi] + NA_WIN_ROWS
                   else masked for ki in range(NA_KROWS)]
            tile_rows.append(jnp.concatenate(row, axis=2))
        tables.append(jnp.concatenate(tile_rows, axis=1))
    tables.append(jnp.full_like(tables[0], MASK_VALUE))
    pattern_of.append(len(geoms))
    w0_blocks = [w0_of(j) // NA_QROWS for j in range(n_blocks)] + [0]
    return jnp.stack(tables, axis=1), pattern_of, w0_blocks


def _lookup(values, j):
    out = jnp.int32(values[-1])
    for idx in range(len(values) - 2, -1, -1):
        out = jnp.where(j == idx, jnp.int32(values[idx]), out)
    return out


def na_attention(qn, kn, p_na, rpb, n_batch, seq, ctx_len):
    t = NA_QROWS * GRID_W
    assert t == ROW_BLOCK == ctx_len
    rows = seq // GRID_W
    n_blocks = seq // t
    bias, pattern_of, w0_blocks = _na_bias_table(rpb, rows)
    v_col0 = 2 * NA_WIDTH // HEAD_DIM
    ctx_base = n_batch * n_blocks

    def q_blk(j, b):
        return jnp.where(j < n_blocks, b * n_blocks + j, ctx_base + b)

    def kv_blk(piece):
        return lambda h, j, b: (b * n_blocks + _lookup(w0_blocks, j) + piece, h)

    def kv_blk_v(piece):
        return lambda h, j, b: (b * n_blocks + _lookup(w0_blocks, j) + piece, v_col0 + h)

    blk = lambda fn: pl.BlockSpec((t, HEAD_DIM), fn)
    in_specs = ([blk(lambda h, j, b: (q_blk(j, b), h))]
                + [blk(kv_blk(i)) for i in range(3)] + [blk(lambda h, j, b: (ctx_base + b, h))]
                + [blk(kv_blk_v(i)) for i in range(3)] + [blk(lambda h, j, b: (ctx_base + b, v_col0 + h))]
                + [pl.BlockSpec((None, None, t, 3 * t), lambda h, j, b: (h, _lookup(pattern_of, j), 0, 0))])
    return pl.pallas_call(
        _na_kernel,
        grid=(NA_HEADS, n_blocks + 1, n_batch),
        in_specs=in_specs,
        out_specs=blk(lambda h, j, b: (q_blk(j, b), h)),
        out_shape=jax.ShapeDtypeStruct((qn.shape[0], NA_WIDTH), BF16),
        compiler_params=_cparams("parallel", "parallel", "parallel"),
        name="na_attention",
    )(qn, kn, kn, kn, kn, p_na, p_na, p_na, p_na, bias)


def _gqa_kernel(q_ref, kl_ref, kc_ref, vl_ref, vc_ref, o_ref):
    t = q_ref.shape[0]
    q = jnp.concatenate([q_ref[:, g * HEAD_DIM:(g + 1) * HEAD_DIM] for g in range(GQA_GROUP)], axis=0)
    o = _attend(q, [(kc_ref[...], vc_ref[...], None), (kl_ref[...], vl_ref[...], None)])
    for g in range(GQA_GROUP):
        o_ref[:, g * HEAD_DIM:(g + 1) * HEAD_DIM] = o[g * t:(g + 1) * t].astype(o_ref.dtype)


def gqa_attention(qn, kn, p_kv, v_col_off, n_batch, seq, ctx_len):
    t = ROW_BLOCK
    q_tiles = seq // t
    ctx_base = n_batch * seq // ctx_len
    v0 = v_col_off // HEAD_DIM
    gw = GQA_GROUP * HEAD_DIM
    return pl.pallas_call(
        _gqa_kernel,
        grid=(n_batch, GQA_KV_HEADS, q_tiles),
        in_specs=[pl.BlockSpec((t, gw), lambda b, h, i: (b * q_tiles + i, h)),
                  pl.BlockSpec((seq, HEAD_DIM), lambda b, h, i: (b, h)),
                  pl.BlockSpec((ctx_len, HEAD_DIM), lambda b, h, i: (ctx_base + b, h)),
                  pl.BlockSpec((seq, HEAD_DIM), lambda b, h, i: (b, v0 + h)),
                  pl.BlockSpec((ctx_len, HEAD_DIM), lambda b, h, i: (ctx_base + b, v0 + h))],
        out_specs=pl.BlockSpec((t, gw), lambda b, h, i: (b * q_tiles + i, h)),
        out_shape=jax.ShapeDtypeStruct((n_batch * seq, GQA_Q), BF16),
        compiler_params=_cparams("parallel", "parallel", "parallel"),
        name="gqa_attention",
    )(qn, kn, kn, p_kv, p_kv)


def _diff_kernel(lam_ref, q_ref, kl_ref, kc_ref, vl_ref, vc_ref, w_ref, o_ref, *, out_scale):
    outs = []
    for comp in range(2):
        cols = slice(comp * HEAD_DIM, (comp + 1) * HEAD_DIM)
        outs.append(_attend(q_ref[:, cols], [(kc_ref[:, cols], vc_ref[...], None),
                                              (kl_ref[:, cols], vl_ref[...], None)]))
    o = outs[0] - lam_ref[0] * outs[1]
    o = o * lax.rsqrt(jnp.mean(o * o, axis=-1, keepdims=True) + EPS) * w_ref[...]
    o_ref[...] = (o * out_scale).astype(o_ref.dtype)


def diff_attention(qn, q_col_off, kn, p_kv, v_col_off, lam, subln_w, out_scale, n_batch, seq, ctx_len):
    t = ROW_BLOCK
    q_tiles = seq // t
    ctx_base = n_batch * seq // ctx_len
    v0 = v_col_off // DIFF_V_DIM
    w2 = 2 * HEAD_DIM
    q0 = q_col_off // w2
    return pl.pallas_call(
        functools.partial(_diff_kernel, out_scale=out_scale),
        grid=(n_batch, DIFF_HEADS, q_tiles),
        in_specs=[pl.BlockSpec(memory_space=pltpu.SMEM),
                  pl.BlockSpec((t, w2), lambda b, h, i: (b * q_tiles + i, q0 + h)),
                  pl.BlockSpec((seq, w2), lambda b, h, i: (b, h)),
                  pl.BlockSpec((ctx_len, w2), lambda b, h, i: (ctx_base + b, h)),
                  pl.BlockSpec((seq, DIFF_V_DIM), lambda b, h, i: (b, v0 + h)),
                  pl.BlockSpec((ctx_len, DIFF_V_DIM), lambda b, h, i: (ctx_base + b, v0 + h)),
                  pl.BlockSpec((1, DIFF_V_DIM), lambda b, h, i: (0, 0))],
        out_specs=pl.BlockSpec((t, DIFF_V_DIM), lambda b, h, i: (b * q_tiles + i, h)),
        out_shape=jax.ShapeDtypeStruct((n_batch * seq, DIFF_V), BF16),
        compiler_params=_cparams("parallel", "parallel", "parallel"),
        name="diff_attention",
    )(lam.reshape(1), qn, kn, kn, p_kv, p_kv, subln_w.reshape(1, DIFF_V_DIM))


def _moe_plan_kernel(route_ref, pos_ref, pad_end_ref, counts_ref, start_ref):
    phase, i = pl.program_id(0), pl.program_id(1)
    t = route_ref.shape[0]
    route = route_ref[...]
    lane = lax.broadcasted_iota(jnp.int32, route.shape, 1)
    chosen = [lane == route[:, ROUTE_EXPERT_LANE + k:ROUTE_EXPERT_LANE + k + 1].astype(jnp.int32)
              for k in range(MOE_TOPK)]
    both = functools.reduce(jnp.add, [c.astype(F32) for c in chosen])
    tile_counts = jnp.sum(both, axis=0, keepdims=True)

    @pl.when((phase == 0) & (i == 0))
    def _():
        counts_ref[...] = jnp.zeros_like(counts_ref)

    @pl.when(phase == 0)
    def _():
        counts_ref[...] += tile_counts

    @pl.when((phase == 1) & (i == 0))
    def _():
        padded = jnp.ceil(counts_ref[...] * (1.0 / MOE_BLOCK)) * MOE_BLOCK
        rows = pad_end_ref.shape[0]
        ki = lax.broadcasted_iota(jnp.int32, (LANES, LANES), 0)
        ni = lax.broadcasted_iota(jnp.int32, (LANES, LANES), 1)
        pad_end = jnp.dot(jnp.broadcast_to(padded, (rows, LANES)), (ki <= ni).astype(F32),
                          precision=HIGHEST, preferred_element_type=F32)
        pad_end_ref[...] = pad_end
        start_ref[...] = pad_end[0:1] - padded

    @pl.when(phase == 1)
    def _():
        li = lax.broadcasted_iota(jnp.int32, (t, t), 0)
        si = lax.broadcasted_iota(jnp.int32, (t, t), 1)
        earlier = jnp.dot((si < li).astype(BF16), both.astype(BF16), preferred_element_type=F32)
        row_of = earlier + start_ref[...]
        out = jnp.zeros(route.shape, F32)
        for k in range(MOE_TOPK):
            p = jnp.sum(jnp.where(chosen[k], row_of, 0.0), axis=-1, keepdims=True)
            out = jnp.where(lane == k, p, out)
        pos_ref[...] = out.astype(jnp.int32)
        start_ref[...] += tile_counts


def moe_plan(route):
    n = route.shape[0]
    t = ROW_BLOCK
    return pl.pallas_call(
        _moe_plan_kernel,
        grid=(2, n // t),
        in_specs=[pl.BlockSpec((t, LANES), lambda p, i: (i, 0))],
        out_specs=[pl.BlockSpec((t, LANES), lambda p, i: (i * p, 0)),
                   pl.BlockSpec((8, LANES), lambda p, i: (0, 0))],
        out_shape=[jax.ShapeDtypeStruct((n, LANES), jnp.int32),
                   jax.ShapeDtypeStruct((8, LANES), F32)],
        scratch_shapes=[pltpu.VMEM((1, LANES), F32), pltpu.VMEM((1, LANES), F32)],
        compiler_params=_cparams("arbitrary", "arbitrary"),
        name="moe_plan",
    )(route)


def _moe_kernel(be_ref, nused_ref, x_ref, w1_ref, w3_ref, w2_ref, o_ref, w1b, w3b, w2b):
    i = pl.program_id(0)
    prev = be_ref[jnp.maximum(i - 1, 0)]

    @pl.when((i == 0) | (be_ref[i] != prev))
    def _():
        w1b[...] = w1_ref[...].astype(BF16)
        w3b[...] = w3_ref[...].astype(BF16)
        w2b[...] = w2_ref[...].astype(BF16)

    @pl.when(i < nused_ref[0])
    def _():
        x = x_ref[...].astype(BF16)
        a = jnp.dot(x, w1b[...], preferred_element_type=F32)
        b = jnp.dot(x, w3b[...], preferred_element_type=F32)
        hid = (a * jax.nn.sigmoid(a)) * b
        o_ref[...] = jnp.dot(hid.astype(BF16), w2b[...], preferred_element_type=F32)

    @pl.when(i >= nused_ref[0])
    def _():
        o_ref[...] = jnp.zeros_like(o_ref)


def moe_experts(x_sorted, block_e, n_used, w1, w3, w2, layer):
    n_blocks = block_e.shape[0]
    d, hid = w1.shape[2], w1.shape[3]
    grid_spec = pltpu.PrefetchScalarGridSpec(
        num_scalar_prefetch=2,
        grid=(n_blocks,),
        in_specs=[pl.BlockSpec((MOE_BLOCK, d), lambda i, be, nu: (i, 0)),
                  pl.BlockSpec((None, None, d, hid), lambda i, be, nu: (layer, be[i], 0, 0)),
                  pl.BlockSpec((None, None, d, hid), lambda i, be, nu: (layer, be[i], 0, 0)),
                  pl.BlockSpec((None, None, hid, d), lambda i, be, nu: (layer, be[i], 0, 0))],
        out_specs=pl.BlockSpec((MOE_BLOCK, d), lambda i, be, nu: (i, 0)),
        scratch_shapes=[pltpu.VMEM((d, hid), BF16), pltpu.VMEM((d, hid), BF16), pltpu.VMEM((hid, d), BF16)],
    )
    return pl.pallas_call(
        _moe_kernel,
        grid_spec=grid_spec,
        out_shape=jax.ShapeDtypeStruct((n_blocks * MOE_BLOCK, d), F32),
        compiler_params=_cparams("arbitrary"),
        name="moe_experts",
    )(block_e, n_used, x_sorted, w1, w3, w2)


def _combine_kernel(h_ref, g_ref, route_ref, *refs):
    y_refs, o_ref = refs[:-1], refs[-1]
    acc = None
    for k, y_ref in enumerate(y_refs):
        part = y_ref[...] * route_ref[:, ROUTE_GATE_LANE + k:ROUTE_GATE_LANE + k + 1]
        acc = part if acc is None else acc + part
    o_ref[...] = h_ref[...] + g_ref[...] * acc


def moe_combine(h, mod3, gate_chunk, route, ys, geom):
    n_lat, seq, n_batch = geom
    n_rows, d = ys[0].shape
    t = ROW_BLOCK
    row = functools.partial(_mod_row, tile=t, n_lat=n_lat, seq=seq, n_batch=n_batch)
    spec = pl.BlockSpec((t, d), lambda i: (i, 0))
    return pl.pallas_call(
        _combine_kernel,
        grid=(n_rows // t,),
        in_specs=[spec, pl.BlockSpec((None, 1, d), lambda i: (row(i), 0, gate_chunk)),
                  pl.BlockSpec((t, LANES), lambda i: (i, 0))] + [spec] * len(ys),
        out_specs=spec,
        out_shape=jax.ShapeDtypeStruct((n_rows, d), F32),
        compiler_params=_cparams("parallel"),
        name="moe_combine",
    )(h, mod3, route, *ys)


def moe_layer(h, f, route, mod3, gate_chunk, w1, w3, w2, layer, geom):
    n = f.shape[0]
    n_assign = n * MOE_TOPK
    n_blocks = -(-n_assign // MOE_BLOCK) + MOE_EXPERTS
    pos, pad_end = moe_plan(route)
    pos = pos[:, :MOE_TOPK]
    pad_end = pad_end[0, :MOE_EXPERTS].astype(jnp.int32)
    n_used = pad_end[-1] // MOE_BLOCK
    blk = jnp.minimum(jnp.arange(n_blocks, dtype=jnp.int32), n_used - 1) * MOE_BLOCK
    block_e = jnp.minimum(jnp.sum(pad_end[None, :] <= blk[:, None], axis=1, dtype=jnp.int32), MOE_EXPERTS - 1)
    tok = jnp.broadcast_to(jnp.arange(n, dtype=jnp.int32)[:, None], (n, MOE_TOPK))
    slot_tok = jnp.zeros((n_blocks * MOE_BLOCK,), jnp.int32).at[pos.reshape(-1)].set(
        tok.reshape(-1), unique_indices=True, mode='promise_in_bounds')
    x_sorted = f.at[slot_tok].get(mode='promise_in_bounds')
    yb = moe_experts(x_sorted, block_e, n_used.reshape(1), w1, w3, w2, layer)
    ys = [yb.at[pos[:, k]].get(mode='promise_in_bounds') for k in range(MOE_TOPK)]
    return moe_combine(h, mod3, gate_chunk, route, ys, geom)


def _even_layer_mix(a, in_w, conv_w, conv_b, dt_bias, a_log, d_skip, ssd_norm_w, q_norm, k_norm, rpb,
                    n_batch, seq, ctx_len):
    n_rows = a.shape[0]
    n_lat = n_batch * seq
    zx_end = SSD_INNER + SSD_XBC
    dt_end = zx_end + 2 * SSD_HEADS
    w_zx = in_w[:, :zx_end].astype(BF16)
    w_dt = jnp.pad(in_w[:, zx_end:dt_end], ((0, 0), (0, LANES - 2 * SSD_HEADS))).astype(BF16)
    w_na = in_w[:, dt_end:].astype(BF16)
    p_zx = matmul([a], [w_zx], n_rows, BF16, tn=512)
    p_dt = matmul([a], [w_dt], n_rows, F32, tn=LANES)
    p_na = matmul([a], [w_na], n_rows, BF16, tn=512)

    xbc = jnp.concatenate([conv_silu(p_zx, conv_w, conv_b, n_batch, seq, 0),
                           conv_silu(p_zx, conv_w, conv_b, n_batch, ctx_len, n_lat)], axis=0)
    dt, acs, acs_t = ssd_dt(p_dt, dt_bias, a_log)
    hg = SSD_HEADS // SSD_GROUPS
    col = lambda v: v[:, :2 * SSD_HEADS].reshape(n_rows, 2, SSD_GROUPS, hg).transpose(1, 2, 0, 3)
    dt_col, acs_col = col(dt), col(acs)
    acs_row = acs_t[:2 * SSD_HEADS].reshape(2, SSD_GROUPS, hg, n_rows)
    y_f = ssd_scan(xbc, dt_col[0], acs_col[0], acs_row[0], n_batch, seq, ctx_len, reverse=False)
    y_b = ssd_scan(xbc, dt_col[1], acs_col[1], acs_row[1], n_batch, seq, ctx_len, reverse=True)
    s_out = ssd_gate(y_f, y_b, xbc, p_zx, d_skip, ssd_norm_w)

    tile_w = lambda w: jnp.tile(w.astype(F32), NA_HEADS).reshape(1, NA_WIDTH)
    qn = head_norm(p_na, 0, NA_WIDTH, tile_w(q_norm), n_rows, scale=HEAD_DIM ** -0.5)
    kn = head_norm(p_na, NA_WIDTH, NA_WIDTH, tile_w(k_norm), n_rows)
    n_out = na_attention(qn, kn, p_na, rpb, n_batch, seq, ctx_len)
    return s_out, n_out


def _odd_layer_mix(a, in_w, gq_norm, gk_norm, dq_norm, dk_norm, lam_vecs, subln_w, lambda_init,
                   n_batch, seq, ctx_len):
    n_rows = a.shape[0]
    n_lat = n_batch * seq
    p_q = matmul([a], [in_w[:, :ODD_Q].astype(BF16)], n_lat, BF16, tn=512)
    p_kv = matmul([a], [in_w[:, ODD_Q:].astype(BF16)], n_rows, BF16, tn=512)
    rope = _rope_tables(seq)
    scale = HEAD_DIM ** -0.5
    rep = lambda w, n: jnp.tile(w.astype(F32), n).reshape(1, n * HEAD_DIM)
    q_w = jnp.concatenate([rep(gq_norm, GQA_HEADS), rep(dq_norm, 2 * DIFF_HEADS)], axis=1)
    qn = head_norm(p_q, 0, ODD_Q, q_w, n_lat, scale=scale, rope=rope, seq=seq, n_lat=n_lat)
    dk_off = 2 * GQA_KV
    dv_off = dk_off + DIFF_QK
    gkn = head_norm(p_kv, 0, GQA_KV, rep(gk_norm, GQA_KV_HEADS), n_rows, rope=rope, seq=seq, n_lat=n_lat)
    dkn = head_norm(p_kv, dk_off, DIFF_QK, rep(dk_norm, 2 * DIFF_HEADS), n_rows, rope=rope, seq=seq, n_lat=n_lat)
    lv = lam_vecs.astype(F32)
    lam = jnp.exp(jnp.dot(lv[0], lv[1])) - jnp.exp(jnp.dot(lv[2], lv[3])) + lambda_init
    og = gqa_attention(qn, gkn, p_kv, GQA_KV, n_batch, seq, ctx_len)
    od = diff_attention(qn, GQA_Q, dkn, p_kv, dv_off, lam, subln_w, 1.0 - lambda_init,
                        n_batch, seq, ctx_len)
    return og, od


def kernel(x, c, ctx, c_ctx, ada_w, ada_b, norm_w, ev_in_w, ev_conv_w, ev_conv_b, ev_dt_bias, ev_a_log, ev_d_skip, ev_ssd_norm_w, ev_na_q_norm, ev_na_k_norm, ev_na_rpb, ev_out_w, od_in_w, od_gqa_q_norm, od_gqa_k_norm, od_diff_q_norm, od_diff_k_norm, od_lambda, od_diff_subln, od_out_w, moe_group_w, moe_expert_w, moe_w1, moe_w3, moe_w2):
    n_batch, seq, d = x.shape
    ctx_len = ctx.shape[1]
    depth = ada_w.shape[0]
    assert depth == 2 and d == D_MODEL
    n_lat, n_ctx = n_batch * seq, n_batch * ctx_len
    n_all = n_lat + n_ctx
    geom = (n_lat, seq, n_batch)

    cc = jnp.concatenate([c, c_ctx[None, :], jnp.zeros((8 - n_batch - 1, d), F32)], axis=0)
    mod = ada_modulation(cc, ada_w, ada_b)
    router_w = lambda l: jnp.pad(jnp.concatenate([moe_group_w[l], moe_expert_w[l]], axis=1),
                                 ((0, 0), (0, LANES - MOE_GROUPS - MOE_EXPERTS)))

    h = jnp.concatenate([x.reshape(n_lat, d), ctx.reshape(n_ctx, d)], axis=0)

    mod3 = mod[0].reshape(8, 1, 6 * d)
    a = norm_modulate(h, norm_w[0, 0], mod3, 1, 0, n_all, geom)
    s_out, n_out = _even_layer_mix(a, ev_in_w[0], ev_conv_w[0], ev_conv_b[0], ev_dt_bias[0], ev_a_log[0],
                                   ev_d_skip[0], ev_ssd_norm_w[0], ev_na_q_norm[0], ev_na_k_norm[0],
                                   ev_na_rpb[0], n_batch, seq, ctx_len)
    out_w = ev_out_w[0].astype(BF16)
    h = matmul([s_out, n_out], [out_w[:SSD_INNER], out_w[SSD_INNER:]], n_all, F32, tn=512,
               residual=(h, mod3, 2, geom))
    f, route = norm_modulate(h, norm_w[0, 1], mod3, 4, 3, n_all, geom, router_w=router_w(0))
    h = moe_layer(h, f, route, mod3, 5, moe_w1, moe_w3, moe_w2, 0, geom)

    mod3 = mod[1].reshape(8, 1, 6 * d)
    a = norm_modulate(h, norm_w[1, 0], mod3, 1, 0, n_all, geom)
    lambda_init = 0.8 - 0.6 * math.exp(-0.3 * 1)
    og, od = _odd_layer_mix(a, od_in_w[0], od_gqa_q_norm[0], od_gqa_k_norm[0], od_diff_q_norm[0],
                            od_diff_k_norm[0], od_lambda[0], od_diff_subln[0], lambda_init,
                            n_batch, seq, ctx_len)
    out_w = od_out_w[0].astype(BF16)
    h = matmul([og, od], [out_w[:GQA_Q], out_w[GQA_Q:]], n_lat, F32, tn=512,
               residual=(h, mod3, 2, geom))
    f, route = norm_modulate(h, norm_w[1, 1], mod3, 4, 3, n_lat, geom, router_w=router_w(1))
    h = moe_layer(h, f, route, mod3, 5, moe_w1, moe_w3, moe_w2, 1, geom)
    return h.reshape(n_batch, seq, d)
```

```python
import functools
import math

import jax
import jax.numpy as jnp
from jax import lax
from jax.experimental import pallas as pl
from jax.experimental.pallas import tpu as pltpu

F32 = jnp.float32
BF16 = jnp.bfloat16
HIGHEST = lax.Precision.HIGHEST

D_MODEL = 2048
GRID_W = 64
EPS = 1e-6
ROPE_THETA = 10000.0
HEAD_DIM = 128

SSD_HEADS = 32
SSD_HEAD_DIM = 64
SSD_INNER = SSD_HEADS * SSD_HEAD_DIM
SSD_GROUPS = 4
SSD_STATE = 128
SSD_GN = SSD_GROUPS * SSD_STATE
SSD_XBC = SSD_INNER + 2 * SSD_GN
SSD_CONV = 5
SSD_CHUNK = 128
SSD_GROUP_W = SSD_INNER // SSD_GROUPS

NA_HEADS = 16
NA_WIDTH = NA_HEADS * HEAD_DIM
NA_WIN_ROWS = 8
NA_WIN_COLS = 16
NA_QROWS = 4
NA_KROWS = 12

GQA_HEADS = 16
GQA_KV_HEADS = 4
GQA_GROUP = GQA_HEADS // GQA_KV_HEADS
GQA_Q = GQA_HEADS * HEAD_DIM
GQA_KV = GQA_KV_HEADS * HEAD_DIM
DIFF_HEADS = 8
DIFF_QK = DIFF_HEADS * 2 * HEAD_DIM
DIFF_V_DIM = 2 * HEAD_DIM
DIFF_V = DIFF_HEADS * DIFF_V_DIM
ODD_Q = GQA_Q + DIFF_QK

MOE_GROUPS = 8
MOE_EXPERTS_PER_GROUP = 8
MOE_EXPERTS = MOE_GROUPS * MOE_EXPERTS_PER_GROUP
MOE_HIDDEN = 512
MOE_TOPK = 2
MOE_BLOCK = 128

ROW_BLOCK = 256
LANES = 128
MASK_VALUE = -1e30
VMEM_LIMIT = 52 * 1024 * 1024


def _cparams(*sem):
    return pltpu.CompilerParams(dimension_semantics=sem, vmem_limit_bytes=VMEM_LIMIT)


def _mod_row(i, tile, n_lat, seq, n_batch):
    return jnp.where(i < n_lat // tile, i // (seq // tile), n_batch)


def _ada_kernel(c_ref, w_ref, b_ref, o_ref):
    c = c_ref[...]
    s = c * jax.nn.sigmoid(c)
    o_ref[...] = jnp.dot(s, w_ref[...], precision=HIGHEST, preferred_element_type=F32) + b_ref[...]


def ada_modulation(cc, ada_w, ada_b):
    n_layers, d, n_out = ada_w.shape
    rows = cc.shape[0]
    tn = 1024
    return pl.pallas_call(
        _ada_kernel,
        grid=(n_layers, n_out // tn),
        in_specs=[pl.BlockSpec((rows, d), lambda l, j: (0, 0)),
                  pl.BlockSpec((None, d, tn), lambda l, j: (l, 0, j)),
                  pl.BlockSpec((None, 1, tn), lambda l, j: (l, 0, j))],
        out_specs=pl.BlockSpec((None, rows, tn), lambda l, j: (l, 0, j)),
        out_shape=jax.ShapeDtypeStruct((n_layers, rows, n_out), F32),
        compiler_params=_cparams("parallel", "parallel"),
        name="ada_modulation",
    )(cc, ada_w, ada_b.reshape(n_layers, 1, n_out))


def _normmod_kernel(h_ref, w_ref, sc_ref, sh_ref, *rest, with_router):
    x = h_ref[...]
    y = x * lax.rsqrt(jnp.mean(x * x, axis=-1, keepdims=True) + EPS) * w_ref[...]
    a = y * (1.0 + sc_ref[...]) + sh_ref[...]
    if with_router:
        rw_ref, o_ref, route_ref = rest
        logits = jnp.dot(a, rw_ref[...], precision=HIGHEST, preferred_element_type=F32)
        route_ref[...] = _route(logits)
    else:
        (o_ref,) = rest
    o_ref[...] = a.astype(o_ref.dtype)


ROUTE_EXPERT_LANE = 0
ROUTE_GATE_LANE = MOE_TOPK


def _route(logits):
    assert MOE_TOPK == 2
    neg = jnp.float32(-3.0e38)
    lane = lax.broadcasted_iota(jnp.int32, logits.shape, 1)
    is_group = lane < MOE_GROUPS
    gl = jnp.where(is_group, logits, neg)
    g_max = jnp.max(gl, axis=-1, keepdims=True)
    g_idx = jnp.min(jnp.where(gl == g_max, lane, LANES), axis=-1, keepdims=True)
    g_gate = 1.0 / jnp.sum(jnp.where(is_group, jnp.exp(logits - g_max), 0.0), axis=-1, keepdims=True)
    first = MOE_GROUPS + g_idx * MOE_EXPERTS_PER_GROUP
    in_group = (lane >= first) & (lane < first + MOE_EXPERTS_PER_GROUP)
    el = jnp.where(in_group, logits, neg)
    v1 = jnp.max(el, axis=-1, keepdims=True)
    i1 = jnp.min(jnp.where(in_group & (el == v1), lane, LANES), axis=-1, keepdims=True)
    rest = in_group & (lane != i1)
    el2 = jnp.where(rest, logits, neg)
    v2 = jnp.max(el2, axis=-1, keepdims=True)
    i2 = jnp.min(jnp.where(rest & (el2 == v2), lane, LANES), axis=-1, keepdims=True)
    e21 = jnp.exp(v2 - v1)
    p1 = 1.0 / (1.0 + e21)
    p2 = e21 / (1.0 + e21)
    vals = [(i1 - MOE_GROUPS).astype(F32), (i2 - MOE_GROUPS).astype(F32), p1 * g_gate, p2 * g_gate]
    out = jnp.zeros(logits.shape, F32)
    for k, v in enumerate(vals):
        out = jnp.where(lane == k, v, out)
    return out


def norm_modulate(h, w, mod3, sc_chunk, sh_chunk, n_rows, geom, router_w=None):
    n_lat, seq, n_batch = geom
    d = h.shape[1]
    t = ROW_BLOCK
    row = functools.partial(_mod_row, tile=t, n_lat=n_lat, seq=seq, n_batch=n_batch)
    in_specs = [pl.BlockSpec((t, d), lambda i: (i, 0)),
                pl.BlockSpec((1, d), lambda i: (0, 0)),
                pl.BlockSpec((None, 1, d), lambda i: (row(i), 0, sc_chunk)),
                pl.BlockSpec((None, 1, d), lambda i: (row(i), 0, sh_chunk))]
    args = [h, w.reshape(1, d), mod3, mod3]
    out_specs = pl.BlockSpec((t, d), lambda i: (i, 0))
    out_shape = jax.ShapeDtypeStruct((n_rows, d), BF16)
    if router_w is not None:
        in_specs.append(pl.BlockSpec(router_w.shape, lambda i: (0, 0)))
        args.append(router_w)
        out_specs = [out_specs, pl.BlockSpec((t, router_w.shape[1]), lambda i: (i, 0))]
        out_shape = [jax.ShapeDtypeStruct((n_rows, d), F32),
                     jax.ShapeDtypeStruct((n_rows, router_w.shape[1]), F32)]
    return pl.pallas_call(
        functools.partial(_normmod_kernel, with_router=router_w is not None),
        grid=(n_rows // t,),
        in_specs=in_specs, out_specs=out_specs, out_shape=out_shape,
        compiler_params=_cparams("parallel"),
        name="norm_modulate",
    )(*args)


def _matmul_kernel(*refs, n_in, epilogue):
    acc = None
    for x_ref, w_ref in zip(refs[:n_in], refs[n_in:2 * n_in]):
        part = jnp.dot(x_ref[...], w_ref[...], preferred_element_type=F32)
        acc = part if acc is None else acc + part
    rest = refs[2 * n_in:]
    if epilogue == "residual":
        h_ref, g_ref, o_ref = rest
        o_ref[...] = h_ref[...] + g_ref[...] * acc
    elif epilogue in ("head_norm", "head_norm_rope"):
        w_ref, o_ref = rest[0], rest[-1]
        if epilogue == "head_norm_rope":
            cos, sin = rest[1][...], rest[2][...]
            even = lax.broadcasted_iota(jnp.int32, (1, HEAD_DIM), 1) % 2 == 0
        for hd in range(acc.shape[1] // HEAD_DIM):
            cols = slice(hd * HEAD_DIM, (hd + 1) * HEAD_DIM)
            y = acc[:, cols]
            y = y * lax.rsqrt(jnp.mean(y * y, axis=-1, keepdims=True) + EPS) * w_ref[:, cols]
            if epilogue == "head_norm_rope":
                partner = jnp.where(even, pltpu.roll(y, HEAD_DIM - 1, 1), pltpu.roll(y, 1, 1))
                y = y * cos + partner * sin
            o_ref[:, cols] = y.astype(o_ref.dtype)
    else:
        (o_ref,) = rest
        o_ref[...] = acc.astype(o_ref.dtype)


def matmul(xs, ws, n_rows, out_dtype, tn, residual=None, head_norm=None):
    n_out = ws[0].shape[1]
    tm = next(t for t in (1024, 512, ROW_BLOCK) if n_rows % t == 0)
    in_specs = ([pl.BlockSpec((tm, x.shape[1]), lambda i, j: (i, 0)) for x in xs]
                + [pl.BlockSpec((w.shape[0], tn), lambda i, j: (0, j)) for w in ws])
    args = list(xs) + list(ws)
    epilogue = "plain"
    if residual is not None:
        epilogue = "residual"
        h, mod3, gate_chunk, (n_lat, seq, n_batch) = residual
        row = functools.partial(_mod_row, tile=tm, n_lat=n_lat, seq=seq, n_batch=n_batch)
        nj = mod3.shape[2] // 6 // tn
        in_specs += [pl.BlockSpec((tm, tn), lambda i, j: (i, j)),
                     pl.BlockSpec((None, 1, tn), lambda i, j: (row(i), 0, gate_chunk * nj + j))]
        args += [h, mod3]
    elif head_norm is not None:
        epilogue = "head_norm"
        w_cols, rope = head_norm
        in_specs.append(pl.BlockSpec((1, tn), lambda i, j: (0, j)))
        args.append(w_cols)
        if rope is not None:
            epilogue = "head_norm_rope"
            cos, sin, seq, n_lat = rope
            per_seq, lat_tiles = seq // tm, n_lat // tm
            tab = pl.BlockSpec((tm, HEAD_DIM), lambda i, j: (jnp.where(i < lat_tiles, i % per_seq, per_seq), 0))
            in_specs += [tab, tab]
            args += [cos, sin]
    return pl.pallas_call(
        functools.partial(_matmul_kernel, n_in=len(xs), epilogue=epilogue),
        grid=(n_rows // tm, n_out // tn),
        in_specs=in_specs,
        out_specs=pl.BlockSpec((tm, tn), lambda i, j: (i, j)),
        out_shape=jax.ShapeDtypeStruct((n_rows, n_out), out_dtype),
        compiler_params=_cparams("parallel", "parallel"),
        name="matmul_" + epilogue,
    )(*args)


CONV_PAD = 8


def _conv_silu_kernel(x_ref, w_ref, b_ref, o_ref, pad_ref):
    seq, width = o_ref.shape
    zeros = jnp.zeros((CONV_PAD, width), F32)
    pad_ref[pl.ds(0, CONV_PAD), :] = zeros
    pad_ref[pl.ds(CONV_PAD + seq, CONV_PAD), :] = zeros
    pad_ref[pl.ds(CONV_PAD, seq), :] = x_ref[...].astype(F32)
    rows = 256
    for base in range(0, seq, rows):
        acc = jnp.broadcast_to(b_ref[...], (rows, width))
        for k in range(SSD_CONV):
            acc = acc + w_ref[pl.ds(k, 1), :] * pad_ref[pl.ds(base + CONV_PAD - SSD_CONV // 2 + k, rows), :]
        o_ref[pl.ds(base, rows), :] = (acc * jax.nn.sigmoid(acc)).astype(o_ref.dtype)


def conv_silu(p_zx, conv_w, conv_b, n_seq, seq, row_off):
    tc = 512
    col0 = SSD_INNER // tc
    off = row_off // seq
    return pl.pallas_call(
        _conv_silu_kernel,
        grid=(n_seq, SSD_XBC // tc),
        in_specs=[pl.BlockSpec((seq, tc), lambda b, j: (b + off, col0 + j)),
                  pl.BlockSpec((SSD_CONV, tc), lambda b, j: (0, j)),
                  pl.BlockSpec((1, tc), lambda b, j: (0, j))],
        out_specs=pl.BlockSpec((seq, tc), lambda b, j: (b, j)),
        out_shape=jax.ShapeDtypeStruct((n_seq * seq, SSD_XBC), BF16),
        scratch_shapes=[pltpu.VMEM((seq + 2 * CONV_PAD, tc), F32)],
        compiler_params=_cparams("parallel", "parallel"),
        name="conv_silu",
    )(p_zx, conv_w, conv_b.reshape(1, SSD_XBC))


def _ssd_dt_kernel(p_ref, bias_ref, a_ref, dt_ref, acs_ref, acst_ref):
    n = p_ref.shape[0]
    pre = p_ref[...] + bias_ref[...]
    dt = jnp.maximum(pre, 0.0) + jnp.log1p(jnp.exp(-jnp.abs(pre)))
    a = dt * a_ref[...]
    li = lax.broadcasted_iota(jnp.int32, (n, n), 0)
    si = lax.broadcasted_iota(jnp.int32, (n, n), 1)
    lower = (si <= li).astype(F32)
    upper = (si >= li).astype(F32)
    fwd = jnp.dot(lower, a, precision=HIGHEST, preferred_element_type=F32)
    bwd = jnp.dot(upper, a, precision=HIGHEST, preferred_element_type=F32)
    lane = lax.broadcasted_iota(jnp.int32, fwd.shape, 1)
    acs = jnp.where(lane < SSD_HEADS, fwd, bwd)
    dt_ref[...] = dt
    acs_ref[...] = acs
    acst_ref[...] = acs.T


def ssd_dt(p_dt, dt_bias, a_log):
    rows = p_dt.shape[0]
    pad = LANES - 2 * SSD_HEADS
    bias = jnp.pad(dt_bias.reshape(1, 2 * SSD_HEADS), ((0, 0), (0, pad)))
    a_neg = jnp.pad(-jnp.exp(a_log.astype(F32)).reshape(1, 2 * SSD_HEADS), ((0, 0), (0, pad)))
    c = SSD_CHUNK
    return pl.pallas_call(
        _ssd_dt_kernel,
        grid=(rows // c,),
        in_specs=[pl.BlockSpec((c, LANES), lambda i: (i, 0)),
                  pl.BlockSpec((1, LANES), lambda i: (0, 0)),
                  pl.BlockSpec((1, LANES), lambda i: (0, 0))],
        out_specs=[pl.BlockSpec((c, LANES), lambda i: (i, 0)),
                   pl.BlockSpec((c, LANES), lambda i: (i, 0)),
                   pl.BlockSpec((LANES, c), lambda i: (0, i))],
        out_shape=[jax.ShapeDtypeStruct((rows, LANES), F32),
                   jax.ShapeDtypeStruct((rows, LANES), F32),
                   jax.ShapeDtypeStruct((LANES, rows), F32)],
        compiler_params=_cparams("parallel"),
        name="ssd_dt",
    )(p_dt, bias, a_neg)


def _ssd_scan_kernel(*refs):
    fwd, bwd, outs = refs[:6], refs[6:12], refs[12:]

    @pl.when(pl.program_id(2) == 0)
    def _():
        for state_ref in outs[2:]:
            state_ref[...] = jnp.zeros_like(state_ref)

    _ssd_chunk(*fwd, outs[0], outs[2], reverse=False)
    _ssd_chunk(*bwd, outs[1], outs[3], reverse=True)


def _ssd_chunk(x_ref, b_ref, c_ref, dt_ref, acs_ref, acsr_ref, y_ref, state_ref, *, reverse):
    n = x_ref.shape[0]
    heads = dt_ref.shape[1]
    pair_w = 2 * SSD_HEAD_DIM
    x = x_ref[...].astype(F32)
    bm = b_ref[...]
    cm = c_ref[...]
    dt = dt_ref[...]
    acs = acs_ref[...]
    acs_row = acsr_ref[...]
    end = 0 if reverse else n - 1
    total = acs[end:end + 1, :]
    e_in = jnp.exp(acs)
    e_out = jnp.exp(total - acs)
    e_tot = jnp.exp(total)

    li = lax.broadcasted_iota(jnp.int32, (n, n), 0)
    si = lax.broadcasted_iota(jnp.int32, (n, n), 1)
    visible = (si >= li) if reverse else (si <= li)
    lo = lax.broadcasted_iota(jnp.int32, (1, pair_w), 1) < SSD_HEAD_DIM

    cb = lax.dot_general(cm, bm, (((1,), (1,)), ((), ())), preferred_element_type=F32)
    state = state_ref[...]
    y_off = jnp.dot(cm, state.astype(BF16), preferred_element_type=F32)

    xd_parts, decay_parts = [], []
    for i in range(heads // 2):
        j0, j1 = 2 * i, 2 * i + 1
        cols = slice(i * pair_w, (i + 1) * pair_w)
        xp = x[:, cols] * jnp.where(lo, dt[:, j0:j0 + 1], dt[:, j1:j1 + 1])
        y_diag = None
        for j, keep in ((j0, lo), (j1, jnp.logical_not(lo))):
            diff = acs[:, j:j + 1] - acs_row[j:j + 1, :]
            m = (cb * jnp.exp(jnp.where(visible, diff, MASK_VALUE))).astype(BF16)
            part = jnp.dot(m, jnp.where(keep, xp, 0.0).astype(BF16), preferred_element_type=F32)
            y_diag = part if y_diag is None else y_diag + part
        y_ref[:, cols] = y_diag + y_off[:, cols] * jnp.where(lo, e_in[:, j0:j0 + 1], e_in[:, j1:j1 + 1])
        xd_parts.append((xp * jnp.where(lo, e_out[:, j0:j0 + 1], e_out[:, j1:j1 + 1])).astype(BF16))
        decay_parts.append(jnp.where(lo, e_tot[:, j0:j0 + 1], e_tot[:, j1:j1 + 1]))
    xd = jnp.concatenate(xd_parts, axis=1)
    decay = jnp.concatenate(decay_parts, axis=1)
    bt = bm.astype(F32).T.astype(BF16)
    state_ref[...] = state * decay + jnp.dot(bt, xd, preferred_element_type=F32)


def ssd_scan(xbc, dt_col, acs_col, acs_row, n_batch, seq, ctx_len):
    c = SSD_CHUNK
    lat_chunks, ctx_chunks = seq // c, ctx_len // c
    ctx_base = n_batch * lat_chunks
    n_steps = lat_chunks + ctx_chunks
    gw = SSD_GROUP_W
    heads = SSD_HEADS // SSD_GROUPS
    b_col0 = SSD_INNER // SSD_STATE
    c_col0 = (SSD_INNER + SSD_GN) // SSD_STATE

    def specs(reverse):
        def chunk(b, k):
            kc = (ctx_chunks - 1 - k) if reverse else k
            kl = (n_steps - 1 - k) if reverse else (k - ctx_chunks)
            return jnp.where(k < ctx_chunks, ctx_base + b * ctx_chunks + kc, b * lat_chunks + kl)

        d = int(reverse)
        ins = [pl.BlockSpec((c, gw), lambda b, g, k: (chunk(b, k), g)),
               pl.BlockSpec((c, SSD_STATE), lambda b, g, k: (chunk(b, k), b_col0 + g)),
               pl.BlockSpec((c, SSD_STATE), lambda b, g, k: (chunk(b, k), c_col0 + g)),
               pl.BlockSpec((None, None, c, heads), lambda b, g, k: (d, g, chunk(b, k), 0)),
               pl.BlockSpec((None, None, c, heads), lambda b, g, k: (d, g, chunk(b, k), 0)),
               pl.BlockSpec((None, None, heads, c), lambda b, g, k: (d, g, 0, chunk(b, k)))]
        return ins, pl.BlockSpec((c, gw), lambda b, g, k: (chunk(b, k), g))

    (in_f, out_f), (in_b, out_b) = specs(False), specs(True)
    y_shape = jax.ShapeDtypeStruct((xbc.shape[0], SSD_INNER), F32)
    args = (xbc, xbc, xbc, dt_col, acs_col, acs_row)
    return pl.pallas_call(
        _ssd_scan_kernel,
        grid=(n_batch, SSD_GROUPS, n_steps),
        in_specs=in_f + in_b,
        out_specs=[out_f, out_b],
        out_shape=[y_shape, y_shape],
        scratch_shapes=[pltpu.VMEM((SSD_STATE, gw), F32), pltpu.VMEM((SSD_STATE, gw), F32)],
        compiler_params=_cparams("parallel", "parallel", "arbitrary"),
        name="ssd_scan",
    )(*args, *args)


def _ssd_gate_kernel(yf_ref, yb_ref, x_ref, z_ref, d_ref, w_ref, o_ref):
    z = z_ref[...].astype(F32)
    y = yf_ref[...] + yb_ref[...] + d_ref[...] * x_ref[...].astype(F32)
    g = y * (z * jax.nn.sigmoid(z))
    o = g * lax.rsqrt(jnp.mean(g * g, axis=-1, keepdims=True) + EPS) * w_ref[...]
    o_ref[...] = o.astype(o_ref.dtype)


def ssd_gate(y_f, y_b, xbc, p_zx, d_skip, norm_w):
    rows = y_f.shape[0]
    t, gw = ROW_BLOCK, SSD_GROUP_W
    d_cols = jnp.repeat(d_skip.astype(F32), SSD_HEAD_DIM).reshape(1, SSD_INNER)
    spec = pl.BlockSpec((t, gw), lambda i, g: (i, g))
    vec = pl.BlockSpec((1, gw), lambda i, g: (0, g))
    return pl.pallas_call(
        _ssd_gate_kernel,
        grid=(rows // t, SSD_GROUPS),
        in_specs=[spec, spec, spec, spec, vec, vec],
        out_specs=spec,
        out_shape=jax.ShapeDtypeStruct((rows, SSD_INNER), BF16),
        compiler_params=_cparams("parallel", "parallel"),
        name="ssd_gate",
    )(y_f, y_b, xbc, p_zx, d_cols, norm_w.reshape(1, SSD_INNER))


ROPE_IDENTITY_ROWS = 1024


def _rope_tables(seq):
    t = jnp.arange(seq, dtype=jnp.int32)
    row = (t // GRID_W).astype(F32)
    col = (t % GRID_W).astype(F32)
    axis_dim = HEAD_DIM // 2
    inv = 1.0 / (ROPE_THETA ** (jnp.arange(0, axis_dim, 2, dtype=F32) / axis_dim))
    ang = jnp.concatenate([row[:, None] * inv[None], col[:, None] * inv[None]], axis=-1)
    cos = jnp.repeat(jnp.cos(ang), 2, axis=-1)
    sin = jnp.stack([-jnp.sin(ang), jnp.sin(ang)], axis=-1).reshape(seq, HEAD_DIM)
    identity = jnp.ones((ROPE_IDENTITY_ROWS, HEAD_DIM), F32)
    return jnp.concatenate([cos, identity], axis=0), jnp.concatenate([sin, 0.0 * identity], axis=0)


LOG2E = math.log2(math.e)
KEY_CHUNK = 256
DIFF_Q_TILE = 512


def _scores(q, k, bias):
    s = lax.dot_general(q, k, (((1,), (1,)), ((), ())), preferred_element_type=F32)
    return s if bias is None else s + bias


def _attend_two_pass(q, pieces):
    scores = [_scores(q, k, bias) for k, _, bias in pieces]
    m = functools.reduce(jnp.maximum, [jnp.max(s, axis=-1, keepdims=True) for s in scores])
    denom = acc = None
    for s, (_, v, _) in zip(scores, pieces):
        p = jnp.exp2(s - m)
        p_sum = jnp.sum(p, axis=-1, keepdims=True)
        pv = jnp.dot(p.astype(v.dtype), v, preferred_element_type=F32)
        denom = p_sum if denom is None else denom + p_sum
        acc = pv if acc is None else acc + pv
    return acc / denom


def _attend(q, pieces):
    chunks = []
    for k, v, bias in pieces:
        for c0 in range(0, k.shape[0], KEY_CHUNK):
            c1 = min(c0 + KEY_CHUNK, k.shape[0])
            chunks.append((k[c0:c1], v[c0:c1], None if bias is None else bias[:, c0:c1]))
    m = denom = acc = None
    for k, v, bias in chunks:
        s = lax.dot_general(q, k, (((1,), (1,)), ((), ())), preferred_element_type=F32)
        if bias is not None:
            s = s + bias
        m_chunk = jnp.max(s, axis=-1, keepdims=True)
        m_new = m_chunk if m is None else jnp.maximum(m, m_chunk)
        p = jnp.exp2(s - m_new)
        p_sum = jnp.sum(p, axis=-1, keepdims=True)
        pv = jnp.dot(p.astype(v.dtype), v, preferred_element_type=F32)
        if m is None:
            denom, acc = p_sum, pv
        else:
            alpha = jnp.exp2(m - m_new)
            denom = alpha * denom + p_sum
            acc = alpha * acc + pv
        m = m_new
    return acc / denom


NA_HEADS_PER_STEP = 4


def _na_kernel(q_ref, k0_ref, k1_ref, k2_ref, kc_ref, v0_ref, v1_ref, v2_ref, vc_ref, bias_ref, o_ref):
    t = q_ref.shape[0]
    for hd in range(NA_HEADS_PER_STEP):
        cols = slice(hd * HEAD_DIM, (hd + 1) * HEAD_DIM)
        pieces = [(k_ref[:, cols], v_ref[:, cols], bias_ref[hd, :, i * t:(i + 1) * t])
                  for i, (k_ref, v_ref) in enumerate(((k0_ref, v0_ref), (k1_ref, v1_ref), (k2_ref, v2_ref)))]
        pieces.append((kc_ref[:, cols], vc_ref[:, cols], None))
        o_ref[:, cols] = _attend_two_pass(q_ref[:, cols], pieces).astype(o_ref.dtype)


def _na_bias_table(rpb, rows):
    n_blocks = rows // NA_QROWS
    w0_of = lambda j: min(max(NA_QROWS * j - NA_WIN_ROWS // 2, 0), rows - NA_KROWS)
    geoms, pattern_of = [], []
    for j in range(n_blocks):
        r0s = tuple(min(max(NA_QROWS * j + a - NA_WIN_ROWS // 2, 0), rows - NA_WIN_ROWS) - w0_of(j)
                    for a in range(NA_QROWS))
        key = (NA_QROWS * j - w0_of(j), r0s)
        if key not in geoms:
            geoms.append(key)
        pattern_of.append(geoms.index(key))
    qc = jnp.arange(GRID_W)[:, None]
    kc = jnp.arange(GRID_W)[None, :]
    cs = jnp.clip(qc - NA_WIN_COLS // 2, 0, GRID_W - NA_WIN_COLS)
    col_ok = (kc >= cs) & (kc < cs + NA_WIN_COLS)
    dc = jnp.clip(kc - qc + NA_WIN_COLS - 1, 0, 2 * NA_WIN_COLS - 2)
    pick = (dc[None] == jnp.arange(2 * NA_WIN_COLS - 1)[:, None, None]).astype(F32)
    tiles = jnp.einsum('hrd,dqk->hrqk', rpb.astype(F32), pick, precision=HIGHEST)
    tiles = jnp.where(col_ok, tiles * LOG2E, MASK_VALUE)
    masked = jnp.full((rpb.shape[0], GRID_W, GRID_W), MASK_VALUE, F32)
    tables = []
    for q_row0, r0s in geoms:
        tile_rows = []
        for qi in range(NA_QROWS):
            row = [tiles[:, ki - (q_row0 + qi) + NA_WIN_ROWS - 1] if r0s[qi] <= ki < r0s[qi] + NA_WIN_ROWS
                   else masked for ki in range(NA_KROWS)]
            tile_rows.append(jnp.concatenate(row, axis=2))
        tables.append(jnp.concatenate(tile_rows, axis=1))
    tables.append(jnp.full_like(tables[0], MASK_VALUE))
    pattern_of.append(len(geoms))
    w0_blocks = [w0_of(j) // NA_QROWS for j in range(n_blocks)] + [0]
    return jnp.stack(tables, axis=1), pattern_of, w0_blocks


def _lookup(values, j):
    out = jnp.int32(values[-1])
    for idx in range(len(values) - 2, -1, -1):
        out = jnp.where(j == idx, jnp.int32(values[idx]), out)
    return out


def na_attention(qn, kn, p_na, v_col_off, rpb, n_batch, seq, ctx_len):
    t = NA_QROWS * GRID_W
    assert t == ROW_BLOCK == ctx_len
    rows = seq // GRID_W
    n_blocks = seq // t
    bias, pattern_of, w0_blocks = _na_bias_table(rpb, rows)
    hw = NA_HEADS_PER_STEP * HEAD_DIM
    v_col0 = v_col_off // hw
    ctx_base = n_batch * n_blocks

    def q_blk(j, b):
        return jnp.where(j < n_blocks, b * n_blocks + j, ctx_base + b)

    def kv_blk(piece):
        return lambda h, j, b: (b * n_blocks + _lookup(w0_blocks, j) + piece, h)

    def kv_blk_v(piece):
        return lambda h, j, b: (b * n_blocks + _lookup(w0_blocks, j) + piece, v_col0 + h)

    blk = lambda fn: pl.BlockSpec((t, hw), fn)
    in_specs = ([blk(lambda h, j, b: (q_blk(j, b), h))]
                + [blk(kv_blk(i)) for i in range(3)] + [blk(lambda h, j, b: (ctx_base + b, h))]
                + [blk(kv_blk_v(i)) for i in range(3)] + [blk(lambda h, j, b: (ctx_base + b, v_col0 + h))]
                + [pl.BlockSpec((NA_HEADS_PER_STEP, None, t, 3 * t),
                                lambda h, j, b: (h, _lookup(pattern_of, j), 0, 0))])
    return pl.pallas_call(
        _na_kernel,
        grid=(NA_HEADS // NA_HEADS_PER_STEP, n_blocks + 1, n_batch),
        in_specs=in_specs,
        out_specs=blk(lambda h, j, b: (q_blk(j, b), h)),
        out_shape=jax.ShapeDtypeStruct((qn.shape[0], NA_WIDTH), BF16),
        compiler_params=_cparams("parallel", "parallel", "parallel"),
        name="na_attention",
    )(qn, kn, kn, kn, kn, p_na, p_na, p_na, p_na, bias)


def _gqa_kernel(q_ref, kl_ref, kc_ref, vl_ref, vc_ref, o_ref):
    t = q_ref.shape[0]
    q = jnp.concatenate([q_ref[:, g * HEAD_DIM:(g + 1) * HEAD_DIM] for g in range(GQA_GROUP)], axis=0)
    o = _attend(q, [(kc_ref[...], vc_ref[...], None), (kl_ref[...], vl_ref[...], None)])
    for g in range(GQA_GROUP):
        o_ref[:, g * HEAD_DIM:(g + 1) * HEAD_DIM] = o[g * t:(g + 1) * t].astype(o_ref.dtype)


def gqa_attention(qn, kn, p_kv, v_col_off, n_batch, seq, ctx_len):
    t = ROW_BLOCK
    q_tiles = seq // t
    ctx_base = n_batch * seq // ctx_len
    v0 = v_col_off // HEAD_DIM
    gw = GQA_GROUP * HEAD_DIM
    return pl.pallas_call(
        _gqa_kernel,
        grid=(n_batch, GQA_KV_HEADS, q_tiles),
        in_specs=[pl.BlockSpec((t, gw), lambda b, h, i: (b * q_tiles + i, h)),
                  pl.BlockSpec((seq, HEAD_DIM), lambda b, h, i: (b, h)),
                  pl.BlockSpec((ctx_len, HEAD_DIM), lambda b, h, i: (ctx_base + b, h)),
                  pl.BlockSpec((seq, HEAD_DIM), lambda b, h, i: (b, v0 + h)),
                  pl.BlockSpec((ctx_len, HEAD_DIM), lambda b, h, i: (ctx_base + b, v0 + h))],
        out_specs=pl.BlockSpec((t, gw), lambda b, h, i: (b * q_tiles + i, h)),
        out_shape=jax.ShapeDtypeStruct((n_batch * seq, GQA_Q), BF16),
        compiler_params=_cparams("parallel", "parallel", "parallel"),
        name="gqa_attention",
    )(qn, kn, kn, p_kv, p_kv)


def _diff_kernel(lam_ref, q_ref, kl_ref, kc_ref, vl_ref, vc_ref, w_ref, o_ref, *, out_scale):
    outs = []
    for comp in range(2):
        cols = slice(comp * HEAD_DIM, (comp + 1) * HEAD_DIM)
        outs.append(_attend(q_ref[:, cols], [(kc_ref[:, cols], vc_ref[...], None),
                                              (kl_ref[:, cols], vl_ref[...], None)]))
    o = outs[0] - lam_ref[0] * outs[1]
    o = o * lax.rsqrt(jnp.mean(o * o, axis=-1, keepdims=True) + EPS) * w_ref[...]
    o_ref[...] = (o * out_scale).astype(o_ref.dtype)


def diff_attention(qn, q_col_off, kn, p_kv, v_col_off, lam, subln_w, out_scale, n_batch, seq, ctx_len):
    t = DIFF_Q_TILE
    q_tiles = seq // t
    ctx_base = n_batch * seq // ctx_len
    v0 = v_col_off // DIFF_V_DIM
    w2 = 2 * HEAD_DIM
    q0 = q_col_off // w2
    return pl.pallas_call(
        functools.partial(_diff_kernel, out_scale=out_scale),
        grid=(n_batch, DIFF_HEADS, q_tiles),
        in_specs=[pl.BlockSpec(memory_space=pltpu.SMEM),
                  pl.BlockSpec((t, w2), lambda b, h, i: (b * q_tiles + i, q0 + h)),
                  pl.BlockSpec((seq, w2), lambda b, h, i: (b, h)),
                  pl.BlockSpec((ctx_len, w2), lambda b, h, i: (ctx_base + b, h)),
                  pl.BlockSpec((seq, DIFF_V_DIM), lambda b, h, i: (b, v0 + h)),
                  pl.BlockSpec((ctx_len, DIFF_V_DIM), lambda b, h, i: (ctx_base + b, v0 + h)),
                  pl.BlockSpec((1, DIFF_V_DIM), lambda b, h, i: (0, 0))],
        out_specs=pl.BlockSpec((t, DIFF_V_DIM), lambda b, h, i: (b * q_tiles + i, h)),
        out_shape=jax.ShapeDtypeStruct((n_batch * seq, DIFF_V), BF16),
        compiler_params=_cparams("parallel", "parallel", "parallel"),
        name="diff_attention",
    )(lam.reshape(1), qn, kn, kn, p_kv, p_kv, subln_w.reshape(1, DIFF_V_DIM))


def _moe_plan_kernel(route_ref, pos_ref, pad_end_ref, counts_ref, start_ref):
    phase, i = pl.program_id(0), pl.program_id(1)
    t = route_ref.shape[0]
    route = route_ref[...]
    lane = lax.broadcasted_iota(jnp.int32, route.shape, 1)
    chosen = [lane == route[:, ROUTE_EXPERT_LANE + k:ROUTE_EXPERT_LANE + k + 1].astype(jnp.int32)
              for k in range(MOE_TOPK)]
    both = functools.reduce(jnp.add, [c.astype(F32) for c in chosen])
    tile_counts = jnp.sum(both, axis=0, keepdims=True)

    @pl.when((phase == 0) & (i == 0))
    def _():
        counts_ref[...] = jnp.zeros_like(counts_ref)

    @pl.when(phase == 0)
    def _():
        counts_ref[...] += tile_counts

    @pl.when((phase == 1) & (i == 0))
    def _():
        padded = jnp.ceil(counts_ref[...] * (1.0 / MOE_BLOCK)) * MOE_BLOCK
        rows = pad_end_ref.shape[0]
        ki = lax.broadcasted_iota(jnp.int32, (LANES, LANES), 0)
        ni = lax.broadcasted_iota(jnp.int32, (LANES, LANES), 1)
        pad_end = jnp.dot(jnp.broadcast_to(padded, (rows, LANES)), (ki <= ni).astype(F32),
                          precision=HIGHEST, preferred_element_type=F32)
        pad_end_ref[...] = pad_end
        start_ref[...] = pad_end[0:1] - padded

    @pl.when(phase == 1)
    def _():
        li = lax.broadcasted_iota(jnp.int32, (t, t), 0)
        si = lax.broadcasted_iota(jnp.int32, (t, t), 1)
        earlier = jnp.dot((si < li).astype(BF16), both.astype(BF16), preferred_element_type=F32)
        row_of = earlier + start_ref[...]
        out = jnp.zeros(route.shape, F32)
        for k in range(MOE_TOPK):
            p = jnp.sum(jnp.where(chosen[k], row_of, 0.0), axis=-1, keepdims=True)
            out = jnp.where(lane == k, p, out)
        pos_ref[...] = out.astype(jnp.int32)
        start_ref[...] += tile_counts


def moe_plan(route):
    n = route.shape[0]
    t = ROW_BLOCK
    return pl.pallas_call(
        _moe_plan_kernel,
        grid=(2, n // t),
        in_specs=[pl.BlockSpec((t, LANES), lambda p, i: (i, 0))],
        out_specs=[pl.BlockSpec((t, LANES), lambda p, i: (i * p, 0)),
                   pl.BlockSpec((8, LANES), lambda p, i: (0, 0))],
        out_shape=[jax.ShapeDtypeStruct((n, LANES), jnp.int32),
                   jax.ShapeDtypeStruct((8, LANES), F32)],
        scratch_shapes=[pltpu.VMEM((1, LANES), F32), pltpu.VMEM((1, LANES), F32)],
        compiler_params=_cparams("arbitrary", "arbitrary"),
        name="moe_plan",
    )(route)


def _moe_kernel(be_ref, next_ref, nused_ref, x_ref, w1_hbm, w3_hbm, w2_hbm, o_ref,
                s1, s3, s2, w1b, w3b, w2b, sems, *, layer):
    i = pl.program_id(0)

    def weight_copies(e):
        return [pltpu.make_async_copy(w_hbm.at[layer, e], stage, sems.at[n])
                for n, (w_hbm, stage) in enumerate(((w1_hbm, s1), (w3_hbm, s3), (w2_hbm, s2)))]

    @pl.when(i == 0)
    def _():
        for c in weight_copies(be_ref[0]):
            c.start()

    prev = be_ref[jnp.maximum(i - 1, 0)]

    @pl.when((i == 0) | (be_ref[i] != prev))
    def _():
        for c in weight_copies(be_ref[i]):
            c.wait()
        w1b[...] = s1[...].astype(BF16)
        w3b[...] = s3[...].astype(BF16)
        w2b[...] = s2[...].astype(BF16)

        @pl.when(next_ref[i] >= 0)
        def _():
            for c in weight_copies(next_ref[i]):
                c.start()

    @pl.when(i < nused_ref[0])
    def _():
        x = x_ref[...].astype(BF16)
        a = jnp.dot(x, w1b[...], preferred_element_type=F32)
        b = jnp.dot(x, w3b[...], preferred_element_type=F32)
        hid = (a * jax.nn.sigmoid(a)) * b
        o_ref[...] = jnp.dot(hid.astype(BF16), w2b[...], preferred_element_type=F32)

    @pl.when(i >= nused_ref[0])
    def _():
        o_ref[...] = jnp.zeros_like(o_ref)


def moe_experts(x_sorted, block_e, next_e, n_used, w1, w3, w2, layer):
    n_blocks = block_e.shape[0]
    d, hid = w1.shape[2], w1.shape[3]
    hbm = pl.BlockSpec(memory_space=pl.ANY)
    grid_spec = pltpu.PrefetchScalarGridSpec(
        num_scalar_prefetch=3,
        grid=(n_blocks,),
        in_specs=[pl.BlockSpec((MOE_BLOCK, d), lambda i, be, nx, nu: (i, 0)), hbm, hbm, hbm],
        out_specs=pl.BlockSpec((MOE_BLOCK, d), lambda i, be, nx, nu: (i, 0)),
        scratch_shapes=[pltpu.VMEM((d, hid), F32), pltpu.VMEM((d, hid), F32), pltpu.VMEM((hid, d), F32),
                        pltpu.VMEM((d, hid), BF16), pltpu.VMEM((d, hid), BF16), pltpu.VMEM((hid, d), BF16),
                        pltpu.SemaphoreType.DMA((3,))],
    )
    return pl.pallas_call(
        functools.partial(_moe_kernel, layer=layer),
        grid_spec=grid_spec,
        out_shape=jax.ShapeDtypeStruct((n_blocks * MOE_BLOCK, d), F32),
        compiler_params=_cparams("arbitrary"),
        name="moe_experts",
    )(block_e, next_e, n_used, x_sorted, w1, w3, w2)


def _combine_kernel(h_ref, g_ref, route_ref, *refs):
    y_refs, o_ref = refs[:-1], refs[-1]
    acc = None
    for k, y_ref in enumerate(y_refs):
        part = y_ref[...] * route_ref[:, ROUTE_GATE_LANE + k:ROUTE_GATE_LANE + k + 1]
        acc = part if acc is None else acc + part
    o_ref[...] = h_ref[...] + g_ref[...] * acc


def moe_combine(h, mod3, gate_chunk, route, ys, geom):
    n_lat, seq, n_batch = geom
    n_rows, d = ys[0].shape
    t = ROW_BLOCK
    row = functools.partial(_mod_row, tile=t, n_lat=n_lat, seq=seq, n_batch=n_batch)
    spec = pl.BlockSpec((t, d), lambda i: (i, 0))
    return pl.pallas_call(
        _combine_kernel,
        grid=(n_rows // t,),
        in_specs=[spec, pl.BlockSpec((None, 1, d), lambda i: (row(i), 0, gate_chunk)),
                  pl.BlockSpec((t, LANES), lambda i: (i, 0))] + [spec] * len(ys),
        out_specs=spec,
        out_shape=jax.ShapeDtypeStruct((n_rows, d), F32),
        compiler_params=_cparams("parallel"),
        name="moe_combine",
    )(h, mod3, route, *ys)


def moe_layer(h, f, route, mod3, gate_chunk, w1, w3, w2, layer, geom):
    n = f.shape[0]
    n_assign = n * MOE_TOPK
    n_blocks = -(-n_assign // MOE_BLOCK) + MOE_EXPERTS
    pos, pad_end = moe_plan(route)
    pos = pos[:, :MOE_TOPK]
    pad_end = pad_end[0, :MOE_EXPERTS].astype(jnp.int32)
    n_used = pad_end[-1] // MOE_BLOCK
    blk = jnp.minimum(jnp.arange(n_blocks, dtype=jnp.int32), n_used - 1) * MOE_BLOCK
    block_e = jnp.minimum(jnp.sum(pad_end[None, :] <= blk[:, None], axis=1, dtype=jnp.int32), MOE_EXPERTS - 1)
    seg_end = pad_end[block_e] // MOE_BLOCK
    next_e = jnp.where(seg_end < n_used, block_e[jnp.minimum(seg_end, n_blocks - 1)], -1)
    tok = jnp.broadcast_to(jnp.arange(n, dtype=jnp.int32)[:, None], (n, MOE_TOPK))
    slot_tok = (jnp.arange(n_blocks * MOE_BLOCK, dtype=jnp.int32) % n).at[pos.reshape(-1)].set(
        tok.reshape(-1), unique_indices=True, mode='promise_in_bounds')
    x_sorted = f.at[slot_tok].get(mode='promise_in_bounds')
    yb = moe_experts(x_sorted, block_e, next_e, n_used.reshape(1), w1, w3, w2, layer)
    ys = [yb.at[pos[:, k]].get(mode='promise_in_bounds') for k in range(MOE_TOPK)]
    return moe_combine(h, mod3, gate_chunk, route, ys, geom)


def _even_layer_mix(a, in_w, conv_w, conv_b, dt_bias, a_log, d_skip, ssd_norm_w, q_norm, k_norm, rpb,
                    n_batch, seq, ctx_len):
    n_rows = a.shape[0]
    n_lat = n_batch * seq
    zx_end = SSD_INNER + SSD_XBC
    dt_end = zx_end + 2 * SSD_HEADS
    w_zx = in_w[:, :zx_end].astype(BF16)
    w_dt = jnp.pad(in_w[:, zx_end:dt_end], ((0, 0), (0, LANES - 2 * SSD_HEADS))).astype(BF16)
    na_cols = lambda i: in_w[:, dt_end + i * NA_WIDTH:dt_end + (i + 1) * NA_WIDTH].astype(BF16)
    tile_w = lambda w, s: (jnp.tile(w.astype(F32), NA_HEADS) * s).reshape(1, NA_WIDTH)
    p_zx = matmul([a], [w_zx], n_rows, BF16, tn=512)
    p_dt = matmul([a], [w_dt], n_rows, F32, tn=LANES)
    qn = matmul([a], [na_cols(0)], n_rows, BF16, tn=512,
                head_norm=(tile_w(q_norm, HEAD_DIM ** -0.5 * LOG2E), None))
    kn = matmul([a], [na_cols(1)], n_rows, BF16, tn=512, head_norm=(tile_w(k_norm, 1.0), None))
    p_v = matmul([a], [na_cols(2)], n_rows, BF16, tn=512)

    xbc = jnp.concatenate([conv_silu(p_zx, conv_w, conv_b, n_batch, seq, 0),
                           conv_silu(p_zx, conv_w, conv_b, n_batch, ctx_len, n_lat)], axis=0)
    dt, acs, acs_t = ssd_dt(p_dt, dt_bias, a_log)
    hg = SSD_HEADS // SSD_GROUPS
    col = lambda v: v[:, :2 * SSD_HEADS].reshape(n_rows, 2, SSD_GROUPS, hg).transpose(1, 2, 0, 3)
    dt_col, acs_col = col(dt), col(acs)
    acs_row = acs_t[:2 * SSD_HEADS].reshape(2, SSD_GROUPS, hg, n_rows)
    y_f, y_b = ssd_scan(xbc, dt_col, acs_col, acs_row, n_batch, seq, ctx_len)
    s_out = ssd_gate(y_f, y_b, xbc, p_zx, d_skip, ssd_norm_w)

    n_out = na_attention(qn, kn, p_v, 0, rpb, n_batch, seq, ctx_len)
    return s_out, n_out


def _odd_layer_mix(a, in_w, gq_norm, gk_norm, dq_norm, dk_norm, lam_vecs, subln_w, lambda_init,
                   n_batch, seq, ctx_len):
    n_rows = a.shape[0]
    n_lat = n_batch * seq
    cos, sin = _rope_tables(seq)
    rope = (cos, sin, seq, n_lat)
    scale = HEAD_DIM ** -0.5 * LOG2E
    rep = lambda w, n: jnp.tile(w.astype(F32), n).reshape(1, n * HEAD_DIM)
    cols = lambda lo, hi: in_w[:, lo:hi].astype(BF16)
    gk_off, gv_off, dk_off, dv_off = ODD_Q, ODD_Q + GQA_KV, ODD_Q + 2 * GQA_KV, ODD_Q + 2 * GQA_KV + DIFF_QK
    q_w = jnp.concatenate([rep(gq_norm, GQA_HEADS), rep(dq_norm, 2 * DIFF_HEADS)], axis=1) * scale
    qn = matmul([a], [cols(0, ODD_Q)], n_lat, BF16, tn=512, head_norm=(q_w, rope))
    gkn = matmul([a], [cols(gk_off, gv_off)], n_rows, BF16, tn=512,
                 head_norm=(rep(gk_norm, GQA_KV_HEADS), rope))
    dkn = matmul([a], [cols(dk_off, dv_off)], n_rows, BF16, tn=512,
                 head_norm=(rep(dk_norm, 2 * DIFF_HEADS), rope))
    gv = matmul([a], [cols(gv_off, dk_off)], n_rows, BF16, tn=512)
    dv = matmul([a], [cols(dv_off, in_w.shape[1])], n_rows, BF16, tn=512)
    lv = lam_vecs.astype(F32)
    lam = jnp.exp(jnp.dot(lv[0], lv[1])) - jnp.exp(jnp.dot(lv[2], lv[3])) + lambda_init
    og = gqa_attention(qn, gkn, gv, 0, n_batch, seq, ctx_len)
    od = diff_attention(qn, GQA_Q, dkn, dv, 0, lam, subln_w, 1.0 - lambda_init, n_batch, seq, ctx_len)
    return og, od


def kernel(x, c, ctx, c_ctx, ada_w, ada_b, norm_w, ev_in_w, ev_conv_w, ev_conv_b, ev_dt_bias, ev_a_log, ev_d_skip, ev_ssd_norm_w, ev_na_q_norm, ev_na_k_norm, ev_na_rpb, ev_out_w, od_in_w, od_gqa_q_norm, od_gqa_k_norm, od_diff_q_norm, od_diff_k_norm, od_lambda, od_diff_subln, od_out_w, moe_group_w, moe_expert_w, moe_w1, moe_w3, moe_w2):
    n_batch, seq, d = x.shape
    ctx_len = ctx.shape[1]
    depth = ada_w.shape[0]
    assert depth == 2 and d == D_MODEL
    n_lat, n_ctx = n_batch * seq, n_batch * ctx_len
    n_all = n_lat + n_ctx
    geom = (n_lat, seq, n_batch)

    cc = jnp.concatenate([c, c_ctx[None, :], jnp.zeros((8 - n_batch - 1, d), F32)], axis=0)
    mod = ada_modulation(cc, ada_w, ada_b)
    router_w = lambda l: jnp.pad(jnp.concatenate([moe_group_w[l], moe_expert_w[l]], axis=1),
                                 ((0, 0), (0, LANES - MOE_GROUPS - MOE_EXPERTS)))

    h = jnp.concatenate([x.reshape(n_lat, d), ctx.reshape(n_ctx, d)], axis=0)

    mod3 = mod[0].reshape(8, 1, 6 * d)
    a = norm_modulate(h, norm_w[0, 0], mod3, 1, 0, n_all, geom)
    s_out, n_out = _even_layer_mix(a, ev_in_w[0], ev_conv_w[0], ev_conv_b[0], ev_dt_bias[0], ev_a_log[0],
                                   ev_d_skip[0], ev_ssd_norm_w[0], ev_na_q_norm[0], ev_na_k_norm[0],
                                   ev_na_rpb[0], n_batch, seq, ctx_len)
    out_w = ev_out_w[0].astype(BF16)
    h = matmul([s_out, n_out], [out_w[:SSD_INNER], out_w[SSD_INNER:]], n_all, F32, tn=512,
               residual=(h, mod3, 2, geom))
    f, route = norm_modulate(h, norm_w[0, 1], mod3, 4, 3, n_all, geom, router_w=router_w(0))
    h = moe_layer(h, f, route, mod3, 5, moe_w1, moe_w3, moe_w2, 0, geom)

    mod3 = mod[1].reshape(8, 1, 6 * d)
    a = norm_modulate(h, norm_w[1, 0], mod3, 1, 0, n_all, geom)
    lambda_init = 0.8 - 0.6 * math.exp(-0.3 * 1)
    og, od = _odd_layer_mix(a, od_in_w[0], od_gqa_q_norm[0], od_gqa_k_norm[0], od_diff_q_norm[0],
                            od_diff_k_norm[0], od_lambda[0], od_diff_subln[0], lambda_init,
                            n_batch, seq, ctx_len)
    out_w = od_out_w[0].astype(BF16)
    h = matmul([og, od], [out_w[:GQA_Q], out_w[GQA_Q:]], n_lat, F32, tn=512,
               residual=(h, mod3, 2, geom))
    f, route = norm_modulate(h, norm_w[1, 1], mod3, 4, 3, n_lat, geom, router_w=router_w(1))
    h = moe_layer(h, f, route, mod3, 5, moe_w1, moe_w3, moe_w2, 1, geom)
    return h.reshape(n_batch, seq, d)
```

```python
import functools
import math

import jax
import jax.numpy as jnp
from jax import lax
from jax.experimental import pallas as pl
from jax.experimental.pallas import tpu as pltpu

F32 = jnp.float32
BF16 = jnp.bfloat16
HIGHEST = lax.Precision.HIGHEST

D_MODEL = 2048
GRID_W = 64
EPS = 1e-6
ROPE_THETA = 10000.0
HEAD_DIM = 128

SSD_HEADS = 32
SSD_HEAD_DIM = 64
SSD_INNER = SSD_HEADS * SSD_HEAD_DIM
SSD_GROUPS = 4
SSD_STATE = 128
SSD_GN = SSD_GROUPS * SSD_STATE
SSD_XBC = SSD_INNER + 2 * SSD_GN
SSD_CONV = 5
SSD_CHUNK = 128
SSD_GROUP_W = SSD_INNER // SSD_GROUPS

NA_HEADS = 16
NA_WIDTH = NA_HEADS * HEAD_DIM
NA_WIN_ROWS = 8
NA_WIN_COLS = 16
NA_QROWS = 4
NA_KROWS = 12

GQA_HEADS = 16
GQA_KV_HEADS = 4
GQA_GROUP = GQA_HEADS // GQA_KV_HEADS
GQA_Q = GQA_HEADS * HEAD_DIM
GQA_KV = GQA_KV_HEADS * HEAD_DIM
DIFF_HEADS = 8
DIFF_QK = DIFF_HEADS * 2 * HEAD_DIM
DIFF_V_DIM = 2 * HEAD_DIM
DIFF_V = DIFF_HEADS * DIFF_V_DIM
ODD_Q = GQA_Q + DIFF_QK

MOE_GROUPS = 8
MOE_EXPERTS_PER_GROUP = 8
MOE_EXPERTS = MOE_GROUPS * MOE_EXPERTS_PER_GROUP
MOE_HIDDEN = 512
MOE_TOPK = 2
MOE_BLOCK = 128

ROW_BLOCK = 256
LANES = 128
MASK_VALUE = -1e30
VMEM_LIMIT = 52 * 1024 * 1024


def _cparams(*sem):
    return pltpu.CompilerParams(dimension_semantics=sem, vmem_limit_bytes=VMEM_LIMIT)


def _mod_row(i, tile, n_lat, seq, n_batch):
    return jnp.where(i < n_lat // tile, i // (seq // tile), n_batch)


def _ada_kernel(c_ref, w_ref, b_ref, o_ref):
    c = c_ref[...]
    s = c * jax.nn.sigmoid(c)
    o_ref[...] = jnp.dot(s, w_ref[...], precision=HIGHEST, preferred_element_type=F32) + b_ref[...]


def ada_modulation(cc, ada_w, ada_b):
    n_layers, d, n_out = ada_w.shape
    rows = cc.shape[0]
    tn = 1024
    return pl.pallas_call(
        _ada_kernel,
        grid=(n_layers, n_out // tn),
        in_specs=[pl.BlockSpec((rows, d), lambda l, j: (0, 0)),
                  pl.BlockSpec((None, d, tn), lambda l, j: (l, 0, j)),
                  pl.BlockSpec((None, 1, tn), lambda l, j: (l, 0, j))],
        out_specs=pl.BlockSpec((None, rows, tn), lambda l, j: (l, 0, j)),
        out_shape=jax.ShapeDtypeStruct((n_layers, rows, n_out), F32),
        compiler_params=_cparams("parallel", "parallel"),
        name="ada_modulation",
    )(cc, ada_w, ada_b.reshape(n_layers, 1, n_out))


def _normmod_kernel(h_ref, w_ref, sc_ref, sh_ref, *rest, with_router):
    x = h_ref[...]
    y = x * lax.rsqrt(jnp.mean(x * x, axis=-1, keepdims=True) + EPS) * w_ref[...]
    a = y * (1.0 + sc_ref[...]) + sh_ref[...]
    if with_router:
        rw_ref, o_ref, route_ref = rest
        logits = jnp.dot(a, rw_ref[...], precision=HIGHEST, preferred_element_type=F32)
        route_ref[...] = _route(logits)
    else:
        (o_ref,) = rest
    o_ref[...] = a.astype(o_ref.dtype)


ROUTE_EXPERT_LANE = 0
ROUTE_GATE_LANE = MOE_TOPK


def _route(logits):
    assert MOE_TOPK == 2
    neg = jnp.float32(-3.0e38)
    lane = lax.broadcasted_iota(jnp.int32, logits.shape, 1)
    is_group = lane < MOE_GROUPS
    gl = jnp.where(is_group, logits, neg)
    g_max = jnp.max(gl, axis=-1, keepdims=True)
    g_idx = jnp.min(jnp.where(gl == g_max, lane, LANES), axis=-1, keepdims=True)
    g_gate = 1.0 / jnp.sum(jnp.where(is_group, jnp.exp(logits - g_max), 0.0), axis=-1, keepdims=True)
    first = MOE_GROUPS + g_idx * MOE_EXPERTS_PER_GROUP
    in_group = (lane >= first) & (lane < first + MOE_EXPERTS_PER_GROUP)
    el = jnp.where(in_group, logits, neg)
    v1 = jnp.max(el, axis=-1, keepdims=True)
    i1 = jnp.min(jnp.where(in_group & (el == v1), lane, LANES), axis=-1, keepdims=True)
    rest = in_group & (lane != i1)
    el2 = jnp.where(rest, logits, neg)
    v2 = jnp.max(el2, axis=-1, keepdims=True)
    i2 = jnp.min(jnp.where(rest & (el2 == v2), lane, LANES), axis=-1, keepdims=True)
    e21 = jnp.exp(v2 - v1)
    p1 = 1.0 / (1.0 + e21)
    p2 = e21 / (1.0 + e21)
    vals = [(i1 - MOE_GROUPS).astype(F32), (i2 - MOE_GROUPS).astype(F32), p1 * g_gate, p2 * g_gate]
    out = jnp.zeros(logits.shape, F32)
    for k, v in enumerate(vals):
        out = jnp.where(lane == k, v, out)
    return out


def norm_modulate(h, w, mod3, sc_chunk, sh_chunk, n_rows, geom, router_w=None):
    n_lat, seq, n_batch = geom
    d = h.shape[1]
    t = ROW_BLOCK
    row = functools.partial(_mod_row, tile=t, n_lat=n_lat, seq=seq, n_batch=n_batch)
    in_specs = [pl.BlockSpec((t, d), lambda i: (i, 0)),
                pl.BlockSpec((1, d), lambda i: (0, 0)),
                pl.BlockSpec((None, 1, d), lambda i: (row(i), 0, sc_chunk)),
                pl.BlockSpec((None, 1, d), lambda i: (row(i), 0, sh_chunk))]
    args = [h, w.reshape(1, d), mod3, mod3]
    out_specs = pl.BlockSpec((t, d), lambda i: (i, 0))
    out_shape = jax.ShapeDtypeStruct((n_rows, d), BF16)
    if router_w is not None:
        in_specs.append(pl.BlockSpec(router_w.shape, lambda i: (0, 0)))
        args.append(router_w)
        out_specs = [out_specs, pl.BlockSpec((t, router_w.shape[1]), lambda i: (i, 0))]
        out_shape = [jax.ShapeDtypeStruct((n_rows, d), F32),
                     jax.ShapeDtypeStruct((n_rows, router_w.shape[1]), F32)]
    return pl.pallas_call(
        functools.partial(_normmod_kernel, with_router=router_w is not None),
        grid=(n_rows // t,),
        in_specs=in_specs, out_specs=out_specs, out_shape=out_shape,
        compiler_params=_cparams("parallel"),
        name="norm_modulate",
    )(*args)


def _matmul_kernel(*refs, n_in, epilogue):
    acc = None
    for x_ref, w_ref in zip(refs[:n_in], refs[n_in:2 * n_in]):
        part = jnp.dot(x_ref[...], w_ref[...], preferred_element_type=F32)
        acc = part if acc is None else acc + part
    rest = refs[2 * n_in:]
    if epilogue == "residual":
        h_ref, g_ref, o_ref = rest
        o_ref[...] = h_ref[...] + g_ref[...] * acc
    elif epilogue in ("head_norm", "head_norm_rope"):
        w_ref, o_ref = rest[0], rest[-1]
        if epilogue == "head_norm_rope":
            cos, sin = rest[1][...], rest[2][...]
            ri = lax.broadcasted_iota(jnp.int32, (HEAD_DIM, HEAD_DIM), 0)
            ci = lax.broadcasted_iota(jnp.int32, (HEAD_DIM, HEAD_DIM), 1)
            swap = (ri == ci + 1 - 2 * (ci % 2)).astype(BF16)
        for hd in range(acc.shape[1] // HEAD_DIM):
            cols = slice(hd * HEAD_DIM, (hd + 1) * HEAD_DIM)
            y = acc[:, cols]
            y = y * lax.rsqrt(jnp.mean(y * y, axis=-1, keepdims=True) + EPS) * w_ref[:, cols]
            if epilogue == "head_norm_rope":
                partner = jnp.dot(y.astype(BF16), swap, preferred_element_type=F32)
                y = y * cos + partner * sin
            o_ref[:, cols] = y.astype(o_ref.dtype)
    else:
        (o_ref,) = rest
        o_ref[...] = acc.astype(o_ref.dtype)


def matmul(xs, ws, n_rows, out_dtype, tn, residual=None, head_norm=None):
    n_out = ws[0].shape[1]
    tm = next(t for t in (1024, 512, ROW_BLOCK) if n_rows % t == 0)
    in_specs = ([pl.BlockSpec((tm, x.shape[1]), lambda i, j: (i, 0)) for x in xs]
                + [pl.BlockSpec((w.shape[0], tn), lambda i, j: (0, j)) for w in ws])
    args = list(xs) + list(ws)
    epilogue = "plain"
    if residual is not None:
        epilogue = "residual"
        h, mod3, gate_chunk, (n_lat, seq, n_batch) = residual
        row = functools.partial(_mod_row, tile=tm, n_lat=n_lat, seq=seq, n_batch=n_batch)
        nj = mod3.shape[2] // 6 // tn
        in_specs += [pl.BlockSpec((tm, tn), lambda i, j: (i, j)),
                     pl.BlockSpec((None, 1, tn), lambda i, j: (row(i), 0, gate_chunk * nj + j))]
        args += [h, mod3]
    elif head_norm is not None:
        epilogue = "head_norm"
        w_cols, rope = head_norm
        in_specs.append(pl.BlockSpec((1, tn), lambda i, j: (0, j)))
        args.append(w_cols)
        if rope is not None:
            epilogue = "head_norm_rope"
            cos, sin, seq, n_lat = rope
            per_seq, lat_tiles = seq // tm, n_lat // tm
            tab = pl.BlockSpec((tm, HEAD_DIM), lambda i, j: (jnp.where(i < lat_tiles, i % per_seq, per_seq), 0))
            in_specs += [tab, tab]
            args += [cos, sin]
    return pl.pallas_call(
        functools.partial(_matmul_kernel, n_in=len(xs), epilogue=epilogue),
        grid=(n_rows // tm, n_out // tn),
        in_specs=in_specs,
        out_specs=pl.BlockSpec((tm, tn), lambda i, j: (i, j)),
        out_shape=jax.ShapeDtypeStruct((n_rows, n_out), out_dtype),
        compiler_params=_cparams("parallel", "parallel"),
        name="matmul_" + epilogue,
    )(*args)


CONV_PAD = 8


def _conv_silu_kernel(x_ref, w_ref, b_ref, o_ref, pad_ref):
    seq, width = o_ref.shape
    zeros = jnp.zeros((CONV_PAD, width), F32)
    pad_ref[pl.ds(0, CONV_PAD), :] = zeros
    pad_ref[pl.ds(CONV_PAD + seq, CONV_PAD), :] = zeros
    pad_ref[pl.ds(CONV_PAD, seq), :] = x_ref[...].astype(F32)
    rows = 256
    for base in range(0, seq, rows):
        acc = jnp.broadcast_to(b_ref[...], (rows, width))
        for k in range(SSD_CONV):
            acc = acc + w_ref[pl.ds(k, 1), :] * pad_ref[pl.ds(base + CONV_PAD - SSD_CONV // 2 + k, rows), :]
        o_ref[pl.ds(base, rows), :] = (acc * jax.nn.sigmoid(acc)).astype(o_ref.dtype)


def conv_silu(p_zx, conv_w, conv_b, n_seq, seq, row_off):
    tc = 512
    col0 = SSD_INNER // tc
    off = row_off // seq
    return pl.pallas_call(
        _conv_silu_kernel,
        grid=(n_seq, SSD_XBC // tc),
        in_specs=[pl.BlockSpec((seq, tc), lambda b, j: (b + off, col0 + j)),
                  pl.BlockSpec((SSD_CONV, tc), lambda b, j: (0, j)),
                  pl.BlockSpec((1, tc), lambda b, j: (0, j))],
        out_specs=pl.BlockSpec((seq, tc), lambda b, j: (b, j)),
        out_shape=jax.ShapeDtypeStruct((n_seq * seq, SSD_XBC), BF16),
        scratch_shapes=[pltpu.VMEM((seq + 2 * CONV_PAD, tc), F32)],
        compiler_params=_cparams("parallel", "parallel"),
        name="conv_silu",
    )(p_zx, conv_w, conv_b.reshape(1, SSD_XBC))


def _ssd_dt_kernel(p_ref, bias_ref, a_ref, dt_ref, acs_ref, acst_ref):
    n = p_ref.shape[0]
    pre = p_ref[...] + bias_ref[...]
    dt = jnp.maximum(pre, 0.0) + jnp.log1p(jnp.exp(-jnp.abs(pre)))
    a = dt * a_ref[...]
    li = lax.broadcasted_iota(jnp.int32, (n, n), 0)
    si = lax.broadcasted_iota(jnp.int32, (n, n), 1)
    lower = (si <= li).astype(F32)
    upper = (si >= li).astype(F32)
    fwd = jnp.dot(lower, a, precision=HIGHEST, preferred_element_type=F32)
    bwd = jnp.dot(upper, a, precision=HIGHEST, preferred_element_type=F32)
    lane = lax.broadcasted_iota(jnp.int32, fwd.shape, 1)
    acs = jnp.where(lane < SSD_HEADS, fwd, bwd)
    dt_ref[...] = dt
    acs_ref[...] = acs
    acst_ref[...] = acs.T


def ssd_dt(p_dt, dt_bias, a_log):
    rows = p_dt.shape[0]
    pad = LANES - 2 * SSD_HEADS
    bias = jnp.pad(dt_bias.reshape(1, 2 * SSD_HEADS), ((0, 0), (0, pad)))
    a_neg = jnp.pad(-jnp.exp(a_log.astype(F32)).reshape(1, 2 * SSD_HEADS), ((0, 0), (0, pad)))
    c = SSD_CHUNK
    return pl.pallas_call(
        _ssd_dt_kernel,
        grid=(rows // c,),
        in_specs=[pl.BlockSpec((c, LANES), lambda i: (i, 0)),
                  pl.BlockSpec((1, LANES), lambda i: (0, 0)),
                  pl.BlockSpec((1, LANES), lambda i: (0, 0))],
        out_specs=[pl.BlockSpec((c, LANES), lambda i: (i, 0)),
                   pl.BlockSpec((c, LANES), lambda i: (i, 0)),
                   pl.BlockSpec((LANES, c), lambda i: (0, i))],
        out_shape=[jax.ShapeDtypeStruct((rows, LANES), F32),
                   jax.ShapeDtypeStruct((rows, LANES), F32),
                   jax.ShapeDtypeStruct((LANES, rows), F32)],
        compiler_params=_cparams("parallel"),
        name="ssd_dt",
    )(p_dt, bias, a_neg)


def _ssd_scan_kernel(*refs):
    fwd, bwd, outs = refs[:6], refs[6:12], refs[12:]

    @pl.when(pl.program_id(2) == 0)
    def _():
        for state_ref in outs[2:]:
            state_ref[...] = jnp.zeros_like(state_ref)

    _ssd_chunk(*fwd, outs[0], outs[2], reverse=False)
    _ssd_chunk(*bwd, outs[1], outs[3], reverse=True)


def _ssd_chunk(x_ref, b_ref, c_ref, dt_ref, acs_ref, acsr_ref, y_ref, state_ref, *, reverse):
    n = x_ref.shape[0]
    heads = dt_ref.shape[1]
    pair_w = 2 * SSD_HEAD_DIM
    x = x_ref[...].astype(F32)
    bm = b_ref[...]
    cm = c_ref[...]
    dt = dt_ref[...]
    acs = acs_ref[...]
    acs_row = acsr_ref[...]
    end = 0 if reverse else n - 1
    total = acs[end:end + 1, :]
    e_tot = jnp.exp(total)

    li = lax.broadcasted_iota(jnp.int32, (n, n), 0)
    si = lax.broadcasted_iota(jnp.int32, (n, n), 1)
    visible = (si >= li) if reverse else (si <= li)
    lo = lax.broadcasted_iota(jnp.int32, (1, pair_w), 1) < SSD_HEAD_DIM

    cb = lax.dot_general(cm, bm, (((1,), (1,)), ((), ())), preferred_element_type=F32)
    state = state_ref[...]
    y_off = jnp.dot(cm, state.astype(BF16), preferred_element_type=F32)

    xd_parts, decay_parts = [], []
    for i in range(heads // 2):
        j0, j1 = 2 * i, 2 * i + 1
        cols = slice(i * pair_w, (i + 1) * pair_w)
        xp = x[:, cols] * jnp.where(lo, dt[:, j0:j0 + 1], dt[:, j1:j1 + 1])
        ms, xs, acs_b = [], [], []
        for j, keep in ((j0, lo), (j1, jnp.logical_not(lo))):
            acs_b.append(jnp.broadcast_to(acs[:, j:j + 1], (n, n)))
            diff = acs_b[-1] - acs_row[j:j + 1, :]
            ms.append((cb * jnp.exp(jnp.where(visible, diff, MASK_VALUE))).astype(BF16))
            xs.append(jnp.where(keep, xp, 0.0).astype(BF16))
        y_diag = jnp.dot(jnp.concatenate(ms, axis=1), jnp.concatenate(xs, axis=0), preferred_element_type=F32)
        acs_pair = jnp.where(lo, acs_b[0], acs_b[1])
        tot_pair = jnp.where(lo, total[:, j0:j0 + 1], total[:, j1:j1 + 1])
        y_ref[:, cols] = (y_diag + y_off[:, cols] * jnp.exp(acs_pair)).astype(y_ref.dtype)
        xd_parts.append((xp * jnp.exp(tot_pair - acs_pair)).astype(BF16))
        decay_parts.append(jnp.where(lo, e_tot[:, j0:j0 + 1], e_tot[:, j1:j1 + 1]))
    xd = jnp.concatenate(xd_parts, axis=1)
    decay = jnp.concatenate(decay_parts, axis=1)
    bt = bm.astype(F32).T.astype(BF16)
    state_ref[...] = state * decay + jnp.dot(bt, xd, preferred_element_type=F32)


def ssd_scan(xbc, dt_col, acs_col, acs_row, n_batch, seq, ctx_len):
    c = SSD_CHUNK
    lat_chunks, ctx_chunks = seq // c, ctx_len // c
    ctx_base = n_batch * lat_chunks
    n_steps = lat_chunks + ctx_chunks
    gw = SSD_GROUP_W
    heads = SSD_HEADS // SSD_GROUPS
    b_col0 = SSD_INNER // SSD_STATE
    c_col0 = (SSD_INNER + SSD_GN) // SSD_STATE

    def specs(reverse):
        def chunk(b, k):
            kc = (ctx_chunks - 1 - k) if reverse else k
            kl = (n_steps - 1 - k) if reverse else (k - ctx_chunks)
            return jnp.where(k < ctx_chunks, ctx_base + b * ctx_chunks + kc, b * lat_chunks + kl)

        d = int(reverse)
        ins = [pl.BlockSpec((c, gw), lambda b, g, k: (chunk(b, k), g)),
               pl.BlockSpec((c, SSD_STATE), lambda b, g, k: (chunk(b, k), b_col0 + g)),
               pl.BlockSpec((c, SSD_STATE), lambda b, g, k: (chunk(b, k), c_col0 + g)),
               pl.BlockSpec((None, None, c, heads), lambda b, g, k: (d, g, chunk(b, k), 0)),
               pl.BlockSpec((None, None, c, heads), lambda b, g, k: (d, g, chunk(b, k), 0)),
               pl.BlockSpec((None, None, heads, c), lambda b, g, k: (d, g, 0, chunk(b, k)))]
        return ins, pl.BlockSpec((c, gw), lambda b, g, k: (chunk(b, k), g))

    (in_f, out_f), (in_b, out_b) = specs(False), specs(True)
    y_shape = jax.ShapeDtypeStruct((xbc.shape[0], SSD_INNER), BF16)
    args = (xbc, xbc, xbc, dt_col, acs_col, acs_row)
    return pl.pallas_call(
        _ssd_scan_kernel,
        grid=(n_batch, SSD_GROUPS, n_steps),
        in_specs=in_f + in_b,
        out_specs=[out_f, out_b],
        out_shape=[y_shape, y_shape],
        scratch_shapes=[pltpu.VMEM((SSD_STATE, gw), F32), pltpu.VMEM((SSD_STATE, gw), F32)],
        compiler_params=_cparams("parallel", "parallel", "arbitrary"),
        name="ssd_scan",
    )(*args, *args)


def _ssd_gate_kernel(yf_ref, yb_ref, x_ref, z_ref, d_ref, w_ref, o_ref):
    z = z_ref[...].astype(F32)
    y = yf_ref[...].astype(F32) + yb_ref[...].astype(F32) + d_ref[...] * x_ref[...].astype(F32)
    g = y * (z * jax.nn.sigmoid(z))
    o = g * lax.rsqrt(jnp.mean(g * g, axis=-1, keepdims=True) + EPS) * w_ref[...]
    o_ref[...] = o.astype(o_ref.dtype)


def ssd_gate(y_f, y_b, xbc, p_zx, d_skip, norm_w):
    rows = y_f.shape[0]
    t, gw = ROW_BLOCK, SSD_GROUP_W
    d_cols = jnp.repeat(d_skip.astype(F32), SSD_HEAD_DIM).reshape(1, SSD_INNER)
    spec = pl.BlockSpec((t, gw), lambda i, g: (i, g))
    vec = pl.BlockSpec((1, gw), lambda i, g: (0, g))
    return pl.pallas_call(
        _ssd_gate_kernel,
        grid=(rows // t, SSD_GROUPS),
        in_specs=[spec, spec, spec, spec, vec, vec],
        out_specs=spec,
        out_shape=jax.ShapeDtypeStruct((rows, SSD_INNER), BF16),
        compiler_params=_cparams("parallel", "parallel"),
        name="ssd_gate",
    )(y_f, y_b, xbc, p_zx, d_cols, norm_w.reshape(1, SSD_INNER))


ROPE_IDENTITY_ROWS = 1024


def _rope_tables(seq):
    t = jnp.arange(seq, dtype=jnp.int32)
    row = (t // GRID_W).astype(F32)
    col = (t % GRID_W).astype(F32)
    axis_dim = HEAD_DIM // 2
    inv = 1.0 / (ROPE_THETA ** (jnp.arange(0, axis_dim, 2, dtype=F32) / axis_dim))
    ang = jnp.concatenate([row[:, None] * inv[None], col[:, None] * inv[None]], axis=-1)
    cos = jnp.repeat(jnp.cos(ang), 2, axis=-1)
    sin = jnp.stack([-jnp.sin(ang), jnp.sin(ang)], axis=-1).reshape(seq, HEAD_DIM)
    identity = jnp.ones((ROPE_IDENTITY_ROWS, HEAD_DIM), F32)
    return jnp.concatenate([cos, identity], axis=0), jnp.concatenate([sin, 0.0 * identity], axis=0)


LOG2E = math.log2(math.e)
KEY_CHUNK = 256
DIFF_Q_TILE = 512


def _scores(q, k, bias):
    s = lax.dot_general(q, k, (((1,), (1,)), ((), ())), preferred_element_type=F32)
    return s if bias is None else s + bias


def _attend_two_pass(q, pieces):
    scores = [_scores(q, k, bias) for k, _, bias in pieces]
    m = functools.reduce(jnp.maximum, [jnp.max(s, axis=-1, keepdims=True) for s in scores])
    denom = acc = None
    for s, (_, v, _) in zip(scores, pieces):
        p = jnp.exp2(s - m)
        p_sum = jnp.sum(p, axis=-1, keepdims=True)
        pv = jnp.dot(p.astype(v.dtype), v, preferred_element_type=F32)
        denom = p_sum if denom is None else denom + p_sum
        acc = pv if acc is None else acc + pv
    return acc / denom


def _attend(q, pieces):
    chunks = []
    for k, v, bias in pieces:
        for c0 in range(0, k.shape[0], KEY_CHUNK):
            c1 = min(c0 + KEY_CHUNK, k.shape[0])
            chunks.append((k[c0:c1], v[c0:c1], None if bias is None else bias[:, c0:c1]))
    m = denom = acc = None
    for k, v, bias in chunks:
        s = lax.dot_general(q, k, (((1,), (1,)), ((), ())), preferred_element_type=F32)
        if bias is not None:
            s = s + bias
        m_chunk = jnp.max(s, axis=-1, keepdims=True)
        m_new = m_chunk if m is None else jnp.maximum(m, m_chunk)
        p = jnp.exp2(s - m_new)
        p_sum = jnp.sum(p, axis=-1, keepdims=True)
        pv = jnp.dot(p.astype(v.dtype), v, preferred_element_type=F32)
        if m is None:
            denom, acc = p_sum, pv
        else:
            alpha = jnp.exp2(m - m_new)
            denom = alpha * denom + p_sum
            acc = alpha * acc + pv
        m = m_new
    return acc / denom


NA_HEADS_PER_STEP = 4


def _na_kernel(q_ref, k0_ref, k1_ref, k2_ref, kc_ref, v0_ref, v1_ref, v2_ref, vc_ref, bias_ref, o_ref):
    t = q_ref.shape[0]
    for hd in range(NA_HEADS_PER_STEP):
        cols = slice(hd * HEAD_DIM, (hd + 1) * HEAD_DIM)
        pieces = [(k_ref[:, cols], v_ref[:, cols], bias_ref[hd, :, i * t:(i + 1) * t])
                  for i, (k_ref, v_ref) in enumerate(((k0_ref, v0_ref), (k1_ref, v1_ref), (k2_ref, v2_ref)))]
        pieces.append((kc_ref[:, cols], vc_ref[:, cols], None))
        o_ref[:, cols] = _attend_two_pass(q_ref[:, cols], pieces).astype(o_ref.dtype)


def _na_bias_table(rpb, rows):
    n_blocks = rows // NA_QROWS
    w0_of = lambda j: min(max(NA_QROWS * j - NA_WIN_ROWS // 2, 0), rows - NA_KROWS)
    geoms, pattern_of = [], []
    for j in range(n_blocks):
        r0s = tuple(min(max(NA_QROWS * j + a - NA_WIN_ROWS // 2, 0), rows - NA_WIN_ROWS) - w0_of(j)
                    for a in range(NA_QROWS))
        key = (NA_QROWS * j - w0_of(j), r0s)
        if key not in geoms:
            geoms.append(key)
        pattern_of.append(geoms.index(key))
    qc = jnp.arange(GRID_W)[:, None]
    kc = jnp.arange(GRID_W)[None, :]
    cs = jnp.clip(qc - NA_WIN_COLS // 2, 0, GRID_W - NA_WIN_COLS)
    col_ok = (kc >= cs) & (kc < cs + NA_WIN_COLS)
    dc = jnp.clip(kc - qc + NA_WIN_COLS - 1, 0, 2 * NA_WIN_COLS - 2)
    pick = (dc[None] == jnp.arange(2 * NA_WIN_COLS - 1)[:, None, None]).astype(F32)
    tiles = jnp.einsum('hrd,dqk->hrqk', rpb.astype(F32), pick, precision=HIGHEST)
    tiles = jnp.where(col_ok, tiles * LOG2E, MASK_VALUE)
    masked = jnp.full((rpb.shape[0], GRID_W, GRID_W), MASK_VALUE, F32)
    tables = []
    for q_row0, r0s in geoms:
        tile_rows = []
        for qi in range(NA_QROWS):
            row = [tiles[:, ki - (q_row0 + qi) + NA_WIN_ROWS - 1] if r0s[qi] <= ki < r0s[qi] + NA_WIN_ROWS
                   else masked for ki in range(NA_KROWS)]
            tile_rows.append(jnp.concatenate(row, axis=2))
        tables.append(jnp.concatenate(tile_rows, axis=1))
    tables.append(jnp.full_like(tables[0], MASK_VALUE))
    pattern_of.append(len(geoms))
    w0_blocks = [w0_of(j) // NA_QROWS for j in range(n_blocks)] + [0]
    return jnp.stack(tables, axis=1), pattern_of, w0_blocks


def _lookup(values, j):
    out = jnp.int32(values[-1])
    for idx in range(len(values) - 2, -1, -1):
        out = jnp.where(j == idx, jnp.int32(values[idx]), out)
    return out


def na_attention(qn, kn, p_na, v_col_off, rpb, n_batch, seq, ctx_len):
    t = NA_QROWS * GRID_W
    assert t == ROW_BLOCK == ctx_len
    rows = seq // GRID_W
    n_blocks = seq // t
    bias, pattern_of, w0_blocks = _na_bias_table(rpb, rows)
    hw = NA_HEADS_PER_STEP * HEAD_DIM
    v_col0 = v_col_off // hw
    ctx_base = n_batch * n_blocks

    def q_blk(j, b):
        return jnp.where(j < n_blocks, b * n_blocks + j, ctx_base + b)

    def kv_blk(piece):
        return lambda h, j, b: (b * n_blocks + _lookup(w0_blocks, j) + piece, h)

    def kv_blk_v(piece):
        return lambda h, j, b: (b * n_blocks + _lookup(w0_blocks, j) + piece, v_col0 + h)

    blk = lambda fn: pl.BlockSpec((t, hw), fn)
    in_specs = ([blk(lambda h, j, b: (q_blk(j, b), h))]
                + [blk(kv_blk(i)) for i in range(3)] + [blk(lambda h, j, b: (ctx_base + b, h))]
                + [blk(kv_blk_v(i)) for i in range(3)] + [blk(lambda h, j, b: (ctx_base + b, v_col0 + h))]
                + [pl.BlockSpec((NA_HEADS_PER_STEP, None, t, 3 * t),
                                lambda h, j, b: (h, _lookup(pattern_of, j), 0, 0))])
    return pl.pallas_call(
        _na_kernel,
        grid=(NA_HEADS // NA_HEADS_PER_STEP, n_blocks + 1, n_batch),
        in_specs=in_specs,
        out_specs=blk(lambda h, j, b: (q_blk(j, b), h)),
        out_shape=jax.ShapeDtypeStruct((qn.shape[0], NA_WIDTH), BF16),
        compiler_params=_cparams("parallel", "parallel", "parallel"),
        name="na_attention",
    )(qn, kn, kn, kn, kn, p_na, p_na, p_na, p_na, bias)


def _gqa_kernel(q_ref, kl_ref, kc_ref, vl_ref, vc_ref, o_ref):
    t = q_ref.shape[0]
    q = jnp.concatenate([q_ref[:, g * HEAD_DIM:(g + 1) * HEAD_DIM] for g in range(GQA_GROUP)], axis=0)
    o = _attend(q, [(kc_ref[...], vc_ref[...], None), (kl_ref[...], vl_ref[...], None)])
    for g in range(GQA_GROUP):
        o_ref[:, g * HEAD_DIM:(g + 1) * HEAD_DIM] = o[g * t:(g + 1) * t].astype(o_ref.dtype)


def gqa_attention(qn, kn, p_kv, v_col_off, n_batch, seq, ctx_len):
    t = ROW_BLOCK
    q_tiles = seq // t
    ctx_base = n_batch * seq // ctx_len
    v0 = v_col_off // HEAD_DIM
    gw = GQA_GROUP * HEAD_DIM
    return pl.pallas_call(
        _gqa_kernel,
        grid=(n_batch, GQA_KV_HEADS, q_tiles),
        in_specs=[pl.BlockSpec((t, gw), lambda b, h, i: (b * q_tiles + i, h)),
                  pl.BlockSpec((seq, HEAD_DIM), lambda b, h, i: (b, h)),
                  pl.BlockSpec((ctx_len, HEAD_DIM), lambda b, h, i: (ctx_base + b, h)),
                  pl.BlockSpec((seq, HEAD_DIM), lambda b, h, i: (b, v0 + h)),
                  pl.BlockSpec((ctx_len, HEAD_DIM), lambda b, h, i: (ctx_base + b, v0 + h))],
        out_specs=pl.BlockSpec((t, gw), lambda b, h, i: (b * q_tiles + i, h)),
        out_shape=jax.ShapeDtypeStruct((n_batch * seq, GQA_Q), BF16),
        compiler_params=_cparams("parallel", "parallel", "parallel"),
        name="gqa_attention",
    )(qn, kn, kn, p_kv, p_kv)


def _diff_kernel(lam_ref, q_ref, kl_ref, kc_ref, vl_ref, vc_ref, w_ref, o_ref, *, out_scale):
    outs = []
    for comp in range(2):
        cols = slice(comp * HEAD_DIM, (comp + 1) * HEAD_DIM)
        outs.append(_attend(q_ref[:, cols], [(kc_ref[:, cols], vc_ref[...], None),
                                              (kl_ref[:, cols], vl_ref[...], None)]))
    o = outs[0] - lam_ref[0] * outs[1]
    o = o * lax.rsqrt(jnp.mean(o * o, axis=-1, keepdims=True) + EPS) * w_ref[...]
    o_ref[...] = (o * out_scale).astype(o_ref.dtype)


def diff_attention(qn, q_col_off, kn, p_kv, v_col_off, lam, subln_w, out_scale, n_batch, seq, ctx_len):
    t = DIFF_Q_TILE
    q_tiles = seq // t
    ctx_base = n_batch * seq // ctx_len
    v0 = v_col_off // DIFF_V_DIM
    w2 = 2 * HEAD_DIM
    q0 = q_col_off // w2
    return pl.pallas_call(
        functools.partial(_diff_kernel, out_scale=out_scale),
        grid=(n_batch, DIFF_HEADS, q_tiles),
        in_specs=[pl.BlockSpec(memory_space=pltpu.SMEM),
                  pl.BlockSpec((t, w2), lambda b, h, i: (b * q_tiles + i, q0 + h)),
                  pl.BlockSpec((seq, w2), lambda b, h, i: (b, h)),
                  pl.BlockSpec((ctx_len, w2), lambda b, h, i: (ctx_base + b, h)),
                  pl.BlockSpec((seq, DIFF_V_DIM), lambda b, h, i: (b, v0 + h)),
                  pl.BlockSpec((ctx_len, DIFF_V_DIM), lambda b, h, i: (ctx_base + b, v0 + h)),
                  pl.BlockSpec((1, DIFF_V_DIM), lambda b, h, i: (0, 0))],
        out_specs=pl.BlockSpec((t, DIFF_V_DIM), lambda b, h, i: (b * q_tiles + i, h)),
        out_shape=jax.ShapeDtypeStruct((n_batch * seq, DIFF_V), BF16),
        compiler_params=_cparams("parallel", "parallel", "parallel"),
        name="diff_attention",
    )(lam.reshape(1), qn, kn, kn, p_kv, p_kv, subln_w.reshape(1, DIFF_V_DIM))


def _moe_plan_kernel(route_ref, pos_ref, pad_end_ref, counts_ref, start_ref):
    phase, i = pl.program_id(0), pl.program_id(1)
    t = route_ref.shape[0]
    route = route_ref[...]
    lane = lax.broadcasted_iota(jnp.int32, route.shape, 1)
    chosen = [lane == route[:, ROUTE_EXPERT_LANE + k:ROUTE_EXPERT_LANE + k + 1].astype(jnp.int32)
              for k in range(MOE_TOPK)]
    both = functools.reduce(jnp.add, [c.astype(F32) for c in chosen])
    tile_counts = jnp.sum(both, axis=0, keepdims=True)

    @pl.when((phase == 0) & (i == 0))
    def _():
        counts_ref[...] = jnp.zeros_like(counts_ref)

    @pl.when(phase == 0)
    def _():
        counts_ref[...] += tile_counts

    @pl.when((phase == 1) & (i == 0))
    def _():
        padded = jnp.ceil(counts_ref[...] * (1.0 / MOE_BLOCK)) * MOE_BLOCK
        rows = pad_end_ref.shape[0]
        ki = lax.broadcasted_iota(jnp.int32, (LANES, LANES), 0)
        ni = lax.broadcasted_iota(jnp.int32, (LANES, LANES), 1)
        pad_end = jnp.dot(jnp.broadcast_to(padded, (rows, LANES)), (ki <= ni).astype(F32),
                          precision=HIGHEST, preferred_element_type=F32)
        pad_end_ref[...] = pad_end
        start_ref[...] = pad_end[0:1] - padded

    @pl.when(phase == 1)
    def _():
        li = lax.broadcasted_iota(jnp.int32, (t, t), 0)
        si = lax.broadcasted_iota(jnp.int32, (t, t), 1)
        earlier = jnp.dot((si < li).astype(BF16), both.astype(BF16), preferred_element_type=F32)
        row_of = earlier + start_ref[...]
        out = jnp.zeros(route.shape, F32)
        for k in range(MOE_TOPK):
            p = jnp.sum(jnp.where(chosen[k], row_of, 0.0), axis=-1, keepdims=True)
            out = jnp.where(lane == k, p, out)
        pos_ref[...] = out.astype(jnp.int32)
        start_ref[...] += tile_counts


def moe_plan(route):
    n = route.shape[0]
    t = ROW_BLOCK
    return pl.pallas_call(
        _moe_plan_kernel,
        grid=(2, n // t),
        in_specs=[pl.BlockSpec((t, LANES), lambda p, i: (i, 0))],
        out_specs=[pl.BlockSpec((t, LANES), lambda p, i: (i * p, 0)),
                   pl.BlockSpec((8, LANES), lambda p, i: (0, 0))],
        out_shape=[jax.ShapeDtypeStruct((n, LANES), jnp.int32),
                   jax.ShapeDtypeStruct((8, LANES), F32)],
        scratch_shapes=[pltpu.VMEM((1, LANES), F32), pltpu.VMEM((1, LANES), F32)],
        compiler_params=_cparams("arbitrary", "arbitrary"),
        name="moe_plan",
    )(route)


MOE_DMA_CHUNKS = 4


def _moe_kernel(be_ref, next_ref, nused_ref, x_ref, w1_hbm, w3_hbm, w2_hbm, o_ref,
                s1, s3, s2, w1b, w3b, w2b, sems, *, layer):
    i = pl.program_id(0)

    def weight_copies(e):
        copies = []
        for n, (w_hbm, stage) in enumerate(((w1_hbm, s1), (w3_hbm, s3), (w2_hbm, s2))):
            rows = stage.shape[0] // MOE_DMA_CHUNKS
            for c in range(MOE_DMA_CHUNKS):
                slab = pl.ds(c * rows, rows)
                copies.append(pltpu.make_async_copy(w_hbm.at[layer, e, slab], stage.at[slab], sems.at[n, c]))
        return copies

    @pl.when(i == 0)
    def _():
        for c in weight_copies(be_ref[0]):
            c.start()

    prev = be_ref[jnp.maximum(i - 1, 0)]

    @pl.when((i == 0) | (be_ref[i] != prev))
    def _():
        for c in weight_copies(be_ref[i]):
            c.wait()
        w1b[...] = s1[...].astype(BF16)
        w3b[...] = s3[...].astype(BF16)
        w2b[...] = s2[...].astype(BF16)

        @pl.when(next_ref[i] >= 0)
        def _():
            for c in weight_copies(next_ref[i]):
                c.start()

    @pl.when(i < nused_ref[0])
    def _():
        x = x_ref[...].astype(BF16)
        a = jnp.dot(x, w1b[...], preferred_element_type=F32)
        b = jnp.dot(x, w3b[...], preferred_element_type=F32)
        hid = (a * jax.nn.sigmoid(a)) * b
        o_ref[...] = jnp.dot(hid.astype(BF16), w2b[...], preferred_element_type=F32)

    @pl.when(i >= nused_ref[0])
    def _():
        o_ref[...] = jnp.zeros_like(o_ref)


def moe_experts(x_sorted, block_e, next_e, n_used, w1, w3, w2, layer):
    n_blocks = block_e.shape[0]
    d, hid = w1.shape[2], w1.shape[3]
    hbm = pl.BlockSpec(memory_space=pl.ANY)
    grid_spec = pltpu.PrefetchScalarGridSpec(
        num_scalar_prefetch=3,
        grid=(n_blocks,),
        in_specs=[pl.BlockSpec((MOE_BLOCK, d), lambda i, be, nx, nu: (i, 0)), hbm, hbm, hbm],
        out_specs=pl.BlockSpec((MOE_BLOCK, d), lambda i, be, nx, nu: (i, 0)),
        scratch_shapes=[pltpu.VMEM((d, hid), F32), pltpu.VMEM((d, hid), F32), pltpu.VMEM((hid, d), F32),
                        pltpu.VMEM((d, hid), BF16), pltpu.VMEM((d, hid), BF16), pltpu.VMEM((hid, d), BF16),
                        pltpu.SemaphoreType.DMA((3, MOE_DMA_CHUNKS))],
    )
    return pl.pallas_call(
        functools.partial(_moe_kernel, layer=layer),
        grid_spec=grid_spec,
        out_shape=jax.ShapeDtypeStruct((n_blocks * MOE_BLOCK, d), F32),
        compiler_params=_cparams("arbitrary"),
        name="moe_experts",
    )(block_e, next_e, n_used, x_sorted, w1, w3, w2)


def _combine_kernel(h_ref, g_ref, route_ref, *refs):
    y_refs, o_ref = refs[:-1], refs[-1]
    acc = None
    for k, y_ref in enumerate(y_refs):
        part = y_ref[...] * route_ref[:, ROUTE_GATE_LANE + k:ROUTE_GATE_LANE + k + 1]
        acc = part if acc is None else acc + part
    o_ref[...] = h_ref[...] + g_ref[...] * acc


def moe_combine(h, mod3, gate_chunk, route, ys, geom):
    n_lat, seq, n_batch = geom
    n_rows, d = ys[0].shape
    t = ROW_BLOCK
    row = functools.partial(_mod_row, tile=t, n_lat=n_lat, seq=seq, n_batch=n_batch)
    spec = pl.BlockSpec((t, d), lambda i: (i, 0))
    return pl.pallas_call(
        _combine_kernel,
        grid=(n_rows // t,),
        in_specs=[spec, pl.BlockSpec((None, 1, d), lambda i: (row(i), 0, gate_chunk)),
                  pl.BlockSpec((t, LANES), lambda i: (i, 0))] + [spec] * len(ys),
        out_specs=spec,
        out_shape=jax.ShapeDtypeStruct((n_rows, d), F32),
        compiler_params=_cparams("parallel"),
        name="moe_combine",
    )(h, mod3, route, *ys)


def moe_layer(h, f, route, mod3, gate_chunk, w1, w3, w2, layer, geom):
    n = f.shape[0]
    n_assign = n * MOE_TOPK
    n_blocks = -(-n_assign // MOE_BLOCK) + MOE_EXPERTS
    pos, pad_end = moe_plan(route)
    pos = pos[:, :MOE_TOPK]
    pad_end = pad_end[0, :MOE_EXPERTS].astype(jnp.int32)
    n_used = pad_end[-1] // MOE_BLOCK
    blk = jnp.minimum(jnp.arange(n_blocks, dtype=jnp.int32), n_used - 1) * MOE_BLOCK
    block_e = jnp.minimum(jnp.sum(pad_end[None, :] <= blk[:, None], axis=1, dtype=jnp.int32), MOE_EXPERTS - 1)
    seg_end = pad_end[block_e] // MOE_BLOCK
    next_e = jnp.where(seg_end < n_used, block_e[jnp.minimum(seg_end, n_blocks - 1)], -1)
    tok = jnp.broadcast_to(jnp.arange(n, dtype=jnp.int32)[:, None], (n, MOE_TOPK))
    slot_tok = (jnp.arange(n_blocks * MOE_BLOCK, dtype=jnp.int32) % n).at[pos.reshape(-1)].set(
        tok.reshape(-1), unique_indices=True, mode='promise_in_bounds')
    x_sorted = f.at[slot_tok].get(mode='promise_in_bounds')
    yb = moe_experts(x_sorted, block_e, next_e, n_used.reshape(1), w1, w3, w2, layer)
    ys = [yb.at[pos[:, k]].get(mode='promise_in_bounds') for k in range(MOE_TOPK)]
    return moe_combine(h, mod3, gate_chunk, route, ys, geom)


def _even_layer_mix(a, in_w, conv_w, conv_b, dt_bias, a_log, d_skip, ssd_norm_w, q_norm, k_norm, rpb,
                    n_batch, seq, ctx_len):
    n_rows = a.shape[0]
    n_lat = n_batch * seq
    zx_end = SSD_INNER + SSD_XBC
    dt_end = zx_end + 2 * SSD_HEADS
    w_zx = in_w[:, :zx_end].astype(BF16)
    w_dt = jnp.pad(in_w[:, zx_end:dt_end], ((0, 0), (0, LANES - 2 * SSD_HEADS))).astype(BF16)
    na_cols = lambda i: in_w[:, dt_end + i * NA_WIDTH:dt_end + (i + 1) * NA_WIDTH].astype(BF16)
    tile_w = lambda w, s: (jnp.tile(w.astype(F32), NA_HEADS) * s).reshape(1, NA_WIDTH)
    p_zx = matmul([a], [w_zx], n_rows, BF16, tn=512)
    p_dt = matmul([a], [w_dt], n_rows, F32, tn=LANES)
    qn = matmul([a], [na_cols(0)], n_rows, BF16, tn=512,
                head_norm=(tile_w(q_norm, HEAD_DIM ** -0.5 * LOG2E), None))
    kn = matmul([a], [na_cols(1)], n_rows, BF16, tn=512, head_norm=(tile_w(k_norm, 1.0), None))
    p_v = matmul([a], [na_cols(2)], n_rows, BF16, tn=512)

    xbc = jnp.concatenate([conv_silu(p_zx, conv_w, conv_b, n_batch, seq, 0),
                           conv_silu(p_zx, conv_w, conv_b, n_batch, ctx_len, n_lat)], axis=0)
    dt, acs, acs_t = ssd_dt(p_dt, dt_bias, a_log)
    hg = SSD_HEADS // SSD_GROUPS
    col = lambda v: v[:, :2 * SSD_HEADS].reshape(n_rows, 2, SSD_GROUPS, hg).transpose(1, 2, 0, 3)
    dt_col, acs_col = col(dt), col(acs)
    acs_row = acs_t[:2 * SSD_HEADS].reshape(2, SSD_GROUPS, hg, n_rows)
    y_f, y_b = ssd_scan(xbc, dt_col, acs_col, acs_row, n_batch, seq, ctx_len)
    s_out = ssd_gate(y_f, y_b, xbc, p_zx, d_skip, ssd_norm_w)

    n_out = na_attention(qn, kn, p_v, 0, rpb, n_batch, seq, ctx_len)
    return s_out, n_out


def _odd_layer_mix(a, in_w, gq_norm, gk_norm, dq_norm, dk_norm, lam_vecs, subln_w, lambda_init,
                   n_batch, seq, ctx_len):
    n_rows = a.shape[0]
    n_lat = n_batch * seq
    cos, sin = _rope_tables(seq)
    rope = (cos, sin, seq, n_lat)
    scale = HEAD_DIM ** -0.5 * LOG2E
    rep = lambda w, n: jnp.tile(w.astype(F32), n).reshape(1, n * HEAD_DIM)
    cols = lambda lo, hi: in_w[:, lo:hi].astype(BF16)
    gk_off, gv_off, dk_off, dv_off = ODD_Q, ODD_Q + GQA_KV, ODD_Q + 2 * GQA_KV, ODD_Q + 2 * GQA_KV + DIFF_QK
    q_w = jnp.concatenate([rep(gq_norm, GQA_HEADS), rep(dq_norm, 2 * DIFF_HEADS)], axis=1) * scale
    qn = matmul([a], [cols(0, ODD_Q)], n_lat, BF16, tn=512, head_norm=(q_w, rope))
    gkn = matmul([a], [cols(gk_off, gv_off)], n_rows, BF16, tn=512,
                 head_norm=(rep(gk_norm, GQA_KV_HEADS), rope))
    dkn = matmul([a], [cols(dk_off, dv_off)], n_rows, BF16, tn=512,
                 head_norm=(rep(dk_norm, 2 * DIFF_HEADS), rope))
    gv = matmul([a], [cols(gv_off, dk_off)], n_rows, BF16, tn=512)
    dv = matmul([a], [cols(dv_off, in_w.shape[1])], n_rows, BF16, tn=512)
    lv = lam_vecs.astype(F32)
    lam = jnp.exp(jnp.dot(lv[0], lv[1])) - jnp.exp(jnp.dot(lv[2], lv[3])) + lambda_init
    og = gqa_attention(qn, gkn, gv, 0, n_batch, seq, ctx_len)
    od = diff_attention(qn, GQA_Q, dkn, dv, 0, lam, subln_w, 1.0 - lambda_init, n_batch, seq, ctx_len)
    return og, od


def kernel(x, c, ctx, c_ctx, ada_w, ada_b, norm_w, ev_in_w, ev_conv_w, ev_conv_b, ev_dt_bias, ev_a_log, ev_d_skip, ev_ssd_norm_w, ev_na_q_norm, ev_na_k_norm, ev_na_rpb, ev_out_w, od_in_w, od_gqa_q_norm, od_gqa_k_norm, od_diff_q_norm, od_diff_k_norm, od_lambda, od_diff_subln, od_out_w, moe_group_w, moe_expert_w, moe_w1, moe_w3, moe_w2):
    n_batch, seq, d = x.shape
    ctx_len = ctx.shape[1]
    depth = ada_w.shape[0]
    assert depth == 2 and d == D_MODEL
    n_lat, n_ctx = n_batch * seq, n_batch * ctx_len
    n_all = n_lat + n_ctx
    geom = (n_lat, seq, n_batch)

    cc = jnp.concatenate([c, c_ctx[None, :], jnp.zeros((8 - n_batch - 1, d), F32)], axis=0)
    mod = ada_modulation(cc, ada_w, ada_b)
    router_w = lambda l: jnp.pad(jnp.concatenate([moe_group_w[l], moe_expert_w[l]], axis=1),
                                 ((0, 0), (0, LANES - MOE_GROUPS - MOE_EXPERTS)))

    h = jnp.concatenate([x.reshape(n_lat, d), ctx.reshape(n_ctx, d)], axis=0)

    mod3 = mod[0].reshape(8, 1, 6 * d)
    a = norm_modulate(h, norm_w[0, 0], mod3, 1, 0, n_all, geom)
    s_out, n_out = _even_layer_mix(a, ev_in_w[0], ev_conv_w[0], ev_conv_b[0], ev_dt_bias[0], ev_a_log[0],
                                   ev_d_skip[0], ev_ssd_norm_w[0], ev_na_q_norm[0], ev_na_k_norm[0],
                                   ev_na_rpb[0], n_batch, seq, ctx_len)
    out_w = ev_out_w[0].astype(BF16)
    h = matmul([s_out, n_out], [out_w[:SSD_INNER], out_w[SSD_INNER:]], n_all, F32, tn=512,
               residual=(h, mod3, 2, geom))
    f, route = norm_modulate(h, norm_w[0, 1], mod3, 4, 3, n_all, geom, router_w=router_w(0))
    h = moe_layer(h, f, route, mod3, 5, moe_w1, moe_w3, moe_w2, 0, geom)

    mod3 = mod[1].reshape(8, 1, 6 * d)
    a = norm_modulate(h, norm_w[1, 0], mod3, 1, 0, n_all, geom)
    lambda_init = 0.8 - 0.6 * math.exp(-0.3 * 1)
    og, od = _odd_layer_mix(a, od_in_w[0], od_gqa_q_norm[0], od_gqa_k_norm[0], od_diff_q_norm[0],
                            od_diff_k_norm[0], od_lambda[0], od_diff_subln[0], lambda_init,
                            n_batch, seq, ctx_len)
    out_w = od_out_w[0].astype(BF16)
    h = matmul([og, od], [out_w[:GQA_Q], out_w[GQA_Q:]], n_lat, F32, tn=512,
               residual=(h, mod3, 2, geom))
    f, route = norm_modulate(h, norm_w[1, 1], mod3, 4, 3, n_lat, geom, router_w=router_w(1))
    h = moe_layer(h, f, route, mod3, 5, moe_w1, moe_w3, moe_w2, 1, geom)
    return h.reshape(n_batch, seq, d)
```

```python
import functools
import math

import jax
import jax.numpy as jnp
from jax import lax
from jax.experimental import pallas as pl
from jax.experimental.pallas import tpu as pltpu

F32 = jnp.float32
BF16 = jnp.bfloat16
HIGHEST = lax.Precision.HIGHEST

D_MODEL = 2048
GRID_W = 64
EPS = 1e-6
ROPE_THETA = 10000.0
HEAD_DIM = 128

SSD_HEADS = 32
SSD_HEAD_DIM = 64
SSD_INNER = SSD_HEADS * SSD_HEAD_DIM
SSD_GROUPS = 4
SSD_STATE = 128
SSD_GN = SSD_GROUPS * SSD_STATE
SSD_XBC = SSD_INNER + 2 * SSD_GN
SSD_CONV = 5
SSD_CHUNK = 128
SSD_GROUP_W = SSD_INNER // SSD_GROUPS

NA_HEADS = 16
NA_WIDTH = NA_HEADS * HEAD_DIM
NA_WIN_ROWS = 8
NA_WIN_COLS = 16
NA_QROWS = 4
NA_KROWS = 12

GQA_HEADS = 16
GQA_KV_HEADS = 4
GQA_GROUP = GQA_HEADS // GQA_KV_HEADS
GQA_Q = GQA_HEADS * HEAD_DIM
GQA_KV = GQA_KV_HEADS * HEAD_DIM
DIFF_HEADS = 8
DIFF_QK = DIFF_HEADS * 2 * HEAD_DIM
DIFF_V_DIM = 2 * HEAD_DIM
DIFF_V = DIFF_HEADS * DIFF_V_DIM
ODD_Q = GQA_Q + DIFF_QK

MOE_GROUPS = 8
MOE_EXPERTS_PER_GROUP = 8
MOE_EXPERTS = MOE_GROUPS * MOE_EXPERTS_PER_GROUP
MOE_HIDDEN = 512
MOE_TOPK = 2
MOE_BLOCK = 128

ROW_BLOCK = 256
LANES = 128
MASK_VALUE = -1e30
VMEM_LIMIT = 52 * 1024 * 1024


def _cparams(*sem):
    return pltpu.CompilerParams(dimension_semantics=sem, vmem_limit_bytes=VMEM_LIMIT)


def _mod_row(i, tile, n_lat, seq, n_batch):
    return jnp.where(i < n_lat // tile, i // (seq // tile), n_batch)


def _ada_kernel(c_ref, w_ref, b_ref, o_ref):
    c = c_ref[...]
    s = c * jax.nn.sigmoid(c)
    o_ref[...] = jnp.dot(s, w_ref[...], precision=HIGHEST, preferred_element_type=F32) + b_ref[...]


def ada_modulation(cc, ada_w, ada_b):
    n_layers, d, n_out = ada_w.shape
    rows = cc.shape[0]
    tn = 1024
    return pl.pallas_call(
        _ada_kernel,
        grid=(n_layers, n_out // tn),
        in_specs=[pl.BlockSpec((rows, d), lambda l, j: (0, 0)),
                  pl.BlockSpec((None, d, tn), lambda l, j: (l, 0, j)),
                  pl.BlockSpec((None, 1, tn), lambda l, j: (l, 0, j))],
        out_specs=pl.BlockSpec((None, rows, tn), lambda l, j: (l, 0, j)),
        out_shape=jax.ShapeDtypeStruct((n_layers, rows, n_out), F32),
        compiler_params=_cparams("parallel", "parallel"),
        name="ada_modulation",
    )(cc, ada_w, ada_b.reshape(n_layers, 1, n_out))


def _normmod_kernel(h_ref, w_ref, sc_ref, sh_ref, *rest, with_router):
    x = h_ref[...]
    y = x * lax.rsqrt(jnp.mean(x * x, axis=-1, keepdims=True) + EPS) * w_ref[...]
    a = y * (1.0 + sc_ref[...]) + sh_ref[...]
    if with_router:
        rw_ref, o_ref, route_ref = rest
        logits = jnp.dot(a, rw_ref[...], precision=HIGHEST, preferred_element_type=F32)
        route_ref[...] = _route(logits)
    else:
        (o_ref,) = rest
    o_ref[...] = a.astype(o_ref.dtype)


ROUTE_EXPERT_LANE = 0
ROUTE_GATE_LANE = MOE_TOPK


def _route(logits):
    assert MOE_TOPK == 2
    neg = jnp.float32(-3.0e38)
    lane = lax.broadcasted_iota(jnp.int32, logits.shape, 1)
    is_group = lane < MOE_GROUPS
    gl = jnp.where(is_group, logits, neg)
    g_max = jnp.max(gl, axis=-1, keepdims=True)
    g_idx = jnp.min(jnp.where(gl == g_max, lane, LANES), axis=-1, keepdims=True)
    g_gate = 1.0 / jnp.sum(jnp.where(is_group, jnp.exp(logits - g_max), 0.0), axis=-1, keepdims=True)
    first = MOE_GROUPS + g_idx * MOE_EXPERTS_PER_GROUP
    in_group = (lane >= first) & (lane < first + MOE_EXPERTS_PER_GROUP)
    el = jnp.where(in_group, logits, neg)
    v1 = jnp.max(el, axis=-1, keepdims=True)
    i1 = jnp.min(jnp.where(in_group & (el == v1), lane, LANES), axis=-1, keepdims=True)
    rest = in_group & (lane != i1)
    el2 = jnp.where(rest, logits, neg)
    v2 = jnp.max(el2, axis=-1, keepdims=True)
    i2 = jnp.min(jnp.where(rest & (el2 == v2), lane, LANES), axis=-1, keepdims=True)
    e21 = jnp.exp(v2 - v1)
    p1 = 1.0 / (1.0 + e21)
    p2 = e21 / (1.0 + e21)
    vals = [(i1 - MOE_GROUPS).astype(F32), (i2 - MOE_GROUPS).astype(F32), p1 * g_gate, p2 * g_gate]
    out = jnp.zeros(logits.shape, F32)
    for k, v in enumerate(vals):
        out = jnp.where(lane == k, v, out)
    return out


def norm_modulate(h, w, mod3, sc_chunk, sh_chunk, n_rows, geom, router_w=None):
    n_lat, seq, n_batch = geom
    d = h.shape[1]
    t = ROW_BLOCK
    row = functools.partial(_mod_row, tile=t, n_lat=n_lat, seq=seq, n_batch=n_batch)
    in_specs = [pl.BlockSpec((t, d), lambda i: (i, 0)),
                pl.BlockSpec((1, d), lambda i: (0, 0)),
                pl.BlockSpec((None, 1, d), lambda i: (row(i), 0, sc_chunk)),
                pl.BlockSpec((None, 1, d), lambda i: (row(i), 0, sh_chunk))]
    args = [h, w.reshape(1, d), mod3, mod3]
    out_specs = pl.BlockSpec((t, d), lambda i: (i, 0))
    out_shape = jax.ShapeDtypeStruct((n_rows, d), BF16)
    if router_w is not None:
        in_specs.append(pl.BlockSpec(router_w.shape, lambda i: (0, 0)))
        args.append(router_w)
        out_specs = [out_specs, pl.BlockSpec((t, router_w.shape[1]), lambda i: (i, 0))]
        out_shape = [jax.ShapeDtypeStruct((n_rows, d), F32),
                     jax.ShapeDtypeStruct((n_rows, router_w.shape[1]), F32)]
    return pl.pallas_call(
        functools.partial(_normmod_kernel, with_router=router_w is not None),
        grid=(n_rows // t,),
        in_specs=in_specs, out_specs=out_specs, out_shape=out_shape,
        compiler_params=_cparams("parallel"),
        name="norm_modulate",
    )(*args)


def _matmul_kernel(*refs, n_in, epilogue):
    acc = None
    for x_ref, w_ref in zip(refs[:n_in], refs[n_in:2 * n_in]):
        part = jnp.dot(x_ref[...], w_ref[...], preferred_element_type=F32)
        acc = part if acc is None else acc + part
    rest = refs[2 * n_in:]
    if epilogue == "residual":
        h_ref, g_ref, o_ref = rest
        o_ref[...] = h_ref[...] + g_ref[...] * acc
    elif epilogue in ("head_norm", "head_norm_rope"):
        w_ref, o_ref = rest[0], rest[-1]
        if epilogue == "head_norm_rope":
            cos, sin = rest[1][...], rest[2][...]
            ri = lax.broadcasted_iota(jnp.int32, (HEAD_DIM, HEAD_DIM), 0)
            ci = lax.broadcasted_iota(jnp.int32, (HEAD_DIM, HEAD_DIM), 1)
            swap = (ri == ci + 1 - 2 * (ci % 2)).astype(BF16)
        for hd in range(acc.shape[1] // HEAD_DIM):
            cols = slice(hd * HEAD_DIM, (hd + 1) * HEAD_DIM)
            y = acc[:, cols]
            y = y * lax.rsqrt(jnp.mean(y * y, axis=-1, keepdims=True) + EPS) * w_ref[:, cols]
            if epilogue == "head_norm_rope":
                partner = jnp.dot(y.astype(BF16), swap, preferred_element_type=F32)
                y = y * cos + partner * sin
            o_ref[:, cols] = y.astype(o_ref.dtype)
    else:
        (o_ref,) = rest
        o_ref[...] = acc.astype(o_ref.dtype)


def matmul(xs, ws, n_rows, n_out, out_dtype, tn, residual=None, head_norm=None):
    tm = next(t for t in (1024, 512, ROW_BLOCK) if n_rows % t == 0)
    in_specs = [pl.BlockSpec((tm, x.shape[1]), lambda i, j: (i, 0)) for x in xs]
    for x, (_, row_off, col_off) in zip(xs, ws):
        k = x.shape[1]
        assert row_off % k == 0 and col_off % tn == 0 and n_out % tn == 0
        in_specs.append(pl.BlockSpec((k, tn), functools.partial(
            lambda i, j, rb, cb: (rb, j + cb), rb=row_off // k, cb=col_off // tn)))
    args = list(xs) + [w for w, _, _ in ws]
    epilogue = "plain"
    if residual is not None:
        epilogue = "residual"
        h, mod3, gate_chunk, (n_lat, seq, n_batch) = residual
        row = functools.partial(_mod_row, tile=tm, n_lat=n_lat, seq=seq, n_batch=n_batch)
        nj = mod3.shape[2] // 6 // tn
        in_specs += [pl.BlockSpec((tm, tn), lambda i, j: (i, j)),
                     pl.BlockSpec((None, 1, tn), lambda i, j: (row(i), 0, gate_chunk * nj + j))]
        args += [h, mod3]
    elif head_norm is not None:
        epilogue = "head_norm"
        w_cols, rope = head_norm
        in_specs.append(pl.BlockSpec((1, tn), lambda i, j: (0, j)))
        args.append(w_cols)
        if rope is not None:
            epilogue = "head_norm_rope"
            cos, sin, seq, n_lat = rope
            per_seq, lat_tiles = seq // tm, n_lat // tm
            tab = pl.BlockSpec((tm, HEAD_DIM), lambda i, j: (jnp.where(i < lat_tiles, i % per_seq, per_seq), 0))
            in_specs += [tab, tab]
            args += [cos, sin]
    return pl.pallas_call(
        functools.partial(_matmul_kernel, n_in=len(xs), epilogue=epilogue),
        grid=(n_rows // tm, n_out // tn),
        in_specs=in_specs,
        out_specs=pl.BlockSpec((tm, tn), lambda i, j: (i, j)),
        out_shape=jax.ShapeDtypeStruct((n_rows, n_out), out_dtype),
        compiler_params=_cparams("parallel", "parallel"),
        name="matmul_" + epilogue,
    )(*args)


CONV_PAD = 8


def _conv_silu_kernel(x_ref, w_ref, b_ref, o_ref, pad_ref):
    seq, width = o_ref.shape
    zeros = jnp.zeros((CONV_PAD, width), F32)
    pad_ref[pl.ds(0, CONV_PAD), :] = zeros
    pad_ref[pl.ds(CONV_PAD + seq, CONV_PAD), :] = zeros
    pad_ref[pl.ds(CONV_PAD, seq), :] = x_ref[...].astype(F32)
    rows = 256
    for base in range(0, seq, rows):
        acc = jnp.broadcast_to(b_ref[...], (rows, width))
        for k in range(SSD_CONV):
            acc = acc + w_ref[pl.ds(k, 1), :] * pad_ref[pl.ds(base + CONV_PAD - SSD_CONV // 2 + k, rows), :]
        o_ref[pl.ds(base, rows), :] = (acc * jax.nn.sigmoid(acc)).astype(o_ref.dtype)


def conv_silu(p_zx, conv_w, conv_b, n_seq, seq, row_off):
    tc = 512
    col0 = SSD_INNER // tc
    off = row_off // seq
    return pl.pallas_call(
        _conv_silu_kernel,
        grid=(n_seq, SSD_XBC // tc),
        in_specs=[pl.BlockSpec((seq, tc), lambda b, j: (b + off, col0 + j)),
                  pl.BlockSpec((SSD_CONV, tc), lambda b, j: (0, j)),
                  pl.BlockSpec((1, tc), lambda b, j: (0, j))],
        out_specs=pl.BlockSpec((seq, tc), lambda b, j: (b, j)),
        out_shape=jax.ShapeDtypeStruct((n_seq * seq, SSD_XBC), BF16),
        scratch_shapes=[pltpu.VMEM((seq + 2 * CONV_PAD, tc), F32)],
        compiler_params=_cparams("parallel", "parallel"),
        name="conv_silu",
    )(p_zx, conv_w, conv_b.reshape(1, SSD_XBC))


def _ssd_dt_kernel(p_ref, bias_ref, a_ref, dt_ref, acs_ref, acst_ref):
    n = p_ref.shape[0]
    pre = p_ref[...] + bias_ref[...]
    dt = jnp.maximum(pre, 0.0) + jnp.log1p(jnp.exp(-jnp.abs(pre)))
    a = dt * a_ref[...]
    li = lax.broadcasted_iota(jnp.int32, (n, n), 0)
    si = lax.broadcasted_iota(jnp.int32, (n, n), 1)
    lower = (si <= li).astype(F32)
    upper = (si >= li).astype(F32)
    fwd = jnp.dot(lower, a, precision=HIGHEST, preferred_element_type=F32)
    bwd = jnp.dot(upper, a, precision=HIGHEST, preferred_element_type=F32)
    lane = lax.broadcasted_iota(jnp.int32, fwd.shape, 1)
    acs = jnp.where(lane < SSD_HEADS, fwd, bwd)
    dt_ref[...] = dt
    acs_ref[...] = acs
    acst_ref[...] = acs.T


def ssd_dt(p_dt, dt_bias, a_log):
    rows = p_dt.shape[0]
    pad = LANES - 2 * SSD_HEADS
    bias = jnp.pad(dt_bias.reshape(1, 2 * SSD_HEADS), ((0, 0), (0, pad)))
    a_neg = jnp.pad(-jnp.exp(a_log.astype(F32)).reshape(1, 2 * SSD_HEADS), ((0, 0), (0, pad)))
    c = SSD_CHUNK
    return pl.pallas_call(
        _ssd_dt_kernel,
        grid=(rows // c,),
        in_specs=[pl.BlockSpec((c, LANES), lambda i: (i, 0)),
                  pl.BlockSpec((1, LANES), lambda i: (0, 0)),
                  pl.BlockSpec((1, LANES), lambda i: (0, 0))],
        out_specs=[pl.BlockSpec((c, LANES), lambda i: (i, 0)),
                   pl.BlockSpec((c, LANES), lambda i: (i, 0)),
                   pl.BlockSpec((LANES, c), lambda i: (0, i))],
        out_shape=[jax.ShapeDtypeStruct((rows, LANES), F32),
                   jax.ShapeDtypeStruct((rows, LANES), F32),
                   jax.ShapeDtypeStruct((LANES, rows), F32)],
        compiler_params=_cparams("parallel"),
        name="ssd_dt",
    )(p_dt, bias, a_neg)


def _ssd_scan_kernel(*refs):
    fwd, bwd, outs = refs[:6], refs[6:12], refs[12:]

    @pl.when(pl.program_id(2) == 0)
    def _():
        for state_ref in outs[2:]:
            state_ref[...] = jnp.zeros_like(state_ref)

    _ssd_chunk(*fwd, outs[0], outs[2], reverse=False)
    _ssd_chunk(*bwd, outs[1], outs[3], reverse=True)


def _ssd_chunk(x_ref, b_ref, c_ref, dt_ref, acs_ref, acsr_ref, y_ref, state_ref, *, reverse):
    n = x_ref.shape[0]
    heads = dt_ref.shape[1]
    pair_w = 2 * SSD_HEAD_DIM
    x = x_ref[...].astype(F32)
    bm = b_ref[...]
    cm = c_ref[...]
    dt = dt_ref[...]
    acs = acs_ref[...]
    acs_row = acsr_ref[...]
    end = 0 if reverse else n - 1
    total = acs[end:end + 1, :]
    e_tot = jnp.exp(total)

    li = lax.broadcasted_iota(jnp.int32, (n, n), 0)
    si = lax.broadcasted_iota(jnp.int32, (n, n), 1)
    visible = (si >= li) if reverse else (si <= li)
    lo = lax.broadcasted_iota(jnp.int32, (1, pair_w), 1) < SSD_HEAD_DIM

    cb = lax.dot_general(cm, bm, (((1,), (1,)), ((), ())), preferred_element_type=F32)
    state = state_ref[...]
    y_off = jnp.dot(cm, state.astype(BF16), preferred_element_type=F32)

    xd_parts, decay_parts = [], []
    for i in range(heads // 2):
        j0, j1 = 2 * i, 2 * i + 1
        cols = slice(i * pair_w, (i + 1) * pair_w)
        xp = x[:, cols] * jnp.where(lo, dt[:, j0:j0 + 1], dt[:, j1:j1 + 1])
        ms, xs, acs_b = [], [], []
        for j, keep in ((j0, lo), (j1, jnp.logical_not(lo))):
            acs_b.append(jnp.broadcast_to(acs[:, j:j + 1], (n, n)))
            diff = acs_b[-1] - acs_row[j:j + 1, :]
            ms.append((cb * jnp.exp(jnp.where(visible, diff, MASK_VALUE))).astype(BF16))
            xs.append(jnp.where(keep, xp, 0.0).astype(BF16))
        y_diag = jnp.dot(jnp.concatenate(ms, axis=1), jnp.concatenate(xs, axis=0), preferred_element_type=F32)
        acs_pair = jnp.where(lo, acs_b[0], acs_b[1])
        tot_pair = jnp.where(lo, total[:, j0:j0 + 1], total[:, j1:j1 + 1])
        y_ref[:, cols] = (y_diag + y_off[:, cols] * jnp.exp(acs_pair)).astype(y_ref.dtype)
        xd_parts.append((xp * jnp.exp(tot_pair - acs_pair)).astype(BF16))
        decay_parts.append(jnp.where(lo, e_tot[:, j0:j0 + 1], e_tot[:, j1:j1 + 1]))
    xd = jnp.concatenate(xd_parts, axis=1)
    decay = jnp.concatenate(decay_parts, axis=1)
    bt = bm.astype(F32).T.astype(BF16)
    state_ref[...] = state * decay + jnp.dot(bt, xd, preferred_element_type=F32)


def ssd_scan(xbc, dt_col, acs_col, acs_row, n_batch, seq, ctx_len):
    c = SSD_CHUNK
    lat_chunks, ctx_chunks = seq // c, ctx_len // c
    ctx_base = n_batch * lat_chunks
    n_steps = lat_chunks + ctx_chunks
    gw = SSD_GROUP_W
    heads = SSD_HEADS // SSD_GROUPS
    b_col0 = SSD_INNER // SSD_STATE
    c_col0 = (SSD_INNER + SSD_GN) // SSD_STATE

    def specs(reverse):
        def chunk(b, k):
            kc = (ctx_chunks - 1 - k) if reverse else k
            kl = (n_steps - 1 - k) if reverse else (k - ctx_chunks)
            return jnp.where(k < ctx_chunks, ctx_base + b * ctx_chunks + kc, b * lat_chunks + kl)

        d = int(reverse)
        ins = [pl.BlockSpec((c, gw), lambda b, g, k: (chunk(b, k), g)),
               pl.BlockSpec((c, SSD_STATE), lambda b, g, k: (chunk(b, k), b_col0 + g)),
               pl.BlockSpec((c, SSD_STATE), lambda b, g, k: (chunk(b, k), c_col0 + g)),
               pl.BlockSpec((None, None, c, heads), lambda b, g, k: (d, g, chunk(b, k), 0)),
               pl.BlockSpec((None, None, c, heads), lambda b, g, k: (d, g, chunk(b, k), 0)),
               pl.BlockSpec((None, None, heads, c), lambda b, g, k: (d, g, 0, chunk(b, k)))]
        return ins, pl.BlockSpec((c, gw), lambda b, g, k: (chunk(b, k), g))

    (in_f, out_f), (in_b, out_b) = specs(False), specs(True)
    y_shape = jax.ShapeDtypeStruct((xbc.shape[0], SSD_INNER), BF16)
    args = (xbc, xbc, xbc, dt_col, acs_col, acs_row)
    return pl.pallas_call(
        _ssd_scan_kernel,
        grid=(n_batch, SSD_GROUPS, n_steps),
        in_specs=in_f + in_b,
        out_specs=[out_f, out_b],
        out_shape=[y_shape, y_shape],
        scratch_shapes=[pltpu.VMEM((SSD_STATE, gw), F32), pltpu.VMEM((SSD_STATE, gw), F32)],
        compiler_params=_cparams("parallel", "parallel", "arbitrary"),
        name="ssd_scan",
    )(*args, *args)


def _ssd_gate_kernel(yf_ref, yb_ref, x_ref, z_ref, d_ref, w_ref, o_ref):
    z = z_ref[...].astype(F32)
    y = yf_ref[...].astype(F32) + yb_ref[...].astype(F32) + d_ref[...] * x_ref[...].astype(F32)
    g = y * (z * jax.nn.sigmoid(z))
    o = g * lax.rsqrt(jnp.mean(g * g, axis=-1, keepdims=True) + EPS) * w_ref[...]
    o_ref[...] = o.astype(o_ref.dtype)


def ssd_gate(y_f, y_b, xbc, p_zx, d_skip, norm_w):
    rows = y_f.shape[0]
    t, gw = ROW_BLOCK, SSD_GROUP_W
    d_cols = jnp.repeat(d_skip.astype(F32), SSD_HEAD_DIM).reshape(1, SSD_INNER)
    spec = pl.BlockSpec((t, gw), lambda i, g: (i, g))
    vec = pl.BlockSpec((1, gw), lambda i, g: (0, g))
    return pl.pallas_call(
        _ssd_gate_kernel,
        grid=(rows // t, SSD_GROUPS),
        in_specs=[spec, spec, spec, spec, vec, vec],
        out_specs=spec,
        out_shape=jax.ShapeDtypeStruct((rows, SSD_INNER), BF16),
        compiler_params=_cparams("parallel", "parallel"),
        name="ssd_gate",
    )(y_f, y_b, xbc, p_zx, d_cols, norm_w.reshape(1, SSD_INNER))


ROPE_IDENTITY_ROWS = 1024


def _rope_tables(seq):
    t = jnp.arange(seq, dtype=jnp.int32)
    row = (t // GRID_W).astype(F32)
    col = (t % GRID_W).astype(F32)
    axis_dim = HEAD_DIM // 2
    inv = 1.0 / (ROPE_THETA ** (jnp.arange(0, axis_dim, 2, dtype=F32) / axis_dim))
    ang = jnp.concatenate([row[:, None] * inv[None], col[:, None] * inv[None]], axis=-1)
    cos = jnp.repeat(jnp.cos(ang), 2, axis=-1)
    sin = jnp.stack([-jnp.sin(ang), jnp.sin(ang)], axis=-1).reshape(seq, HEAD_DIM)
    identity = jnp.ones((ROPE_IDENTITY_ROWS, HEAD_DIM), F32)
    return jnp.concatenate([cos, identity], axis=0), jnp.concatenate([sin, 0.0 * identity], axis=0)


LOG2E = math.log2(math.e)
KEY_CHUNK = 256
DIFF_Q_TILE = 512


def _scores(q, k, bias):
    s = lax.dot_general(q, k, (((1,), (1,)), ((), ())), preferred_element_type=F32)
    return s if bias is None else s + bias


def _attend_two_pass(q, pieces):
    scores = [_scores(q, k, bias) for k, _, bias in pieces]
    m = functools.reduce(jnp.maximum, [jnp.max(s, axis=-1, keepdims=True) for s in scores])
    denom = acc = None
    for s, (_, v, _) in zip(scores, pieces):
        p = jnp.exp2(s - m)
        p_sum = jnp.sum(p, axis=-1, keepdims=True)
        pv = jnp.dot(p.astype(v.dtype), v, preferred_element_type=F32)
        denom = p_sum if denom is None else denom + p_sum
        acc = pv if acc is None else acc + pv
    return acc / denom


def _attend(q, pieces):
    chunks = []
    for k, v, bias in pieces:
        for c0 in range(0, k.shape[0], KEY_CHUNK):
            c1 = min(c0 + KEY_CHUNK, k.shape[0])
            chunks.append((k[c0:c1], v[c0:c1], None if bias is None else bias[:, c0:c1]))
    m = denom = acc = None
    for k, v, bias in chunks:
        s = lax.dot_general(q, k, (((1,), (1,)), ((), ())), preferred_element_type=F32)
        if bias is not None:
            s = s + bias
        m_chunk = jnp.max(s, axis=-1, keepdims=True)
        m_new = m_chunk if m is None else jnp.maximum(m, m_chunk)
        p = jnp.exp2(s - m_new)
        p_sum = jnp.sum(p, axis=-1, keepdims=True)
        pv = jnp.dot(p.astype(v.dtype), v, preferred_element_type=F32)
        if m is None:
            denom, acc = p_sum, pv
        else:
            alpha = jnp.exp2(m - m_new)
            denom = alpha * denom + p_sum
            acc = alpha * acc + pv
        m = m_new
    return acc / denom


NA_HEADS_PER_STEP = 4


def _na_kernel(q_ref, k0_ref, k1_ref, k2_ref, kc_ref, v0_ref, v1_ref, v2_ref, vc_ref, bias_ref, o_ref):
    t = q_ref.shape[0]
    for hd in range(NA_HEADS_PER_STEP):
        cols = slice(hd * HEAD_DIM, (hd + 1) * HEAD_DIM)
        pieces = [(k_ref[:, cols], v_ref[:, cols], bias_ref[hd, :, i * t:(i + 1) * t])
                  for i, (k_ref, v_ref) in enumerate(((k0_ref, v0_ref), (k1_ref, v1_ref), (k2_ref, v2_ref)))]
        pieces.append((kc_ref[:, cols], vc_ref[:, cols], None))
        o_ref[:, cols] = _attend_two_pass(q_ref[:, cols], pieces).astype(o_ref.dtype)


def _na_bias_table(rpb, rows):
    n_blocks = rows // NA_QROWS
    w0_of = lambda j: min(max(NA_QROWS * j - NA_WIN_ROWS // 2, 0), rows - NA_KROWS)
    geoms, pattern_of = [], []
    for j in range(n_blocks):
        r0s = tuple(min(max(NA_QROWS * j + a - NA_WIN_ROWS // 2, 0), rows - NA_WIN_ROWS) - w0_of(j)
                    for a in range(NA_QROWS))
        key = (NA_QROWS * j - w0_of(j), r0s)
        if key not in geoms:
            geoms.append(key)
        pattern_of.append(geoms.index(key))
    qc = jnp.arange(GRID_W)[:, None]
    kc = jnp.arange(GRID_W)[None, :]
    cs = jnp.clip(qc - NA_WIN_COLS // 2, 0, GRID_W - NA_WIN_COLS)
    col_ok = (kc >= cs) & (kc < cs + NA_WIN_COLS)
    dc = jnp.clip(kc - qc + NA_WIN_COLS - 1, 0, 2 * NA_WIN_COLS - 2)
    pick = (dc[None] == jnp.arange(2 * NA_WIN_COLS - 1)[:, None, None]).astype(F32)
    tiles = jnp.einsum('hrd,dqk->hrqk', rpb.astype(F32), pick, precision=HIGHEST)
    tiles = jnp.where(col_ok, tiles * LOG2E, MASK_VALUE)
    masked = jnp.full((rpb.shape[0], GRID_W, GRID_W), MASK_VALUE, F32)
    tables = []
    for q_row0, r0s in geoms:
        tile_rows = []
        for qi in range(NA_QROWS):
            row = [tiles[:, ki - (q_row0 + qi) + NA_WIN_ROWS - 1] if r0s[qi] <= ki < r0s[qi] + NA_WIN_ROWS
                   else masked for ki in range(NA_KROWS)]
            tile_rows.append(jnp.concatenate(row, axis=2))
        tables.append(jnp.concatenate(tile_rows, axis=1))
    tables.append(jnp.full_like(tables[0], MASK_VALUE))
    pattern_of.append(len(geoms))
    w0_blocks = [w0_of(j) // NA_QROWS for j in range(n_blocks)] + [0]
    return jnp.stack(tables, axis=1), pattern_of, w0_blocks


def _lookup(values, j):
    out = jnp.int32(values[-1])
    for idx in range(len(values) - 2, -1, -1):
        out = jnp.where(j == idx, jnp.int32(values[idx]), out)
    return out


def na_attention(qn, kn, p_na, v_col_off, rpb, n_batch, seq, ctx_len):
    t = NA_QROWS * GRID_W
    assert t == ROW_BLOCK == ctx_len
    rows = seq // GRID_W
    n_blocks = seq // t
    bias, pattern_of, w0_blocks = _na_bias_table(rpb, rows)
    hw = NA_HEADS_PER_STEP * HEAD_DIM
    v_col0 = v_col_off // hw
    ctx_base = n_batch * n_blocks

    def q_blk(j, b):
        return jnp.where(j < n_blocks, b * n_blocks + j, ctx_base + b)

    def kv_blk(piece):
        return lambda h, j, b: (b * n_blocks + _lookup(w0_blocks, j) + piece, h)

    def kv_blk_v(piece):
        return lambda h, j, b: (b * n_blocks + _lookup(w0_blocks, j) + piece, v_col0 + h)

    blk = lambda fn: pl.BlockSpec((t, hw), fn)
    in_specs = ([blk(lambda h, j, b: (q_blk(j, b), h))]
                + [blk(kv_blk(i)) for i in range(3)] + [blk(lambda h, j, b: (ctx_base + b, h))]
                + [blk(kv_blk_v(i)) for i in range(3)] + [blk(lambda h, j, b: (ctx_base + b, v_col0 + h))]
                + [pl.BlockSpec((NA_HEADS_PER_STEP, None, t, 3 * t),
                                lambda h, j, b: (h, _lookup(pattern_of, j), 0, 0))])
    return pl.pallas_call(
        _na_kernel,
        grid=(NA_HEADS // NA_HEADS_PER_STEP, n_blocks + 1, n_batch),
        in_specs=in_specs,
        out_specs=blk(lambda h, j, b: (q_blk(j, b), h)),
        out_shape=jax.ShapeDtypeStruct((qn.shape[0], NA_WIDTH), BF16),
        compiler_params=_cparams("parallel", "parallel", "parallel"),
        name="na_attention",
    )(qn, kn, kn, kn, kn, p_na, p_na, p_na, p_na, bias)


def _gqa_kernel(q_ref, kl_ref, kc_ref, vl_ref, vc_ref, o_ref):
    t = q_ref.shape[0]
    q = jnp.concatenate([q_ref[:, g * HEAD_DIM:(g + 1) * HEAD_DIM] for g in range(GQA_GROUP)], axis=0)
    o = _attend(q, [(kc_ref[...], vc_ref[...], None), (kl_ref[...], vl_ref[...], None)])
    for g in range(GQA_GROUP):
        o_ref[:, g * HEAD_DIM:(g + 1) * HEAD_DIM] = o[g * t:(g + 1) * t].astype(o_ref.dtype)


def gqa_attention(qn, kn, p_kv, v_col_off, n_batch, seq, ctx_len):
    t = ROW_BLOCK
    q_tiles = seq // t
    ctx_base = n_batch * seq // ctx_len
    v0 = v_col_off // HEAD_DIM
    gw = GQA_GROUP * HEAD_DIM
    return pl.pallas_call(
        _gqa_kernel,
        grid=(n_batch, GQA_KV_HEADS, q_tiles),
        in_specs=[pl.BlockSpec((t, gw), lambda b, h, i: (b * q_tiles + i, h)),
                  pl.BlockSpec((seq, HEAD_DIM), lambda b, h, i: (b, h)),
                  pl.BlockSpec((ctx_len, HEAD_DIM), lambda b, h, i: (ctx_base + b, h)),
                  pl.BlockSpec((seq, HEAD_DIM), lambda b, h, i: (b, v0 + h)),
                  pl.BlockSpec((ctx_len, HEAD_DIM), lambda b, h, i: (ctx_base + b, v0 + h))],
        out_specs=pl.BlockSpec((t, gw), lambda b, h, i: (b * q_tiles + i, h)),
        out_shape=jax.ShapeDtypeStruct((n_batch * seq, GQA_Q), BF16),
        compiler_params=_cparams("parallel", "parallel", "parallel"),
        name="gqa_attention",
    )(qn, kn, kn, p_kv, p_kv)


def _diff_kernel(lam_ref, q_ref, kl_ref, kc_ref, vl_ref, vc_ref, w_ref, o_ref, *, out_scale):
    outs = []
    for comp in range(2):
        cols = slice(comp * HEAD_DIM, (comp + 1) * HEAD_DIM)
        outs.append(_attend(q_ref[:, cols], [(kc_ref[:, cols], vc_ref[...], None),
                                              (kl_ref[:, cols], vl_ref[...], None)]))
    o = outs[0] - lam_ref[0] * outs[1]
    o = o * lax.rsqrt(jnp.mean(o * o, axis=-1, keepdims=True) + EPS) * w_ref[...]
    o_ref[...] = (o * out_scale).astype(o_ref.dtype)


def diff_attention(qn, q_col_off, kn, p_kv, v_col_off, lam, subln_w, out_scale, n_batch, seq, ctx_len):
    t = DIFF_Q_TILE
    q_tiles = seq // t
    ctx_base = n_batch * seq // ctx_len
    v0 = v_col_off // DIFF_V_DIM
    w2 = 2 * HEAD_DIM
    q0 = q_col_off // w2
    return pl.pallas_call(
        functools.partial(_diff_kernel, out_scale=out_scale),
        grid=(n_batch, DIFF_HEADS, q_tiles),
        in_specs=[pl.BlockSpec(memory_space=pltpu.SMEM),
                  pl.BlockSpec((t, w2), lambda b, h, i: (b * q_tiles + i, q0 + h)),
                  pl.BlockSpec((seq, w2), lambda b, h, i: (b, h)),
                  pl.BlockSpec((ctx_len, w2), lambda b, h, i: (ctx_base + b, h)),
                  pl.BlockSpec((seq, DIFF_V_DIM), lambda b, h, i: (b, v0 + h)),
                  pl.BlockSpec((ctx_len, DIFF_V_DIM), lambda b, h, i: (ctx_base + b, v0 + h)),
                  pl.BlockSpec((1, DIFF_V_DIM), lambda b, h, i: (0, 0))],
        out_specs=pl.BlockSpec((t, DIFF_V_DIM), lambda b, h, i: (b * q_tiles + i, h)),
        out_shape=jax.ShapeDtypeStruct((n_batch * seq, DIFF_V), BF16),
        compiler_params=_cparams("parallel", "parallel", "parallel"),
        name="diff_attention",
    )(lam.reshape(1), qn, kn, kn, p_kv, p_kv, subln_w.reshape(1, DIFF_V_DIM))


def _moe_plan_kernel(route_ref, pos_ref, pad_end_ref, slot_ref, counts_ref, start_ref):
    phase, i = pl.program_id(0), pl.program_id(1)
    t = route_ref.shape[0]
    route = route_ref[...]
    lane = lax.broadcasted_iota(jnp.int32, route.shape, 1)
    chosen = [lane == route[:, ROUTE_EXPERT_LANE + k:ROUTE_EXPERT_LANE + k + 1].astype(jnp.int32)
              for k in range(MOE_TOPK)]
    both = functools.reduce(jnp.add, [c.astype(F32) for c in chosen])
    tile_counts = jnp.sum(both, axis=0, keepdims=True)

    @pl.when((phase == 0) & (i == 0))
    def _():
        counts_ref[...] = jnp.zeros_like(counts_ref)

    @pl.when(phase == 0)
    def _():
        counts_ref[...] += tile_counts

    @pl.when((phase == 1) & (i == 0))
    def _():
        padded = jnp.ceil(counts_ref[...] * (1.0 / MOE_BLOCK)) * MOE_BLOCK
        rows = pad_end_ref.shape[0]
        ki = lax.broadcasted_iota(jnp.int32, (LANES, LANES), 0)
        ni = lax.broadcasted_iota(jnp.int32, (LANES, LANES), 1)
        pad_end = jnp.dot(jnp.broadcast_to(padded, (rows, LANES)), (ki <= ni).astype(F32),
                          precision=HIGHEST, preferred_element_type=F32)
        pad_end_ref[...] = pad_end
        start_ref[...] = pad_end[0:1] - padded
        slot_ref[...] = jnp.zeros_like(slot_ref)

    @pl.when(phase == 1)
    def _():
        li = lax.broadcasted_iota(jnp.int32, (t, t), 0)
        si = lax.broadcasted_iota(jnp.int32, (t, t), 1)
        earlier = jnp.dot((si < li).astype(BF16), both.astype(BF16), preferred_element_type=F32)
        row_of = earlier + start_ref[...]
        out = jnp.zeros(route.shape, F32)
        token = (i * t + 1 + lax.broadcasted_iota(jnp.int32, (t, 1), 0)).astype(F32)
        blk_lane = lax.broadcasted_iota(jnp.int32, (t, slot_ref.shape[0]), 1)
        placed = jnp.zeros(slot_ref.shape, F32)
        for k in range(MOE_TOPK):
            p = jnp.sum(jnp.where(chosen[k], row_of, 0.0), axis=-1, keepdims=True)
            out = jnp.where(lane == k, p, out)
            blk = jnp.floor(p * (1.0 / MOE_BLOCK))
            in_blk = (p - blk * MOE_BLOCK).astype(jnp.int32)
            block_hot = (blk_lane == blk.astype(jnp.int32)).astype(F32)
            row_hot = jnp.where(lane == in_blk, token, 0.0)
            placed = placed + jnp.dot(block_hot.T, row_hot, precision=HIGHEST, preferred_element_type=F32)
        slot_ref[...] += placed
        pos_ref[...] = out.astype(jnp.int32)
        start_ref[...] += tile_counts


def moe_plan(route, n_blocks):
    n = route.shape[0]
    t = ROW_BLOCK
    blocks_pad = -(-n_blocks // LANES) * LANES
    return pl.pallas_call(
        _moe_plan_kernel,
        grid=(2, n // t),
        in_specs=[pl.BlockSpec((t, LANES), lambda p, i: (i, 0))],
        out_specs=[pl.BlockSpec((t, LANES), lambda p, i: (i * p, 0)),
                   pl.BlockSpec((8, LANES), lambda p, i: (0, 0)),
                   pl.BlockSpec((blocks_pad, MOE_BLOCK), lambda p, i: (0, 0))],
        out_shape=[jax.ShapeDtypeStruct((n, LANES), jnp.int32),
                   jax.ShapeDtypeStruct((8, LANES), F32),
                   jax.ShapeDtypeStruct((blocks_pad, MOE_BLOCK), F32)],
        scratch_shapes=[pltpu.VMEM((1, LANES), F32), pltpu.VMEM((1, LANES), F32)],
        compiler_params=_cparams("arbitrary", "arbitrary"),
        name="moe_plan",
    )(route)


MOE_DMA_CHUNKS = 4


def _moe_kernel(be_ref, next_ref, seg_ref, nused_ref, x_ref, w1_hbm, w3_hbm, w2_hbm, o_ref,
                s1, s3, s2, w1b, w3b, w2b, sems, *, layer):
    i = pl.program_id(0)
    slot = seg_ref[i] % 2

    def weight_copies(e, slot):
        copies = []
        for n, (w_hbm, stage) in enumerate(((w1_hbm, s1), (w3_hbm, s3), (w2_hbm, s2))):
            rows = stage.shape[1] // MOE_DMA_CHUNKS
            for c in range(MOE_DMA_CHUNKS):
                slab = pl.ds(c * rows, rows)
                copies.append(pltpu.make_async_copy(w_hbm.at[layer, e, slab], stage.at[slot, slab],
                                                    sems.at[slot, n, c]))
        return copies

    @pl.when(i == 0)
    def _():
        for c in weight_copies(be_ref[0], slot):
            c.start()

    prev = be_ref[jnp.maximum(i - 1, 0)]

    @pl.when((i == 0) | (be_ref[i] != prev))
    def _():
        for c in weight_copies(be_ref[i], slot):
            c.wait()

        @pl.when(next_ref[i] >= 0)
        def _():
            for c in weight_copies(next_ref[i], 1 - slot):
                c.start()

        w1b[...] = s1[slot].astype(BF16)
        w3b[...] = s3[slot].astype(BF16)
        w2b[...] = s2[slot].astype(BF16)

    @pl.when(i < nused_ref[0])
    def _():
        x = x_ref[...].astype(BF16)
        a = jnp.dot(x, w1b[...], preferred_element_type=F32)
        b = jnp.dot(x, w3b[...], preferred_element_type=F32)
        hid = (a * jax.nn.sigmoid(a)) * b
        o_ref[...] = jnp.dot(hid.astype(BF16), w2b[...], preferred_element_type=F32)

    @pl.when(i >= nused_ref[0])
    def _():
        o_ref[...] = jnp.zeros_like(o_ref)


def moe_experts(x_sorted, block_e, next_e, n_used, w1, w3, w2, layer):
    n_blocks = block_e.shape[0]
    d, hid = w1.shape[2], w1.shape[3]
    hbm = pl.BlockSpec(memory_space=pl.ANY)
    changed = jnp.concatenate([jnp.zeros((1,), jnp.int32), (block_e[1:] != block_e[:-1]).astype(jnp.int32)])
    segment = jnp.cumsum(changed, dtype=jnp.int32)
    row_spec = pl.BlockSpec((MOE_BLOCK, d), lambda i, be, nx, sg, nu: (i, 0))
    grid_spec = pltpu.PrefetchScalarGridSpec(
        num_scalar_prefetch=4,
        grid=(n_blocks,),
        in_specs=[row_spec, hbm, hbm, hbm],
        out_specs=row_spec,
        scratch_shapes=[pltpu.VMEM((2, d, hid), F32), pltpu.VMEM((2, d, hid), F32), pltpu.VMEM((2, hid, d), F32),
                        pltpu.VMEM((d, hid), BF16), pltpu.VMEM((d, hid), BF16), pltpu.VMEM((hid, d), BF16),
                        pltpu.SemaphoreType.DMA((2, 3, MOE_DMA_CHUNKS))],
    )
    return pl.pallas_call(
        functools.partial(_moe_kernel, layer=layer),
        grid_spec=grid_spec,
        out_shape=jax.ShapeDtypeStruct((n_blocks * MOE_BLOCK, d), F32),
        compiler_params=_cparams("arbitrary"),
        name="moe_experts",
    )(block_e, next_e, segment, n_used, x_sorted, w1, w3, w2)


def _combine_kernel(h_ref, g_ref, route_ref, *refs):
    y_refs, o_ref = refs[:-1], refs[-1]
    acc = None
    for k, y_ref in enumerate(y_refs):
        part = y_ref[...] * route_ref[:, ROUTE_GATE_LANE + k:ROUTE_GATE_LANE + k + 1]
        acc = part if acc is None else acc + part
    o_ref[...] = h_ref[...] + g_ref[...] * acc


def moe_combine(h, mod3, gate_chunk, route, ys, geom):
    n_lat, seq, n_batch = geom
    n_rows, d = ys[0].shape
    t = ROW_BLOCK
    row = functools.partial(_mod_row, tile=t, n_lat=n_lat, seq=seq, n_batch=n_batch)
    spec = pl.BlockSpec((t, d), lambda i: (i, 0))
    return pl.pallas_call(
        _combine_kernel,
        grid=(n_rows // t,),
        in_specs=[spec, pl.BlockSpec((None, 1, d), lambda i: (row(i), 0, gate_chunk)),
                  pl.BlockSpec((t, LANES), lambda i: (i, 0))] + [spec] * len(ys),
        out_specs=spec,
        out_shape=jax.ShapeDtypeStruct((n_rows, d), F32),
        compiler_params=_cparams("parallel"),
        name="moe_combine",
    )(h, mod3, route, *ys)


def moe_layer(h, f, route, mod3, gate_chunk, w1, w3, w2, layer, geom):
    n = f.shape[0]
    n_assign = n * MOE_TOPK
    n_blocks = -(-n_assign // MOE_BLOCK) + MOE_EXPERTS
    pos, pad_end, slot = moe_plan(route, n_blocks)
    pos = pos[:, :MOE_TOPK]
    pad_end = pad_end[0, :MOE_EXPERTS].astype(jnp.int32)
    n_used = pad_end[-1] // MOE_BLOCK
    blk = jnp.minimum(jnp.arange(n_blocks, dtype=jnp.int32), n_used - 1) * MOE_BLOCK
    block_e = jnp.minimum(jnp.sum(pad_end[None, :] <= blk[:, None], axis=1, dtype=jnp.int32), MOE_EXPERTS - 1)
    seg_end = pad_end[block_e] // MOE_BLOCK
    next_e = jnp.where(seg_end < n_used, block_e[jnp.minimum(seg_end, n_blocks - 1)], -1)
    slot = slot[:n_blocks].reshape(-1).astype(jnp.int32)
    slot_tok = jnp.where(slot > 0, slot - 1, jnp.arange(n_blocks * MOE_BLOCK, dtype=jnp.int32) % n)
    x_sorted = f.at[slot_tok].get(mode='promise_in_bounds')
    yb = moe_experts(x_sorted, block_e, next_e, n_used.reshape(1), w1, w3, w2, layer)
    ys = [yb.at[pos[:, k]].get(mode='promise_in_bounds') for k in range(MOE_TOPK)]
    return moe_combine(h, mod3, gate_chunk, route, ys, geom)


def _even_layer_mix(a, in_w, conv_w, conv_b, dt_bias, a_log, d_skip, ssd_norm_w, q_norm, k_norm, rpb,
                    n_batch, seq, ctx_len):
    n_rows = a.shape[0]
    n_lat = n_batch * seq
    zx_end = SSD_INNER + SSD_XBC
    dt_end = zx_end + 2 * SSD_HEADS
    w = jnp.concatenate([in_w[:, :zx_end], in_w[:, dt_end:],
                         jnp.pad(in_w[:, zx_end:dt_end], ((0, 0), (0, LANES - 2 * SSD_HEADS)))],
                        axis=1).astype(BF16)
    na_off = lambda i: zx_end + i * NA_WIDTH
    dt_off = zx_end + 3 * NA_WIDTH
    tile_w = lambda v, s: (jnp.tile(v.astype(F32), NA_HEADS) * s).reshape(1, NA_WIDTH)
    p_zx = matmul([a], [(w, 0, 0)], n_rows, zx_end, BF16, tn=512)
    p_dt = matmul([a], [(w, 0, dt_off)], n_rows, LANES, F32, tn=LANES)
    qn = matmul([a], [(w, 0, na_off(0))], n_rows, NA_WIDTH, BF16, tn=512,
                head_norm=(tile_w(q_norm, HEAD_DIM ** -0.5 * LOG2E), None))
    kn = matmul([a], [(w, 0, na_off(1))], n_rows, NA_WIDTH, BF16, tn=512,
                head_norm=(tile_w(k_norm, 1.0), None))
    p_v = matmul([a], [(w, 0, na_off(2))], n_rows, NA_WIDTH, BF16, tn=512)

    xbc = jnp.concatenate([conv_silu(p_zx, conv_w, conv_b, n_batch, seq, 0),
                           conv_silu(p_zx, conv_w, conv_b, n_batch, ctx_len, n_lat)], axis=0)
    dt, acs, acs_t = ssd_dt(p_dt, dt_bias, a_log)
    hg = SSD_HEADS // SSD_GROUPS
    col = lambda v: v[:, :2 * SSD_HEADS].reshape(n_rows, 2, SSD_GROUPS, hg).transpose(1, 2, 0, 3)
    dt_col, acs_col = col(dt), col(acs)
    acs_row = acs_t[:2 * SSD_HEADS].reshape(2, SSD_GROUPS, hg, n_rows)
    y_f, y_b = ssd_scan(xbc, dt_col, acs_col, acs_row, n_batch, seq, ctx_len)
    s_out = ssd_gate(y_f, y_b, xbc, p_zx, d_skip, ssd_norm_w)

    n_out = na_attention(qn, kn, p_v, 0, rpb, n_batch, seq, ctx_len)
    return s_out, n_out


def _odd_layer_mix(a, in_w, gq_norm, gk_norm, dq_norm, dk_norm, lam_vecs, subln_w, lambda_init,
                   n_batch, seq, ctx_len):
    n_rows = a.shape[0]
    n_lat = n_batch * seq
    cos, sin = _rope_tables(seq)
    rope = (cos, sin, seq, n_lat)
    scale = HEAD_DIM ** -0.5 * LOG2E
    rep = lambda w, n: jnp.tile(w.astype(F32), n).reshape(1, n * HEAD_DIM)
    w = in_w.astype(BF16)
    gk_off, gv_off, dk_off, dv_off = ODD_Q, ODD_Q + GQA_KV, ODD_Q + 2 * GQA_KV, ODD_Q + 2 * GQA_KV + DIFF_QK
    q_w = jnp.concatenate([rep(gq_norm, GQA_HEADS), rep(dq_norm, 2 * DIFF_HEADS)], axis=1) * scale
    qn = matmul([a], [(w, 0, 0)], n_lat, ODD_Q, BF16, tn=512, head_norm=(q_w, rope))
    gkn = matmul([a], [(w, 0, gk_off)], n_rows, GQA_KV, BF16, tn=512,
                 head_norm=(rep(gk_norm, GQA_KV_HEADS), rope))
    dkn = matmul([a], [(w, 0, dk_off)], n_rows, DIFF_QK, BF16, tn=512,
                 head_norm=(rep(dk_norm, 2 * DIFF_HEADS), rope))
    gv = matmul([a], [(w, 0, gv_off)], n_rows, GQA_KV, BF16, tn=512)
    dv = matmul([a], [(w, 0, dv_off)], n_rows, DIFF_V, BF16, tn=512)
    lv = lam_vecs.astype(F32)
    lam = jnp.exp(jnp.dot(lv[0], lv[1])) - jnp.exp(jnp.dot(lv[2], lv[3])) + lambda_init
    og = gqa_attention(qn, gkn, gv, 0, n_batch, seq, ctx_len)
    od = diff_attention(qn, GQA_Q, dkn, dv, 0, lam, subln_w, 1.0 - lambda_init, n_batch, seq, ctx_len)
    return og, od


def kernel(x, c, ctx, c_ctx, ada_w, ada_b, norm_w, ev_in_w, ev_conv_w, ev_conv_b, ev_dt_bias, ev_a_log, ev_d_skip, ev_ssd_norm_w, ev_na_q_norm, ev_na_k_norm, ev_na_rpb, ev_out_w, od_in_w, od_gqa_q_norm, od_gqa_k_norm, od_diff_q_norm, od_diff_k_norm, od_lambda, od_diff_subln, od_out_w, moe_group_w, moe_expert_w, moe_w1, moe_w3, moe_w2):
    n_batch, seq, d = x.shape
    ctx_len = ctx.shape[1]
    depth = ada_w.shape[0]
    assert depth == 2 and d == D_MODEL
    n_lat, n_ctx = n_batch * seq, n_batch * ctx_len
    n_all = n_lat + n_ctx
    geom = (n_lat, seq, n_batch)

    cc = jnp.concatenate([c, c_ctx[None, :], jnp.zeros((8 - n_batch - 1, d), F32)], axis=0)
    mod = ada_modulation(cc, ada_w, ada_b)
    router_w = lambda l: jnp.pad(jnp.concatenate([moe_group_w[l], moe_expert_w[l]], axis=1),
                                 ((0, 0), (0, LANES - MOE_GROUPS - MOE_EXPERTS)))

    h = jnp.concatenate([x.reshape(n_lat, d), ctx.reshape(n_ctx, d)], axis=0)

    mod3 = mod[0].reshape(8, 1, 6 * d)
    a = norm_modulate(h, norm_w[0, 0], mod3, 1, 0, n_all, geom)
    s_out, n_out = _even_layer_mix(a, ev_in_w[0], ev_conv_w[0], ev_conv_b[0], ev_dt_bias[0], ev_a_log[0],
                                   ev_d_skip[0], ev_ssd_norm_w[0], ev_na_q_norm[0], ev_na_k_norm[0],
                                   ev_na_rpb[0], n_batch, seq, ctx_len)
    out_w = ev_out_w[0].astype(BF16)
    h = matmul([s_out, n_out], [(out_w, 0, 0), (out_w, SSD_INNER, 0)], n_all, d, F32, tn=512,
               residual=(h, mod3, 2, geom))
    f, route = norm_modulate(h, norm_w[0, 1], mod3, 4, 3, n_all, geom, router_w=router_w(0))
    h = moe_layer(h, f, route, mod3, 5, moe_w1, moe_w3, moe_w2, 0, geom)

    mod3 = mod[1].reshape(8, 1, 6 * d)
    a = norm_modulate(h, norm_w[1, 0], mod3, 1, 0, n_all, geom)
    lambda_init = 0.8 - 0.6 * math.exp(-0.3 * 1)
    og, od = _odd_layer_mix(a, od_in_w[0], od_gqa_q_norm[0], od_gqa_k_norm[0], od_diff_q_norm[0],
                            od_diff_k_norm[0], od_lambda[0], od_diff_subln[0], lambda_init,
                            n_batch, seq, ctx_len)
    out_w = od_out_w[0].astype(BF16)
    h = matmul([og, od], [(out_w, 0, 0), (out_w, GQA_Q, 0)], n_lat, d, F32, tn=512,
               residual=(h, mod3, 2, geom))
    f, route = norm_modulate(h, norm_w[1, 1], mod3, 4, 3, n_lat, geom, router_w=router_w(1))
    h = moe_layer(h, f, route, mod3, 5, moe_w1, moe_w3, moe_w2, 1, geom)
    return h.reshape(n_batch, seq, d)
```

```python
import functools
import math

import jax
import jax.numpy as jnp
from jax import lax
from jax.experimental import pallas as pl
from jax.experimental.pallas import tpu as pltpu

F32 = jnp.float32
BF16 = jnp.bfloat16
HIGHEST = lax.Precision.HIGHEST

D_MODEL = 2048
GRID_W = 64
EPS = 1e-6
ROPE_THETA = 10000.0
HEAD_DIM = 128

SSD_HEADS = 32
SSD_HEAD_DIM = 64
SSD_INNER = SSD_HEADS * SSD_HEAD_DIM
SSD_GROUPS = 4
SSD_STATE = 128
SSD_GN = SSD_GROUPS * SSD_STATE
SSD_XBC = SSD_INNER + 2 * SSD_GN
SSD_CONV = 5
SSD_CHUNK = 128
SSD_GROUP_W = SSD_INNER // SSD_GROUPS

NA_HEADS = 16
NA_WIDTH = NA_HEADS * HEAD_DIM
NA_WIN_ROWS = 8
NA_WIN_COLS = 16
NA_QROWS = 4
NA_KROWS = 12

GQA_HEADS = 16
GQA_KV_HEADS = 4
GQA_GROUP = GQA_HEADS // GQA_KV_HEADS
GQA_Q = GQA_HEADS * HEAD_DIM
GQA_KV = GQA_KV_HEADS * HEAD_DIM
DIFF_HEADS = 8
DIFF_QK = DIFF_HEADS * 2 * HEAD_DIM
DIFF_V_DIM = 2 * HEAD_DIM
DIFF_V = DIFF_HEADS * DIFF_V_DIM
ODD_Q = GQA_Q + DIFF_QK

MOE_GROUPS = 8
MOE_EXPERTS_PER_GROUP = 8
MOE_EXPERTS = MOE_GROUPS * MOE_EXPERTS_PER_GROUP
MOE_HIDDEN = 512
MOE_TOPK = 2
MOE_BLOCK = 128

ROW_BLOCK = 256
LANES = 128
MASK_VALUE = -1e30
VMEM_LIMIT = 52 * 1024 * 1024


def _cparams(*sem):
    return pltpu.CompilerParams(dimension_semantics=sem, vmem_limit_bytes=VMEM_LIMIT)


def _mod_row(i, tile, n_lat, seq, n_batch):
    return jnp.where(i < n_lat // tile, i // (seq // tile), n_batch)


def _ada_kernel(c_ref, w_ref, b_ref, o_ref):
    c = c_ref[...]
    s = c * jax.nn.sigmoid(c)
    o_ref[...] = jnp.dot(s, w_ref[...], precision=HIGHEST, preferred_element_type=F32) + b_ref[...]


def ada_modulation(cc, ada_w, ada_b):
    n_layers, d, n_out = ada_w.shape
    rows = cc.shape[0]
    tn = 1024
    return pl.pallas_call(
        _ada_kernel,
        grid=(n_layers, n_out // tn),
        in_specs=[pl.BlockSpec((rows, d), lambda l, j: (0, 0)),
                  pl.BlockSpec((None, d, tn), lambda l, j: (l, 0, j)),
                  pl.BlockSpec((None, 1, tn), lambda l, j: (l, 0, j))],
        out_specs=pl.BlockSpec((None, rows, tn), lambda l, j: (l, 0, j)),
        out_shape=jax.ShapeDtypeStruct((n_layers, rows, n_out), F32),
        compiler_params=_cparams("parallel", "parallel"),
        name="ada_modulation",
    )(cc, ada_w, ada_b.reshape(n_layers, 1, n_out))


def _normmod_kernel(h_ref, w_ref, sc_ref, sh_ref, *rest, with_router):
    x = h_ref[...]
    y = x * lax.rsqrt(jnp.mean(x * x, axis=-1, keepdims=True) + EPS) * w_ref[...]
    a = y * (1.0 + sc_ref[...]) + sh_ref[...]
    if with_router:
        rw_ref, o_ref, route_ref = rest
        logits = jnp.dot(a, rw_ref[...], precision=HIGHEST, preferred_element_type=F32)
        route_ref[...] = _route(logits)
    else:
        (o_ref,) = rest
    o_ref[...] = a.astype(o_ref.dtype)


ROUTE_EXPERT_LANE = 0
ROUTE_GATE_LANE = MOE_TOPK


def _route(logits):
    assert MOE_TOPK == 2
    neg = jnp.float32(-3.0e38)
    lane = lax.broadcasted_iota(jnp.int32, logits.shape, 1)
    is_group = lane < MOE_GROUPS
    gl = jnp.where(is_group, logits, neg)
    g_max = jnp.max(gl, axis=-1, keepdims=True)
    g_idx = jnp.min(jnp.where(gl == g_max, lane, LANES), axis=-1, keepdims=True)
    g_gate = 1.0 / jnp.sum(jnp.where(is_group, jnp.exp(logits - g_max), 0.0), axis=-1, keepdims=True)
    first = MOE_GROUPS + g_idx * MOE_EXPERTS_PER_GROUP
    in_group = (lane >= first) & (lane < first + MOE_EXPERTS_PER_GROUP)
    el = jnp.where(in_group, logits, neg)
    v1 = jnp.max(el, axis=-1, keepdims=True)
    i1 = jnp.min(jnp.where(in_group & (el == v1), lane, LANES), axis=-1, keepdims=True)
    rest = in_group & (lane != i1)
    el2 = jnp.where(rest, logits, neg)
    v2 = jnp.max(el2, axis=-1, keepdims=True)
    i2 = jnp.min(jnp.where(rest & (el2 == v2), lane, LANES), axis=-1, keepdims=True)
    e21 = jnp.exp(v2 - v1)
    p1 = 1.0 / (1.0 + e21)
    p2 = e21 / (1.0 + e21)
    vals = [(i1 - MOE_GROUPS).astype(F32), (i2 - MOE_GROUPS).astype(F32), p1 * g_gate, p2 * g_gate]
    out = jnp.zeros(logits.shape, F32)
    for k, v in enumerate(vals):
        out = jnp.where(lane == k, v, out)
    return out


def norm_modulate(h, w, mod3, sc_chunk, sh_chunk, n_rows, geom, router_w=None):
    n_lat, seq, n_batch = geom
    d = h.shape[1]
    t = ROW_BLOCK
    row = functools.partial(_mod_row, tile=t, n_lat=n_lat, seq=seq, n_batch=n_batch)
    in_specs = [pl.BlockSpec((t, d), lambda i: (i, 0)),
                pl.BlockSpec((1, d), lambda i: (0, 0)),
                pl.BlockSpec((None, 1, d), lambda i: (row(i), 0, sc_chunk)),
                pl.BlockSpec((None, 1, d), lambda i: (row(i), 0, sh_chunk))]
    args = [h, w.reshape(1, d), mod3, mod3]
    out_specs = pl.BlockSpec((t, d), lambda i: (i, 0))
    out_shape = jax.ShapeDtypeStruct((n_rows, d), BF16)
    if router_w is not None:
        in_specs.append(pl.BlockSpec(router_w.shape, lambda i: (0, 0)))
        args.append(router_w)
        out_specs = [out_specs, pl.BlockSpec((t, router_w.shape[1]), lambda i: (i, 0))]
        out_shape = [jax.ShapeDtypeStruct((n_rows, d), F32),
                     jax.ShapeDtypeStruct((n_rows, router_w.shape[1]), F32)]
    return pl.pallas_call(
        functools.partial(_normmod_kernel, with_router=router_w is not None),
        grid=(n_rows // t,),
        in_specs=in_specs, out_specs=out_specs, out_shape=out_shape,
        compiler_params=_cparams("parallel"),
        name="norm_modulate",
    )(*args)


def _matmul_kernel(*refs, n_in, epilogue):
    acc = None
    for x_ref, w_ref in zip(refs[:n_in], refs[n_in:2 * n_in]):
        part = jnp.dot(x_ref[...], w_ref[...], preferred_element_type=F32)
        acc = part if acc is None else acc + part
    rest = refs[2 * n_in:]
    if epilogue == "residual":
        h_ref, g_ref, o_ref = rest
        o_ref[...] = h_ref[...] + g_ref[...] * acc
    elif epilogue in ("head_norm", "head_norm_rope"):
        w_ref, o_ref = rest[0], rest[-1]
        if epilogue == "head_norm_rope":
            cos, sin = rest[1][...], rest[2][...]
            ri = lax.broadcasted_iota(jnp.int32, (HEAD_DIM, HEAD_DIM), 0)
            ci = lax.broadcasted_iota(jnp.int32, (HEAD_DIM, HEAD_DIM), 1)
            swap = (ri == ci + 1 - 2 * (ci % 2)).astype(BF16)
        for hd in range(acc.shape[1] // HEAD_DIM):
            cols = slice(hd * HEAD_DIM, (hd + 1) * HEAD_DIM)
            y = acc[:, cols]
            y = y * lax.rsqrt(jnp.mean(y * y, axis=-1, keepdims=True) + EPS) * w_ref[:, cols]
            if epilogue == "head_norm_rope":
                partner = jnp.dot(y.astype(BF16), swap, preferred_element_type=F32)
                y = y * cos + partner * sin
            o_ref[:, cols] = y.astype(o_ref.dtype)
    else:
        (o_ref,) = rest
        o_ref[...] = acc.astype(o_ref.dtype)


def matmul(xs, ws, n_rows, n_out, out_dtype, tn, residual=None, head_norm=None):
    tm = next(t for t in (1024, 512, ROW_BLOCK) if n_rows % t == 0)
    in_specs = [pl.BlockSpec((tm, x.shape[1]), lambda i, j: (i, 0)) for x in xs]
    for x, (_, row_off, col_off) in zip(xs, ws):
        k = x.shape[1]
        assert row_off % k == 0 and col_off % tn == 0 and n_out % tn == 0
        in_specs.append(pl.BlockSpec((k, tn), functools.partial(
            lambda i, j, rb, cb: (rb, j + cb), rb=row_off // k, cb=col_off // tn)))
    args = list(xs) + [w for w, _, _ in ws]
    epilogue = "plain"
    if residual is not None:
        epilogue = "residual"
        h, mod3, gate_chunk, (n_lat, seq, n_batch) = residual
        row = functools.partial(_mod_row, tile=tm, n_lat=n_lat, seq=seq, n_batch=n_batch)
        nj = mod3.shape[2] // 6 // tn
        in_specs += [pl.BlockSpec((tm, tn), lambda i, j: (i, j)),
                     pl.BlockSpec((None, 1, tn), lambda i, j: (row(i), 0, gate_chunk * nj + j))]
        args += [h, mod3]
    elif head_norm is not None:
        epilogue = "head_norm"
        w_cols, rope = head_norm
        in_specs.append(pl.BlockSpec((1, tn), lambda i, j: (0, j)))
        args.append(w_cols)
        if rope is not None:
            epilogue = "head_norm_rope"
            cos, sin, seq, n_lat = rope
            per_seq, lat_tiles = seq // tm, n_lat // tm
            tab = pl.BlockSpec((tm, HEAD_DIM), lambda i, j: (jnp.where(i < lat_tiles, i % per_seq, per_seq), 0))
            in_specs += [tab, tab]
            args += [cos, sin]
    return pl.pallas_call(
        functools.partial(_matmul_kernel, n_in=len(xs), epilogue=epilogue),
        grid=(n_rows // tm, n_out // tn),
        in_specs=in_specs,
        out_specs=pl.BlockSpec((tm, tn), lambda i, j: (i, j)),
        out_shape=jax.ShapeDtypeStruct((n_rows, n_out), out_dtype),
        compiler_params=_cparams("parallel", "parallel"),
        name="matmul_" + epilogue,
    )(*args)


CONV_HALO = 16


def _conv_silu_kernel(x_ref, prev_ref, next_ref, w_ref, b_ref, o_ref, pad_ref, *, per_seq, lat_blocks):
    i = pl.program_id(0)
    rows, width = o_ref.shape
    is_lat = i < lat_blocks
    has_prev = is_lat & (i % per_seq != 0)
    has_next = is_lat & (i % per_seq != per_seq - 1)
    pad_ref[pl.ds(0, CONV_HALO), :] = jnp.where(has_prev, prev_ref[...].astype(F32), 0.0)
    pad_ref[pl.ds(CONV_HALO, rows), :] = x_ref[...].astype(F32)
    pad_ref[pl.ds(CONV_HALO + rows, CONV_HALO), :] = jnp.where(has_next, next_ref[...].astype(F32), 0.0)
    acc = jnp.broadcast_to(b_ref[...], (rows, width))
    for k in range(SSD_CONV):
        acc = acc + w_ref[pl.ds(k, 1), :] * pad_ref[pl.ds(CONV_HALO - SSD_CONV // 2 + k, rows), :]
    o_ref[...] = (acc * jax.nn.sigmoid(acc)).astype(o_ref.dtype)


def conv_silu(p_zx, conv_w, conv_b, n_lat, seq, ctx_len):
    t, tc = ROW_BLOCK, 512
    assert ctx_len == t and seq % t == 0
    n_rows = p_zx.shape[0]
    col0 = SSD_INNER // tc
    halo_per_block = t // CONV_HALO
    last_halo = n_rows // CONV_HALO - 1
    return pl.pallas_call(
        functools.partial(_conv_silu_kernel, per_seq=seq // t, lat_blocks=n_lat // t),
        grid=(n_rows // t, SSD_XBC // tc),
        in_specs=[pl.BlockSpec((t, tc), lambda i, j: (i, col0 + j)),
                  pl.BlockSpec((CONV_HALO, tc), lambda i, j: (jnp.maximum(i * halo_per_block - 1, 0), col0 + j)),
                  pl.BlockSpec((CONV_HALO, tc),
                               lambda i, j: (jnp.minimum((i + 1) * halo_per_block, last_halo), col0 + j)),
                  pl.BlockSpec((SSD_CONV, tc), lambda i, j: (0, j)),
                  pl.BlockSpec((1, tc), lambda i, j: (0, j))],
        out_specs=pl.BlockSpec((t, tc), lambda i, j: (i, j)),
        out_shape=jax.ShapeDtypeStruct((n_rows, SSD_XBC), BF16),
        scratch_shapes=[pltpu.VMEM((t + 2 * CONV_HALO, tc), F32)],
        compiler_params=_cparams("parallel", "parallel"),
        name="conv_silu",
    )(p_zx, p_zx, p_zx, conv_w, conv_b.reshape(1, SSD_XBC))


def _ssd_dt_kernel(p_ref, bias_ref, a_ref, dt_ref, acs_ref, acst_ref):
    n = p_ref.shape[0]
    pre = p_ref[...] + bias_ref[...]
    dt = jnp.maximum(pre, 0.0) + jnp.log1p(jnp.exp(-jnp.abs(pre)))
    a = dt * a_ref[...]
    li = lax.broadcasted_iota(jnp.int32, (n, n), 0)
    si = lax.broadcasted_iota(jnp.int32, (n, n), 1)
    lower = (si <= li).astype(F32)
    upper = (si >= li).astype(F32)
    fwd = jnp.dot(lower, a, precision=HIGHEST, preferred_element_type=F32)
    bwd = jnp.dot(upper, a, precision=HIGHEST, preferred_element_type=F32)
    lane = lax.broadcasted_iota(jnp.int32, fwd.shape, 1)
    acs = jnp.where(lane < SSD_HEADS, fwd, bwd)
    dt_ref[...] = dt
    acs_ref[...] = acs
    acst_ref[...] = acs.T


def ssd_dt(p_dt, dt_bias, a_log):
    rows = p_dt.shape[0]
    pad = LANES - 2 * SSD_HEADS
    bias = jnp.pad(dt_bias.reshape(1, 2 * SSD_HEADS), ((0, 0), (0, pad)))
    a_neg = jnp.pad(-jnp.exp(a_log.astype(F32)).reshape(1, 2 * SSD_HEADS), ((0, 0), (0, pad)))
    c = SSD_CHUNK
    return pl.pallas_call(
        _ssd_dt_kernel,
        grid=(rows // c,),
        in_specs=[pl.BlockSpec((c, LANES), lambda i: (i, 0)),
                  pl.BlockSpec((1, LANES), lambda i: (0, 0)),
                  pl.BlockSpec((1, LANES), lambda i: (0, 0))],
        out_specs=[pl.BlockSpec((c, LANES), lambda i: (i, 0)),
                   pl.BlockSpec((c, LANES), lambda i: (i, 0)),
                   pl.BlockSpec((LANES, c), lambda i: (0, i))],
        out_shape=[jax.ShapeDtypeStruct((rows, LANES), F32),
                   jax.ShapeDtypeStruct((rows, LANES), F32),
                   jax.ShapeDtypeStruct((LANES, rows), F32)],
        compiler_params=_cparams("parallel"),
        name="ssd_dt",
    )(p_dt, bias, a_neg)


def _ssd_scan_kernel(*refs):
    fwd, bwd, outs = refs[:6], refs[6:12], refs[12:]

    @pl.when(pl.program_id(2) == 0)
    def _():
        for state_ref in outs[2:]:
            state_ref[...] = jnp.zeros_like(state_ref)

    _ssd_chunk(*fwd, outs[0], outs[2], reverse=False)
    _ssd_chunk(*bwd, outs[1], outs[3], reverse=True)


def _ssd_chunk(x_ref, b_ref, c_ref, dt_ref, acs_ref, acsr_ref, y_ref, state_ref, *, reverse):
    n = x_ref.shape[0]
    heads = dt_ref.shape[1]
    pair_w = 2 * SSD_HEAD_DIM
    x = x_ref[...].astype(F32)
    bm = b_ref[...]
    cm = c_ref[...]
    dt = dt_ref[...]
    acs = acs_ref[...]
    acs_row = acsr_ref[...]
    end = 0 if reverse else n - 1
    total = acs[end:end + 1, :]
    e_tot = jnp.exp(total)

    li = lax.broadcasted_iota(jnp.int32, (n, n), 0)
    si = lax.broadcasted_iota(jnp.int32, (n, n), 1)
    visible = (si >= li) if reverse else (si <= li)
    lo = lax.broadcasted_iota(jnp.int32, (1, pair_w), 1) < SSD_HEAD_DIM

    cb = lax.dot_general(cm, bm, (((1,), (1,)), ((), ())), preferred_element_type=F32)
    state = state_ref[...]
    y_off = jnp.dot(cm, state.astype(BF16), preferred_element_type=F32)

    xd_parts, decay_parts = [], []
    for i in range(heads // 2):
        j0, j1 = 2 * i, 2 * i + 1
        cols = slice(i * pair_w, (i + 1) * pair_w)
        xp = x[:, cols] * jnp.where(lo, dt[:, j0:j0 + 1], dt[:, j1:j1 + 1])
        ms, xs, acs_b = [], [], []
        for j, keep in ((j0, lo), (j1, jnp.logical_not(lo))):
            acs_b.append(jnp.broadcast_to(acs[:, j:j + 1], (n, n)))
            diff = acs_b[-1] - acs_row[j:j + 1, :]
            ms.append((cb * jnp.exp(jnp.where(visible, diff, MASK_VALUE))).astype(BF16))
            xs.append(jnp.where(keep, xp, 0.0).astype(BF16))
        y_diag = jnp.dot(jnp.concatenate(ms, axis=1), jnp.concatenate(xs, axis=0), preferred_element_type=F32)
        acs_pair = jnp.where(lo, acs_b[0], acs_b[1])
        tot_pair = jnp.where(lo, total[:, j0:j0 + 1], total[:, j1:j1 + 1])
        y_ref[:, cols] = (y_diag + y_off[:, cols] * jnp.exp(acs_pair)).astype(y_ref.dtype)
        xd_parts.append((xp * jnp.exp(tot_pair - acs_pair)).astype(BF16))
        decay_parts.append(jnp.where(lo, e_tot[:, j0:j0 + 1], e_tot[:, j1:j1 + 1]))
    xd = jnp.concatenate(xd_parts, axis=1)
    decay = jnp.concatenate(decay_parts, axis=1)
    bt = bm.astype(F32).T.astype(BF16)
    state_ref[...] = state * decay + jnp.dot(bt, xd, preferred_element_type=F32)


def ssd_scan(xbc, dt_col, acs_col, acs_row, n_batch, seq, ctx_len):
    c = SSD_CHUNK
    lat_chunks, ctx_chunks = seq // c, ctx_len // c
    ctx_base = n_batch * lat_chunks
    n_steps = lat_chunks + ctx_chunks
    gw = SSD_GROUP_W
    heads = SSD_HEADS // SSD_GROUPS
    b_col0 = SSD_INNER // SSD_STATE
    c_col0 = (SSD_INNER + SSD_GN) // SSD_STATE

    def specs(reverse):
        def chunk(b, k):
            kc = (ctx_chunks - 1 - k) if reverse else k
            kl = (n_steps - 1 - k) if reverse else (k - ctx_chunks)
            return jnp.where(k < ctx_chunks, ctx_base + b * ctx_chunks + kc, b * lat_chunks + kl)

        d = int(reverse)
        ins = [pl.BlockSpec((c, gw), lambda b, g, k: (chunk(b, k), g)),
               pl.BlockSpec((c, SSD_STATE), lambda b, g, k: (chunk(b, k), b_col0 + g)),
               pl.BlockSpec((c, SSD_STATE), lambda b, g, k: (chunk(b, k), c_col0 + g)),
               pl.BlockSpec((None, None, c, heads), lambda b, g, k: (d, g, chunk(b, k), 0)),
               pl.BlockSpec((None, None, c, heads), lambda b, g, k: (d, g, chunk(b, k), 0)),
               pl.BlockSpec((None, None, heads, c), lambda b, g, k: (d, g, 0, chunk(b, k)))]
        return ins, pl.BlockSpec((c, gw), lambda b, g, k: (chunk(b, k), g))

    (in_f, out_f), (in_b, out_b) = specs(False), specs(True)
    y_shape = jax.ShapeDtypeStruct((xbc.shape[0], SSD_INNER), BF16)
    args = (xbc, xbc, xbc, dt_col, acs_col, acs_row)
    return pl.pallas_call(
        _ssd_scan_kernel,
        grid=(n_batch, SSD_GROUPS, n_steps),
        in_specs=in_f + in_b,
        out_specs=[out_f, out_b],
        out_shape=[y_shape, y_shape],
        scratch_shapes=[pltpu.VMEM((SSD_STATE, gw), F32), pltpu.VMEM((SSD_STATE, gw), F32)],
        compiler_params=_cparams("parallel", "parallel", "arbitrary"),
        name="ssd_scan",
    )(*args, *args)


def _ssd_gate_kernel(yf_ref, yb_ref, x_ref, z_ref, d_ref, w_ref, o_ref):
    z = z_ref[...].astype(F32)
    y = yf_ref[...].astype(F32) + yb_ref[...].astype(F32) + d_ref[...] * x_ref[...].astype(F32)
    g = y * (z * jax.nn.sigmoid(z))
    o = g * lax.rsqrt(jnp.mean(g * g, axis=-1, keepdims=True) + EPS) * w_ref[...]
    o_ref[...] = o.astype(o_ref.dtype)


def ssd_gate(y_f, y_b, xbc, p_zx, d_skip, norm_w):
    rows = y_f.shape[0]
    t, gw = ROW_BLOCK, SSD_GROUP_W
    d_cols = jnp.repeat(d_skip.astype(F32), SSD_HEAD_DIM).reshape(1, SSD_INNER)
    spec = pl.BlockSpec((t, gw), lambda i, g: (i, g))
    vec = pl.BlockSpec((1, gw), lambda i, g: (0, g))
    return pl.pallas_call(
        _ssd_gate_kernel,
        grid=(rows // t, SSD_GROUPS),
        in_specs=[spec, spec, spec, spec, vec, vec],
        out_specs=spec,
        out_shape=jax.ShapeDtypeStruct((rows, SSD_INNER), BF16),
        compiler_params=_cparams("parallel", "parallel"),
        name="ssd_gate",
    )(y_f, y_b, xbc, p_zx, d_cols, norm_w.reshape(1, SSD_INNER))


ROPE_IDENTITY_ROWS = 1024


def _rope_tables(seq):
    t = jnp.arange(seq, dtype=jnp.int32)
    row = (t // GRID_W).astype(F32)
    col = (t % GRID_W).astype(F32)
    axis_dim = HEAD_DIM // 2
    inv = 1.0 / (ROPE_THETA ** (jnp.arange(0, axis_dim, 2, dtype=F32) / axis_dim))
    ang = jnp.concatenate([row[:, None] * inv[None], col[:, None] * inv[None]], axis=-1)
    cos = jnp.repeat(jnp.cos(ang), 2, axis=-1)
    sin = jnp.stack([-jnp.sin(ang), jnp.sin(ang)], axis=-1).reshape(seq, HEAD_DIM)
    identity = jnp.ones((ROPE_IDENTITY_ROWS, HEAD_DIM), F32)
    return jnp.concatenate([cos, identity], axis=0), jnp.concatenate([sin, 0.0 * identity], axis=0)


LOG2E = math.log2(math.e)
KEY_CHUNK = 256
DIFF_Q_TILE = 1024
GQA_Q_TILE = 512


def _scores(q, k, bias):
    s = lax.dot_general(q, k, (((1,), (1,)), ((), ())), preferred_element_type=F32)
    return s if bias is None else s + bias


def _attend_two_pass(q, pieces):
    scores = [_scores(q, k, bias) for k, _, bias in pieces]
    m = functools.reduce(jnp.maximum, [jnp.max(s, axis=-1, keepdims=True) for s in scores])
    denom = acc = None
    for s, (_, v, _) in zip(scores, pieces):
        p = jnp.exp2(s - m)
        p_sum = jnp.sum(p, axis=-1, keepdims=True)
        pv = jnp.dot(p.astype(v.dtype), v, preferred_element_type=F32)
        denom = p_sum if denom is None else denom + p_sum
        acc = pv if acc is None else acc + pv
    return acc / denom


def _attend(q, pieces):
    chunks = []
    for k, v, bias in pieces:
        for c0 in range(0, k.shape[0], KEY_CHUNK):
            c1 = min(c0 + KEY_CHUNK, k.shape[0])
            chunks.append((k[c0:c1], v[c0:c1], None if bias is None else bias[:, c0:c1]))
    m = denom = acc = None
    for k, v, bias in chunks:
        s = lax.dot_general(q, k, (((1,), (1,)), ((), ())), preferred_element_type=F32)
        if bias is not None:
            s = s + bias
        m_chunk = jnp.max(s, axis=-1, keepdims=True)
        m_new = m_chunk if m is None else jnp.maximum(m, m_chunk)
        p = jnp.exp2(s - m_new)
        p_sum = jnp.sum(p, axis=-1, keepdims=True)
        pv = jnp.dot(p.astype(v.dtype), v, preferred_element_type=F32)
        if m is None:
            denom, acc = p_sum, pv
        else:
            alpha = jnp.exp2(m - m_new)
            denom = alpha * denom + p_sum
            acc = alpha * acc + pv
        m = m_new
    return acc / denom


NA_HEADS_PER_STEP = 4


def _na_kernel(q_ref, k0_ref, k1_ref, k2_ref, kc_ref, v0_ref, v1_ref, v2_ref, vc_ref, bias_ref, o_ref):
    t = q_ref.shape[0]
    for hd in range(NA_HEADS_PER_STEP):
        cols = slice(hd * HEAD_DIM, (hd + 1) * HEAD_DIM)
        pieces = [(k_ref[:, cols], v_ref[:, cols], bias_ref[hd, :, i * t:(i + 1) * t])
                  for i, (k_ref, v_ref) in enumerate(((k0_ref, v0_ref), (k1_ref, v1_ref), (k2_ref, v2_ref)))]
        pieces.append((kc_ref[:, cols], vc_ref[:, cols], None))
        o_ref[:, cols] = _attend_two_pass(q_ref[:, cols], pieces).astype(o_ref.dtype)


def _na_bias_table(rpb, rows):
    n_blocks = rows // NA_QROWS
    w0_of = lambda j: min(max(NA_QROWS * j - NA_WIN_ROWS // 2, 0), rows - NA_KROWS)
    geoms, pattern_of = [], []
    for j in range(n_blocks):
        r0s = tuple(min(max(NA_QROWS * j + a - NA_WIN_ROWS // 2, 0), rows - NA_WIN_ROWS) - w0_of(j)
                    for a in range(NA_QROWS))
        key = (NA_QROWS * j - w0_of(j), r0s)
        if key not in geoms:
            geoms.append(key)
        pattern_of.append(geoms.index(key))
    qc = jnp.arange(GRID_W)[:, None]
    kc = jnp.arange(GRID_W)[None, :]
    cs = jnp.clip(qc - NA_WIN_COLS // 2, 0, GRID_W - NA_WIN_COLS)
    col_ok = (kc >= cs) & (kc < cs + NA_WIN_COLS)
    dc = jnp.clip(kc - qc + NA_WIN_COLS - 1, 0, 2 * NA_WIN_COLS - 2)
    pick = (dc[None] == jnp.arange(2 * NA_WIN_COLS - 1)[:, None, None]).astype(F32)
    tiles = jnp.einsum('hrd,dqk->hrqk', rpb.astype(F32), pick, precision=HIGHEST)
    tiles = jnp.where(col_ok, tiles * LOG2E, MASK_VALUE)
    masked = jnp.full((rpb.shape[0], GRID_W, GRID_W), MASK_VALUE, F32)
    tables = []
    for q_row0, r0s in geoms:
        tile_rows = []
        for qi in range(NA_QROWS):
            row = [tiles[:, ki - (q_row0 + qi) + NA_WIN_ROWS - 1] if r0s[qi] <= ki < r0s[qi] + NA_WIN_ROWS
                   else masked for ki in range(NA_KROWS)]
            tile_rows.append(jnp.concatenate(row, axis=2))
        tables.append(jnp.concatenate(tile_rows, axis=1))
    tables.append(jnp.full_like(tables[0], MASK_VALUE))
    pattern_of.append(len(geoms))
    w0_blocks = [w0_of(j) // NA_QROWS for j in range(n_blocks)] + [0]
    return jnp.stack(tables, axis=1), pattern_of, w0_blocks


def _lookup(values, j):
    out = jnp.int32(values[-1])
    for idx in range(len(values) - 2, -1, -1):
        out = jnp.where(j == idx, jnp.int32(values[idx]), out)
    return out


def na_attention(qn, kn, p_na, v_col_off, rpb, n_batch, seq, ctx_len):
    t = NA_QROWS * GRID_W
    assert t == ROW_BLOCK == ctx_len
    rows = seq // GRID_W
    n_blocks = seq // t
    bias, pattern_of, w0_blocks = _na_bias_table(rpb, rows)
    hw = NA_HEADS_PER_STEP * HEAD_DIM
    v_col0 = v_col_off // hw
    ctx_base = n_batch * n_blocks

    def q_blk(j, b):
        return jnp.where(j < n_blocks, b * n_blocks + j, ctx_base + b)

    def kv_blk(piece):
        return lambda h, j, b: (b * n_blocks + _lookup(w0_blocks, j) + piece, h)

    def kv_blk_v(piece):
        return lambda h, j, b: (b * n_blocks + _lookup(w0_blocks, j) + piece, v_col0 + h)

    blk = lambda fn: pl.BlockSpec((t, hw), fn)
    in_specs = ([blk(lambda h, j, b: (q_blk(j, b), h))]
                + [blk(kv_blk(i)) for i in range(3)] + [blk(lambda h, j, b: (ctx_base + b, h))]
                + [blk(kv_blk_v(i)) for i in range(3)] + [blk(lambda h, j, b: (ctx_base + b, v_col0 + h))]
                + [pl.BlockSpec((NA_HEADS_PER_STEP, None, t, 3 * t),
                                lambda h, j, b: (h, _lookup(pattern_of, j), 0, 0))])
    return pl.pallas_call(
        _na_kernel,
        grid=(NA_HEADS // NA_HEADS_PER_STEP, n_blocks + 1, n_batch),
        in_specs=in_specs,
        out_specs=blk(lambda h, j, b: (q_blk(j, b), h)),
        out_shape=jax.ShapeDtypeStruct((qn.shape[0], NA_WIDTH), BF16),
        compiler_params=_cparams("parallel", "parallel", "parallel"),
        name="na_attention",
    )(qn, kn, kn, kn, kn, p_na, p_na, p_na, p_na, bias)


def _gqa_kernel(q_ref, kl_ref, kc_ref, vl_ref, vc_ref, o_ref):
    t = q_ref.shape[0]
    q = jnp.concatenate([q_ref[:, g * HEAD_DIM:(g + 1) * HEAD_DIM] for g in range(GQA_GROUP)], axis=0)
    o = _attend(q, [(kc_ref[...], vc_ref[...], None), (kl_ref[...], vl_ref[...], None)])
    for g in range(GQA_GROUP):
        o_ref[:, g * HEAD_DIM:(g + 1) * HEAD_DIM] = o[g * t:(g + 1) * t].astype(o_ref.dtype)


def gqa_attention(qn, kn, p_kv, v_col_off, n_batch, seq, ctx_len):
    t = GQA_Q_TILE
    q_tiles = seq // t
    ctx_base = n_batch * seq // ctx_len
    v0 = v_col_off // HEAD_DIM
    gw = GQA_GROUP * HEAD_DIM
    return pl.pallas_call(
        _gqa_kernel,
        grid=(n_batch, GQA_KV_HEADS, q_tiles),
        in_specs=[pl.BlockSpec((t, gw), lambda b, h, i: (b * q_tiles + i, h)),
                  pl.BlockSpec((seq, HEAD_DIM), lambda b, h, i: (b, h)),
                  pl.BlockSpec((ctx_len, HEAD_DIM), lambda b, h, i: (ctx_base + b, h)),
                  pl.BlockSpec((seq, HEAD_DIM), lambda b, h, i: (b, v0 + h)),
                  pl.BlockSpec((ctx_len, HEAD_DIM), lambda b, h, i: (ctx_base + b, v0 + h))],
        out_specs=pl.BlockSpec((t, gw), lambda b, h, i: (b * q_tiles + i, h)),
        out_shape=jax.ShapeDtypeStruct((n_batch * seq, GQA_Q), BF16),
        compiler_params=_cparams("parallel", "parallel", "parallel"),
        name="gqa_attention",
    )(qn, kn, kn, p_kv, p_kv)


def _diff_kernel(lam_ref, q_ref, kl_ref, kc_ref, vl_ref, vc_ref, w_ref, o_ref, *, out_scale):
    outs = []
    for comp in range(2):
        cols = slice(comp * HEAD_DIM, (comp + 1) * HEAD_DIM)
        outs.append(_attend(q_ref[:, cols], [(kc_ref[:, cols], vc_ref[...], None),
                                              (kl_ref[:, cols], vl_ref[...], None)]))
    o = outs[0] - lam_ref[0] * outs[1]
    o = o * lax.rsqrt(jnp.mean(o * o, axis=-1, keepdims=True) + EPS) * w_ref[...]
    o_ref[...] = (o * out_scale).astype(o_ref.dtype)


def diff_attention(qn, q_col_off, kn, p_kv, v_col_off, lam, subln_w, out_scale, n_batch, seq, ctx_len):
    t = DIFF_Q_TILE
    q_tiles = seq // t
    ctx_base = n_batch * seq // ctx_len
    v0 = v_col_off // DIFF_V_DIM
    w2 = 2 * HEAD_DIM
    q0 = q_col_off // w2
    return pl.pallas_call(
        functools.partial(_diff_kernel, out_scale=out_scale),
        grid=(n_batch, DIFF_HEADS, q_tiles),
        in_specs=[pl.BlockSpec(memory_space=pltpu.SMEM),
                  pl.BlockSpec((t, w2), lambda b, h, i: (b * q_tiles + i, q0 + h)),
                  pl.BlockSpec((seq, w2), lambda b, h, i: (b, h)),
                  pl.BlockSpec((ctx_len, w2), lambda b, h, i: (ctx_base + b, h)),
                  pl.BlockSpec((seq, DIFF_V_DIM), lambda b, h, i: (b, v0 + h)),
                  pl.BlockSpec((ctx_len, DIFF_V_DIM), lambda b, h, i: (ctx_base + b, v0 + h)),
                  pl.BlockSpec((1, DIFF_V_DIM), lambda b, h, i: (0, 0))],
        out_specs=pl.BlockSpec((t, DIFF_V_DIM), lambda b, h, i: (b * q_tiles + i, h)),
        out_shape=jax.ShapeDtypeStruct((n_batch * seq, DIFF_V), BF16),
        compiler_params=_cparams("parallel", "parallel", "parallel"),
        name="diff_attention",
    )(lam.reshape(1), qn, kn, kn, p_kv, p_kv, subln_w.reshape(1, DIFF_V_DIM))


def _moe_plan_kernel(route_ref, pos_ref, pad_end_ref, slot_ref, counts_ref, start_ref):
    phase, i = pl.program_id(0), pl.program_id(1)
    t = route_ref.shape[0]
    route = route_ref[...]
    lane = lax.broadcasted_iota(jnp.int32, route.shape, 1)
    chosen = [lane == route[:, ROUTE_EXPERT_LANE + k:ROUTE_EXPERT_LANE + k + 1].astype(jnp.int32)
              for k in range(MOE_TOPK)]
    both = functools.reduce(jnp.add, [c.astype(F32) for c in chosen])
    tile_counts = jnp.sum(both, axis=0, keepdims=True)

    @pl.when((phase == 0) & (i == 0))
    def _():
        counts_ref[...] = jnp.zeros_like(counts_ref)

    @pl.when(phase == 0)
    def _():
        counts_ref[...] += tile_counts

    @pl.when((phase == 1) & (i == 0))
    def _():
        padded = jnp.ceil(counts_ref[...] * (1.0 / MOE_BLOCK)) * MOE_BLOCK
        rows = pad_end_ref.shape[0]
        ki = lax.broadcasted_iota(jnp.int32, (LANES, LANES), 0)
        ni = lax.broadcasted_iota(jnp.int32, (LANES, LANES), 1)
        pad_end = jnp.dot(jnp.broadcast_to(padded, (rows, LANES)), (ki <= ni).astype(F32),
                          precision=HIGHEST, preferred_element_type=F32)
        pad_end_ref[...] = pad_end
        start_ref[...] = pad_end[0:1] - padded
        slot_ref[...] = jnp.zeros_like(slot_ref)

    @pl.when(phase == 1)
    def _():
        li = lax.broadcasted_iota(jnp.int32, (t, t), 0)
        si = lax.broadcasted_iota(jnp.int32, (t, t), 1)
        earlier = jnp.dot((si < li).astype(BF16), both.astype(BF16), preferred_element_type=F32)
        row_of = earlier + start_ref[...]
        out = jnp.zeros(route.shape, F32)
        token = (i * t + 1 + lax.broadcasted_iota(jnp.int32, (t, 1), 0)).astype(F32)
        blk_lane = lax.broadcasted_iota(jnp.int32, (t, slot_ref.shape[0]), 1)
        placed = jnp.zeros(slot_ref.shape, F32)
        for k in range(MOE_TOPK):
            p = jnp.sum(jnp.where(chosen[k], row_of, 0.0), axis=-1, keepdims=True)
            out = jnp.where(lane == k, p, out)
            blk = jnp.floor(p * (1.0 / MOE_BLOCK))
            in_blk = (p - blk * MOE_BLOCK).astype(jnp.int32)
            block_hot = (blk_lane == blk.astype(jnp.int32)).astype(F32)
            row_hot = jnp.where(lane == in_blk, token, 0.0)
            placed = placed + jnp.dot(block_hot.T, row_hot, precision=HIGHEST, preferred_element_type=F32)
        slot_ref[...] += placed
        pos_ref[...] = out.astype(jnp.int32)
        start_ref[...] += tile_counts


def moe_plan(route, n_blocks):
    n = route.shape[0]
    t = ROW_BLOCK
    blocks_pad = -(-n_blocks // LANES) * LANES
    return pl.pallas_call(
        _moe_plan_kernel,
        grid=(2, n // t),
        in_specs=[pl.BlockSpec((t, LANES), lambda p, i: (i, 0))],
        out_specs=[pl.BlockSpec((t, LANES), lambda p, i: (i * p, 0)),
                   pl.BlockSpec((8, LANES), lambda p, i: (0, 0)),
                   pl.BlockSpec((blocks_pad, MOE_BLOCK), lambda p, i: (0, 0))],
        out_shape=[jax.ShapeDtypeStruct((n, LANES), jnp.int32),
                   jax.ShapeDtypeStruct((8, LANES), F32),
                   jax.ShapeDtypeStruct((blocks_pad, MOE_BLOCK), F32)],
        scratch_shapes=[pltpu.VMEM((1, LANES), F32), pltpu.VMEM((1, LANES), F32)],
        compiler_params=_cparams("arbitrary", "arbitrary"),
        name="moe_plan",
    )(route)


MOE_DMA_CHUNKS = 4


def _moe_kernel(be_ref, next_ref, seg_ref, nused_ref, x_ref, w1_hbm, w3_hbm, w2_hbm, o_ref,
                s1, s3, s2, w1b, w3b, w2b, sems, *, layer):
    i = pl.program_id(0)
    slot = seg_ref[i] % 2

    def weight_copies(e, slot):
        copies = []
        for n, (w_hbm, stage) in enumerate(((w1_hbm, s1), (w3_hbm, s3), (w2_hbm, s2))):
            rows = stage.shape[1] // MOE_DMA_CHUNKS
            for c in range(MOE_DMA_CHUNKS):
                slab = pl.ds(c * rows, rows)
                copies.append(pltpu.make_async_copy(w_hbm.at[layer, e, slab], stage.at[slot, slab],
                                                    sems.at[slot, n, c]))
        return copies

    @pl.when(i == 0)
    def _():
        for c in weight_copies(be_ref[0], slot):
            c.start()

    prev = be_ref[jnp.maximum(i - 1, 0)]

    @pl.when((i == 0) | (be_ref[i] != prev))
    def _():
        for c in weight_copies(be_ref[i], slot):
            c.wait()

        @pl.when(next_ref[i] >= 0)
        def _():
            for c in weight_copies(next_ref[i], 1 - slot):
                c.start()

        w1b[...] = s1[slot].astype(BF16)
        w3b[...] = s3[slot].astype(BF16)
        w2b[...] = s2[slot].astype(BF16)

    @pl.when(i < nused_ref[0])
    def _():
        x = x_ref[...].astype(BF16)
        a = jnp.dot(x, w1b[...], preferred_element_type=F32)
        b = jnp.dot(x, w3b[...], preferred_element_type=F32)
        hid = (a * jax.nn.sigmoid(a)) * b
        o_ref[...] = jnp.dot(hid.astype(BF16), w2b[...], preferred_element_type=F32)

    @pl.when(i >= nused_ref[0])
    def _():
        o_ref[...] = jnp.zeros_like(o_ref)


def moe_experts(x_sorted, block_e, next_e, n_used, w1, w3, w2, layer):
    n_blocks = block_e.shape[0]
    d, hid = w1.shape[2], w1.shape[3]
    hbm = pl.BlockSpec(memory_space=pl.ANY)
    changed = jnp.concatenate([jnp.zeros((1,), jnp.int32), (block_e[1:] != block_e[:-1]).astype(jnp.int32)])
    segment = jnp.cumsum(changed, dtype=jnp.int32)
    row_spec = pl.BlockSpec((MOE_BLOCK, d), lambda i, be, nx, sg, nu: (i, 0))
    x_spec = pl.BlockSpec((MOE_BLOCK, d), lambda i, be, nx, sg, nu: (jnp.minimum(i, nu[0] - 1), 0))
    grid_spec = pltpu.PrefetchScalarGridSpec(
        num_scalar_prefetch=4,
        grid=(n_blocks,),
        in_specs=[x_spec, hbm, hbm, hbm],
        out_specs=row_spec,
        scratch_shapes=[pltpu.VMEM((2, d, hid), F32), pltpu.VMEM((2, d, hid), F32), pltpu.VMEM((2, hid, d), F32),
                        pltpu.VMEM((d, hid), BF16), pltpu.VMEM((d, hid), BF16), pltpu.VMEM((hid, d), BF16),
                        pltpu.SemaphoreType.DMA((2, 3, MOE_DMA_CHUNKS))],
    )
    return pl.pallas_call(
        functools.partial(_moe_kernel, layer=layer),
        grid_spec=grid_spec,
        out_shape=jax.ShapeDtypeStruct((n_blocks * MOE_BLOCK, d), F32),
        compiler_params=_cparams("arbitrary"),
        name="moe_experts",
    )(block_e, next_e, segment, n_used, x_sorted, w1, w3, w2)


def _combine_kernel(h_ref, g_ref, route_ref, *refs, n_y, next_norm):
    y_refs, rest = refs[:n_y], refs[n_y:]
    acc = None
    for k, y_ref in enumerate(y_refs):
        part = y_ref[...] * route_ref[:, ROUTE_GATE_LANE + k:ROUTE_GATE_LANE + k + 1]
        acc = part if acc is None else acc + part
    h = h_ref[...] + g_ref[...] * acc
    if next_norm:
        w_ref, sc_ref, sh_ref, o_ref, a_ref = rest
        y = h * lax.rsqrt(jnp.mean(h * h, axis=-1, keepdims=True) + EPS) * w_ref[...]
        a_ref[...] = (y * (1.0 + sc_ref[...]) + sh_ref[...]).astype(a_ref.dtype)
    else:
        (o_ref,) = rest
    o_ref[...] = h


def moe_combine(h, mod3, gate_chunk, route, ys, geom, next_norm=None):
    n_lat, seq, n_batch = geom
    n_rows, d = ys[0].shape
    t = ROW_BLOCK
    row = functools.partial(_mod_row, tile=t, n_lat=n_lat, seq=seq, n_batch=n_batch)
    spec = pl.BlockSpec((t, d), lambda i: (i, 0))
    in_specs = [spec, pl.BlockSpec((None, 1, d), lambda i: (row(i), 0, gate_chunk)),
                pl.BlockSpec((t, LANES), lambda i: (i, 0))] + [spec] * len(ys)
    args = [h, mod3, route, *ys]
    out_specs, out_shape = spec, jax.ShapeDtypeStruct((n_rows, d), F32)
    if next_norm is not None:
        w, mod3_next, sc_chunk, sh_chunk = next_norm
        in_specs += [pl.BlockSpec((1, d), lambda i: (0, 0)),
                     pl.BlockSpec((None, 1, d), lambda i: (row(i), 0, sc_chunk)),
                     pl.BlockSpec((None, 1, d), lambda i: (row(i), 0, sh_chunk))]
        args += [w.reshape(1, d), mod3_next, mod3_next]
        out_specs, out_shape = [spec, spec], [out_shape, jax.ShapeDtypeStruct((n_rows, d), BF16)]
    return pl.pallas_call(
        functools.partial(_combine_kernel, n_y=len(ys), next_norm=next_norm is not None),
        grid=(n_rows // t,),
        in_specs=in_specs, out_specs=out_specs, out_shape=out_shape,
        compiler_params=_cparams("parallel"),
        name="moe_combine",
    )(*args)


def moe_layer(h, f, route, mod3, gate_chunk, w1, w3, w2, layer, geom, next_norm=None):
    n = f.shape[0]
    n_assign = n * MOE_TOPK
    n_blocks = -(-n_assign // MOE_BLOCK) + MOE_EXPERTS
    pos, pad_end, slot = moe_plan(route, n_blocks)
    pos = pos[:, :MOE_TOPK]
    pad_end = pad_end[0, :MOE_EXPERTS].astype(jnp.int32)
    n_used = pad_end[-1] // MOE_BLOCK
    blk = jnp.minimum(jnp.arange(n_blocks, dtype=jnp.int32), n_used - 1) * MOE_BLOCK
    block_e = jnp.minimum(jnp.sum(pad_end[None, :] <= blk[:, None], axis=1, dtype=jnp.int32), MOE_EXPERTS - 1)
    seg_end = pad_end[block_e] // MOE_BLOCK
    next_e = jnp.where(seg_end < n_used, block_e[jnp.minimum(seg_end, n_blocks - 1)], -1)
    slot = slot[:n_blocks].reshape(-1).astype(jnp.int32)
    slot_tok = jnp.where(slot > 0, slot - 1, jnp.arange(n_blocks * MOE_BLOCK, dtype=jnp.int32) % n)
    x_sorted = f.at[slot_tok].get(mode='promise_in_bounds')
    yb = moe_experts(x_sorted, block_e, next_e, n_used.reshape(1), w1, w3, w2, layer)
    ys = [yb.at[pos[:, k]].get(mode='promise_in_bounds') for k in range(MOE_TOPK)]
    return moe_combine(h, mod3, gate_chunk, route, ys, geom, next_norm)


def _even_layer_mix(a, in_w, conv_w, conv_b, dt_bias, a_log, d_skip, ssd_norm_w, q_norm, k_norm, rpb,
                    n_batch, seq, ctx_len):
    n_rows = a.shape[0]
    n_lat = n_batch * seq
    zx_end = SSD_INNER + SSD_XBC
    dt_end = zx_end + 2 * SSD_HEADS
    w = jnp.concatenate([in_w[:, :zx_end], in_w[:, dt_end:],
                         jnp.pad(in_w[:, zx_end:dt_end], ((0, 0), (0, LANES - 2 * SSD_HEADS)))],
                        axis=1).astype(BF16)
    na_off = lambda i: zx_end + i * NA_WIDTH
    dt_off = zx_end + 3 * NA_WIDTH
    tile_w = lambda v, s: (jnp.tile(v.astype(F32), NA_HEADS) * s).reshape(1, NA_WIDTH)
    p_zx = matmul([a], [(w, 0, 0)], n_rows, zx_end, BF16, tn=512)
    p_dt = matmul([a], [(w, 0, dt_off)], n_rows, LANES, F32, tn=LANES)
    qn = matmul([a], [(w, 0, na_off(0))], n_rows, NA_WIDTH, BF16, tn=512,
                head_norm=(tile_w(q_norm, HEAD_DIM ** -0.5 * LOG2E), None))
    kn = matmul([a], [(w, 0, na_off(1))], n_rows, NA_WIDTH, BF16, tn=512,
                head_norm=(tile_w(k_norm, 1.0), None))
    p_v = matmul([a], [(w, 0, na_off(2))], n_rows, NA_WIDTH, BF16, tn=512)

    xbc = conv_silu(p_zx, conv_w, conv_b, n_lat, seq, ctx_len)
    dt, acs, acs_t = ssd_dt(p_dt, dt_bias, a_log)
    hg = SSD_HEADS // SSD_GROUPS
    col = lambda v: v[:, :2 * SSD_HEADS].reshape(n_rows, 2, SSD_GROUPS, hg).transpose(1, 2, 0, 3)
    dt_col, acs_col = col(dt), col(acs)
    acs_row = acs_t[:2 * SSD_HEADS].reshape(2, SSD_GROUPS, hg, n_rows)
    y_f, y_b = ssd_scan(xbc, dt_col, acs_col, acs_row, n_batch, seq, ctx_len)
    s_out = ssd_gate(y_f, y_b, xbc, p_zx, d_skip, ssd_norm_w)

    n_out = na_attention(qn, kn, p_v, 0, rpb, n_batch, seq, ctx_len)
    return s_out, n_out


def _odd_layer_mix(a, in_w, gq_norm, gk_norm, dq_norm, dk_norm, lam_vecs, subln_w, lambda_init,
                   n_batch, seq, ctx_len):
    n_rows = a.shape[0]
    n_lat = n_batch * seq
    cos, sin = _rope_tables(seq)
    rope = (cos, sin, seq, n_lat)
    scale = HEAD_DIM ** -0.5 * LOG2E
    rep = lambda w, n: jnp.tile(w.astype(F32), n).reshape(1, n * HEAD_DIM)
    w = in_w.astype(BF16)
    gk_off, gv_off, dk_off, dv_off = ODD_Q, ODD_Q + GQA_KV, ODD_Q + 2 * GQA_KV, ODD_Q + 2 * GQA_KV + DIFF_QK
    q_w = jnp.concatenate([rep(gq_norm, GQA_HEADS), rep(dq_norm, 2 * DIFF_HEADS)], axis=1) * scale
    qn = matmul([a], [(w, 0, 0)], n_lat, ODD_Q, BF16, tn=512, head_norm=(q_w, rope))
    gkn = matmul([a], [(w, 0, gk_off)], n_rows, GQA_KV, BF16, tn=512,
                 head_norm=(rep(gk_norm, GQA_KV_HEADS), rope))
    dkn = matmul([a], [(w, 0, dk_off)], n_rows, DIFF_QK, BF16, tn=512,
                 head_norm=(rep(dk_norm, 2 * DIFF_HEADS), rope))
    gv = matmul([a], [(w, 0, gv_off)], n_rows, GQA_KV, BF16, tn=512)
    dv = matmul([a], [(w, 0, dv_off)], n_rows, DIFF_V, BF16, tn=512)
    lv = lam_vecs.astype(F32)
    lam = jnp.exp(jnp.dot(lv[0], lv[1])) - jnp.exp(jnp.dot(lv[2], lv[3])) + lambda_init
    og = gqa_attention(qn, gkn, gv, 0, n_batch, seq, ctx_len)
    od = diff_attention(qn, GQA_Q, dkn, dv, 0, lam, subln_w, 1.0 - lambda_init, n_batch, seq, ctx_len)
    return og, od


def kernel(x, c, ctx, c_ctx, ada_w, ada_b, norm_w, ev_in_w, ev_conv_w, ev_conv_b, ev_dt_bias, ev_a_log, ev_d_skip, ev_ssd_norm_w, ev_na_q_norm, ev_na_k_norm, ev_na_rpb, ev_out_w, od_in_w, od_gqa_q_norm, od_gqa_k_norm, od_diff_q_norm, od_diff_k_norm, od_lambda, od_diff_subln, od_out_w, moe_group_w, moe_expert_w, moe_w1, moe_w3, moe_w2):
    n_batch, seq, d = x.shape
    ctx_len = ctx.shape[1]
    depth = ada_w.shape[0]
    assert depth == 2 and d == D_MODEL
    n_lat, n_ctx = n_batch * seq, n_batch * ctx_len
    n_all = n_lat + n_ctx
    geom = (n_lat, seq, n_batch)

    cc = jnp.concatenate([c, c_ctx[None, :], jnp.zeros((8 - n_batch - 1, d), F32)], axis=0)
    mod = ada_modulation(cc, ada_w, ada_b)
    router_w = lambda l: jnp.pad(jnp.concatenate([moe_group_w[l], moe_expert_w[l]], axis=1),
                                 ((0, 0), (0, LANES - MOE_GROUPS - MOE_EXPERTS)))

    h = jnp.concatenate([x.reshape(n_lat, d), ctx.reshape(n_ctx, d)], axis=0)

    mod3 = mod[0].reshape(8, 1, 6 * d)
    a = norm_modulate(h, norm_w[0, 0], mod3, 1, 0, n_all, geom)
    s_out, n_out = _even_layer_mix(a, ev_in_w[0], ev_conv_w[0], ev_conv_b[0], ev_dt_bias[0], ev_a_log[0],
                                   ev_d_skip[0], ev_ssd_norm_w[0], ev_na_q_norm[0], ev_na_k_norm[0],
                                   ev_na_rpb[0], n_batch, seq, ctx_len)
    out_w = ev_out_w[0].astype(BF16)
    h = matmul([s_out, n_out], [(out_w, 0, 0), (out_w, SSD_INNER, 0)], n_all, d, F32, tn=512,
               residual=(h, mod3, 2, geom))
    f, route = norm_modulate(h, norm_w[0, 1], mod3, 4, 3, n_all, geom, router_w=router_w(0))
    mod3_next = mod[1].reshape(8, 1, 6 * d)
    h, a = moe_layer(h, f, route, mod3, 5, moe_w1, moe_w3, moe_w2, 0, geom,
                     next_norm=(norm_w[1, 0], mod3_next, 1, 0))

    mod3 = mod3_next
    lambda_init = 0.8 - 0.6 * math.exp(-0.3 * 1)
    og, od = _odd_layer_mix(a, od_in_w[0], od_gqa_q_norm[0], od_gqa_k_norm[0], od_diff_q_norm[0],
                            od_diff_k_norm[0], od_lambda[0], od_diff_subln[0], lambda_init,
                            n_batch, seq, ctx_len)
    out_w = od_out_w[0].astype(BF16)
    h = matmul([og, od], [(out_w, 0, 0), (out_w, GQA_Q, 0)], n_lat, d, F32, tn=512,
               residual=(h, mod3, 2, geom))
    f, route = norm_modulate(h, norm_w[1, 1], mod3, 4, 3, n_lat, geom, router_w=router_w(1))
    h = moe_layer(h, f, route, mod3, 5, moe_w1, moe_w3, moe_w2, 1, geom)
    return h.reshape(n_batch, seq, d)
```

```python
import functools
import math

import jax
import jax.numpy as jnp
from jax import lax
from jax.experimental import pallas as pl
from jax.experimental.pallas import tpu as pltpu

F32 = jnp.float32
BF16 = jnp.bfloat16
HIGHEST = lax.Precision.HIGHEST

D_MODEL = 2048
GRID_W = 64
EPS = 1e-6
ROPE_THETA = 10000.0
HEAD_DIM = 128

SSD_HEADS = 32
SSD_HEAD_DIM = 64
SSD_INNER = SSD_HEADS * SSD_HEAD_DIM
SSD_GROUPS = 4
SSD_STATE = 128
SSD_GN = SSD_GROUPS * SSD_STATE
SSD_XBC = SSD_INNER + 2 * SSD_GN
SSD_CONV = 5
SSD_CHUNK = 128
SSD_GROUP_W = SSD_INNER // SSD_GROUPS

NA_HEADS = 16
NA_WIDTH = NA_HEADS * HEAD_DIM
NA_WIN_ROWS = 8
NA_WIN_COLS = 16
NA_QROWS = 4
NA_KROWS = 12

GQA_HEADS = 16
GQA_KV_HEADS = 4
GQA_GROUP = GQA_HEADS // GQA_KV_HEADS
GQA_Q = GQA_HEADS * HEAD_DIM
GQA_KV = GQA_KV_HEADS * HEAD_DIM
DIFF_HEADS = 8
DIFF_QK = DIFF_HEADS * 2 * HEAD_DIM
DIFF_V_DIM = 2 * HEAD_DIM
DIFF_V = DIFF_HEADS * DIFF_V_DIM
ODD_Q = GQA_Q + DIFF_QK

MOE_GROUPS = 8
MOE_EXPERTS_PER_GROUP = 8
MOE_EXPERTS = MOE_GROUPS * MOE_EXPERTS_PER_GROUP
MOE_HIDDEN = 512
MOE_TOPK = 2
MOE_BLOCK = 128

ROW_BLOCK = 256
LANES = 128
MASK_VALUE = -1e30
VMEM_LIMIT = 52 * 1024 * 1024


def _cparams(*sem):
    return pltpu.CompilerParams(dimension_semantics=sem, vmem_limit_bytes=VMEM_LIMIT)


def _mod_row(i, tile, n_lat, seq, n_batch):
    return jnp.where(i < n_lat // tile, i // (seq // tile), n_batch)


def _ada_kernel(c_ref, w_ref, b_ref, o_ref):
    c = c_ref[...]
    s = c * jax.nn.sigmoid(c)
    o_ref[...] = jnp.dot(s, w_ref[...], precision=HIGHEST, preferred_element_type=F32) + b_ref[...]


def ada_modulation(cc, ada_w, ada_b):
    n_layers, d, n_out = ada_w.shape
    rows = cc.shape[0]
    tn = 1024
    return pl.pallas_call(
        _ada_kernel,
        grid=(n_layers, n_out // tn),
        in_specs=[pl.BlockSpec((rows, d), lambda l, j: (0, 0)),
                  pl.BlockSpec((None, d, tn), lambda l, j: (l, 0, j)),
                  pl.BlockSpec((None, 1, tn), lambda l, j: (l, 0, j))],
        out_specs=pl.BlockSpec((None, rows, tn), lambda l, j: (l, 0, j)),
        out_shape=jax.ShapeDtypeStruct((n_layers, rows, n_out), F32),
        compiler_params=_cparams("parallel", "parallel"),
        name="ada_modulation",
    )(cc, ada_w, ada_b.reshape(n_layers, 1, n_out))


def _normmod_kernel(h_ref, w_ref, sc_ref, sh_ref, *rest, with_router):
    x = h_ref[...]
    y = x * lax.rsqrt(jnp.mean(x * x, axis=-1, keepdims=True) + EPS) * w_ref[...]
    a = y * (1.0 + sc_ref[...]) + sh_ref[...]
    if with_router:
        rw_ref, o_ref, route_ref = rest
        logits = jnp.dot(a, rw_ref[...], precision=HIGHEST, preferred_element_type=F32)
        route_ref[...] = _route(logits)
    else:
        (o_ref,) = rest
    o_ref[...] = a.astype(o_ref.dtype)


ROUTE_EXPERT_LANE = 0
ROUTE_GATE_LANE = MOE_TOPK


def _route(logits):
    assert MOE_TOPK == 2
    neg = jnp.float32(-3.0e38)
    lane = lax.broadcasted_iota(jnp.int32, logits.shape, 1)
    is_group = lane < MOE_GROUPS
    gl = jnp.where(is_group, logits, neg)
    g_max = jnp.max(gl, axis=-1, keepdims=True)
    g_idx = jnp.min(jnp.where(gl == g_max, lane, LANES), axis=-1, keepdims=True)
    g_gate = 1.0 / jnp.sum(jnp.where(is_group, jnp.exp(logits - g_max), 0.0), axis=-1, keepdims=True)
    first = MOE_GROUPS + g_idx * MOE_EXPERTS_PER_GROUP
    in_group = (lane >= first) & (lane < first + MOE_EXPERTS_PER_GROUP)
    el = jnp.where(in_group, logits, neg)
    v1 = jnp.max(el, axis=-1, keepdims=True)
    i1 = jnp.min(jnp.where(in_group & (el == v1), lane, LANES), axis=-1, keepdims=True)
    rest = in_group & (lane != i1)
    el2 = jnp.where(rest, logits, neg)
    v2 = jnp.max(el2, axis=-1, keepdims=True)
    i2 = jnp.min(jnp.where(rest & (el2 == v2), lane, LANES), axis=-1, keepdims=True)
    e21 = jnp.exp(v2 - v1)
    p1 = 1.0 / (1.0 + e21)
    p2 = e21 / (1.0 + e21)
    vals = [(i1 - MOE_GROUPS).astype(F32), (i2 - MOE_GROUPS).astype(F32), p1 * g_gate, p2 * g_gate]
    out = jnp.zeros(logits.shape, F32)
    for k, v in enumerate(vals):
        out = jnp.where(lane == k, v, out)
    return out


def norm_modulate(h, w, mod3, sc_chunk, sh_chunk, n_rows, geom, router_w=None):
    n_lat, seq, n_batch = geom
    d = h.shape[1]
    t = ROW_BLOCK
    row = functools.partial(_mod_row, tile=t, n_lat=n_lat, seq=seq, n_batch=n_batch)
    in_specs = [pl.BlockSpec((t, d), lambda i: (i, 0)),
                pl.BlockSpec((1, d), lambda i: (0, 0)),
                pl.BlockSpec((None, 1, d), lambda i: (row(i), 0, sc_chunk)),
                pl.BlockSpec((None, 1, d), lambda i: (row(i), 0, sh_chunk))]
    args = [h, w.reshape(1, d), mod3, mod3]
    out_specs = pl.BlockSpec((t, d), lambda i: (i, 0))
    out_shape = jax.ShapeDtypeStruct((n_rows, d), BF16)
    if router_w is not None:
        in_specs.append(pl.BlockSpec(router_w.shape, lambda i: (0, 0)))
        args.append(router_w)
        out_specs = [out_specs, pl.BlockSpec((t, router_w.shape[1]), lambda i: (i, 0))]
        out_shape = [jax.ShapeDtypeStruct((n_rows, d), F32),
                     jax.ShapeDtypeStruct((n_rows, router_w.shape[1]), F32)]
    return pl.pallas_call(
        functools.partial(_normmod_kernel, with_router=router_w is not None),
        grid=(n_rows // t,),
        in_specs=in_specs, out_specs=out_specs, out_shape=out_shape,
        compiler_params=_cparams("parallel"),
        name="norm_modulate",
    )(*args)


def _matmul_kernel(*refs, n_in, epilogue):
    acc = None
    for x_ref, w_ref in zip(refs[:n_in], refs[n_in:2 * n_in]):
        part = jnp.dot(x_ref[...], w_ref[...], preferred_element_type=F32)
        acc = part if acc is None else acc + part
    rest = refs[2 * n_in:]
    if epilogue == "residual":
        h_ref, g_ref, o_ref = rest
        o_ref[...] = h_ref[...] + g_ref[...] * acc
    elif epilogue in ("head_norm", "head_norm_rope"):
        w_ref, o_ref = rest[0], rest[-1]
        if epilogue == "head_norm_rope":
            cos, sin = rest[1][...], rest[2][...]
            ri = lax.broadcasted_iota(jnp.int32, (HEAD_DIM, HEAD_DIM), 0)
            ci = lax.broadcasted_iota(jnp.int32, (HEAD_DIM, HEAD_DIM), 1)
            swap = (ri == ci + 1 - 2 * (ci % 2)).astype(BF16)
        for hd in range(acc.shape[1] // HEAD_DIM):
            cols = slice(hd * HEAD_DIM, (hd + 1) * HEAD_DIM)
            y = acc[:, cols]
            y = y * lax.rsqrt(jnp.mean(y * y, axis=-1, keepdims=True) + EPS) * w_ref[:, cols]
            if epilogue == "head_norm_rope":
                partner = jnp.dot(y.astype(BF16), swap, preferred_element_type=F32)
                y = y * cos + partner * sin
            o_ref[:, cols] = y.astype(o_ref.dtype)
    else:
        (o_ref,) = rest
        o_ref[...] = acc.astype(o_ref.dtype)


def matmul(xs, ws, n_rows, n_out, out_dtype, tn, residual=None, head_norm=None):
    tm = next(t for t in (1024, 512, ROW_BLOCK) if n_rows % t == 0)
    in_specs = [pl.BlockSpec((tm, x.shape[1]), lambda i, j: (i, 0)) for x in xs]
    for x, (_, row_off, col_off) in zip(xs, ws):
        k = x.shape[1]
        assert row_off % k == 0 and col_off % tn == 0 and n_out % tn == 0
        in_specs.append(pl.BlockSpec((k, tn), functools.partial(
            lambda i, j, rb, cb: (rb, j + cb), rb=row_off // k, cb=col_off // tn)))
    args = list(xs) + [w for w, _, _ in ws]
    epilogue = "plain"
    if residual is not None:
        epilogue = "residual"
        h, mod3, gate_chunk, (n_lat, seq, n_batch) = residual
        row = functools.partial(_mod_row, tile=tm, n_lat=n_lat, seq=seq, n_batch=n_batch)
        nj = mod3.shape[2] // 6 // tn
        in_specs += [pl.BlockSpec((tm, tn), lambda i, j: (i, j)),
                     pl.BlockSpec((None, 1, tn), lambda i, j: (row(i), 0, gate_chunk * nj + j))]
        args += [h, mod3]
    elif head_norm is not None:
        epilogue = "head_norm"
        w_cols, rope = head_norm
        in_specs.append(pl.BlockSpec((1, tn), lambda i, j: (0, j)))
        args.append(w_cols)
        if rope is not None:
            epilogue = "head_norm_rope"
            cos, sin, seq, n_lat = rope
            per_seq, lat_tiles = seq // tm, n_lat // tm
            tab = pl.BlockSpec((tm, HEAD_DIM), lambda i, j: (jnp.where(i < lat_tiles, i % per_seq, per_seq), 0))
            in_specs += [tab, tab]
            args += [cos, sin]
    return pl.pallas_call(
        functools.partial(_matmul_kernel, n_in=len(xs), epilogue=epilogue),
        grid=(n_rows // tm, n_out // tn),
        in_specs=in_specs,
        out_specs=pl.BlockSpec((tm, tn), lambda i, j: (i, j)),
        out_shape=jax.ShapeDtypeStruct((n_rows, n_out), out_dtype),
        compiler_params=_cparams("parallel", "parallel"),
        name="matmul_" + epilogue,
    )(*args)


CONV_HALO = 16


def _conv_silu_kernel(x_ref, prev_ref, next_ref, w_ref, b_ref, o_ref, *, per_seq, lat_blocks):
    i = pl.program_id(0)
    rows, width = o_ref.shape
    is_lat = i < lat_blocks
    has_prev = is_lat & (i % per_seq != 0)
    has_next = is_lat & (i % per_seq != per_seq - 1)
    x = x_ref[...]
    prev = jnp.where(has_prev, prev_ref[...].astype(F32), 0.0)
    nxt = jnp.where(has_next, next_ref[...].astype(F32), 0.0)
    out_row = lax.broadcasted_iota(jnp.int32, (rows, rows), 0)
    src_row = lax.broadcasted_iota(jnp.int32, (rows, rows), 1)
    edge = 8
    row8 = lax.broadcasted_iota(jnp.int32, (edge, 1), 0)
    acc = jnp.broadcast_to(b_ref[...], (rows, width))
    for k in range(SSD_CONV):
        shift = k - SSD_CONV // 2
        if shift == 0:
            tap = x.astype(F32)
        else:
            tap = jnp.dot((src_row == out_row + shift).astype(x.dtype), x, preferred_element_type=F32)
            if shift < 0:
                fix = tap[:edge]
                for r in range(-shift):
                    fix = jnp.where(row8 == r, prev[CONV_HALO + shift + r:CONV_HALO + shift + r + 1], fix)
                tap = jnp.concatenate([fix, tap[edge:]], axis=0)
            else:
                fix = tap[rows - edge:]
                for r in range(shift):
                    fix = jnp.where(row8 == edge - shift + r, nxt[r:r + 1], fix)
                tap = jnp.concatenate([tap[:rows - edge], fix], axis=0)
        acc = acc + w_ref[pl.ds(k, 1), :] * tap
    o_ref[...] = (acc * jax.nn.sigmoid(acc)).astype(o_ref.dtype)


def conv_silu(p_zx, conv_w, conv_b, n_lat, seq, ctx_len):
    t, tc = ROW_BLOCK, 1024
    assert ctx_len == t and seq % t == 0
    n_rows = p_zx.shape[0]
    col0 = SSD_INNER // tc
    halo_per_block = t // CONV_HALO
    last_halo = n_rows // CONV_HALO - 1
    return pl.pallas_call(
        functools.partial(_conv_silu_kernel, per_seq=seq // t, lat_blocks=n_lat // t),
        grid=(n_rows // t, SSD_XBC // tc),
        in_specs=[pl.BlockSpec((t, tc), lambda i, j: (i, col0 + j)),
                  pl.BlockSpec((CONV_HALO, tc), lambda i, j: (jnp.maximum(i * halo_per_block - 1, 0), col0 + j)),
                  pl.BlockSpec((CONV_HALO, tc),
                               lambda i, j: (jnp.minimum((i + 1) * halo_per_block, last_halo), col0 + j)),
                  pl.BlockSpec((SSD_CONV, tc), lambda i, j: (0, j)),
                  pl.BlockSpec((1, tc), lambda i, j: (0, j))],
        out_specs=pl.BlockSpec((t, tc), lambda i, j: (i, j)),
        out_shape=jax.ShapeDtypeStruct((n_rows, SSD_XBC), BF16),
        compiler_params=_cparams("parallel", "parallel"),
        name="conv_silu",
    )(p_zx, p_zx, p_zx, conv_w, conv_b.reshape(1, SSD_XBC))


def _ssd_dt_kernel(p_ref, bias_ref, a_ref, dt_ref, acs_ref, acst_ref):
    n = p_ref.shape[0]
    pre = p_ref[...] + bias_ref[...]
    dt = jnp.maximum(pre, 0.0) + jnp.log1p(jnp.exp(-jnp.abs(pre)))
    a = dt * a_ref[...]
    li = lax.broadcasted_iota(jnp.int32, (n, n), 0)
    si = lax.broadcasted_iota(jnp.int32, (n, n), 1)
    lower = (si <= li).astype(F32)
    upper = (si >= li).astype(F32)
    fwd = jnp.dot(lower, a, precision=HIGHEST, preferred_element_type=F32)
    bwd = jnp.dot(upper, a, precision=HIGHEST, preferred_element_type=F32)
    lane = lax.broadcasted_iota(jnp.int32, fwd.shape, 1)
    acs = jnp.where(lane < SSD_HEADS, fwd, bwd)
    dt_ref[...] = dt
    acs_ref[...] = acs
    acst_ref[...] = acs.T


def ssd_dt(p_dt, dt_bias, a_log):
    rows = p_dt.shape[0]
    pad = LANES - 2 * SSD_HEADS
    bias = jnp.pad(dt_bias.reshape(1, 2 * SSD_HEADS), ((0, 0), (0, pad)))
    a_neg = jnp.pad(-jnp.exp(a_log.astype(F32)).reshape(1, 2 * SSD_HEADS), ((0, 0), (0, pad)))
    c = SSD_CHUNK
    return pl.pallas_call(
        _ssd_dt_kernel,
        grid=(rows // c,),
        in_specs=[pl.BlockSpec((c, LANES), lambda i: (i, 0)),
                  pl.BlockSpec((1, LANES), lambda i: (0, 0)),
                  pl.BlockSpec((1, LANES), lambda i: (0, 0))],
        out_specs=[pl.BlockSpec((c, LANES), lambda i: (i, 0)),
                   pl.BlockSpec((c, LANES), lambda i: (i, 0)),
                   pl.BlockSpec((LANES, c), lambda i: (0, i))],
        out_shape=[jax.ShapeDtypeStruct((rows, LANES), F32),
                   jax.ShapeDtypeStruct((rows, LANES), F32),
                   jax.ShapeDtypeStruct((LANES, rows), F32)],
        compiler_params=_cparams("parallel"),
        name="ssd_dt",
    )(p_dt, bias, a_neg)


def _ssd_scan_kernel(*refs):
    fwd, bwd, outs = refs[:6], refs[6:12], refs[12:]

    @pl.when(pl.program_id(2) == 0)
    def _():
        for state_ref in outs[2:]:
            state_ref[...] = jnp.zeros_like(state_ref)

    _ssd_chunk(*fwd, outs[0], outs[2], reverse=False)
    _ssd_chunk(*bwd, outs[1], outs[3], reverse=True)


def _ssd_chunk(x_ref, b_ref, c_ref, dt_ref, acs_ref, acsr_ref, y_ref, state_ref, *, reverse):
    n = x_ref.shape[0]
    heads = dt_ref.shape[1]
    pair_w = 2 * SSD_HEAD_DIM
    x = x_ref[...].astype(F32)
    bm = b_ref[...]
    cm = c_ref[...]
    dt = dt_ref[...]
    acs = acs_ref[...]
    acs_row = acsr_ref[...]
    end = 0 if reverse else n - 1
    total = acs[end:end + 1, :]
    e_tot = jnp.exp(total)

    li = lax.broadcasted_iota(jnp.int32, (n, n), 0)
    si = lax.broadcasted_iota(jnp.int32, (n, n), 1)
    visible = (si >= li) if reverse else (si <= li)
    lo = lax.broadcasted_iota(jnp.int32, (1, pair_w), 1) < SSD_HEAD_DIM

    cb = lax.dot_general(cm, bm, (((1,), (1,)), ((), ())), preferred_element_type=F32)
    state = state_ref[...]
    y_off = jnp.dot(cm, state.astype(BF16), preferred_element_type=F32)

    xd_parts, decay_parts = [], []
    for i in range(heads // 2):
        j0, j1 = 2 * i, 2 * i + 1
        cols = slice(i * pair_w, (i + 1) * pair_w)
        xp = x[:, cols] * jnp.where(lo, dt[:, j0:j0 + 1], dt[:, j1:j1 + 1])
        ms, xs, acs_b = [], [], []
        for j, keep in ((j0, lo), (j1, jnp.logical_not(lo))):
            acs_b.append(jnp.broadcast_to(acs[:, j:j + 1], (n, n)))
            diff = acs_b[-1] - acs_row[j:j + 1, :]
            ms.append((cb * jnp.exp(jnp.where(visible, diff, MASK_VALUE))).astype(BF16))
            xs.append(jnp.where(keep, xp, 0.0).astype(BF16))
        y_diag = jnp.dot(jnp.concatenate(ms, axis=1), jnp.concatenate(xs, axis=0), preferred_element_type=F32)
        acs_pair = jnp.where(lo, acs_b[0], acs_b[1])
        tot_pair = jnp.where(lo, total[:, j0:j0 + 1], total[:, j1:j1 + 1])
        y_ref[:, cols] = (y_diag + y_off[:, cols] * jnp.exp(acs_pair)).astype(y_ref.dtype)
        xd_parts.append((xp * jnp.exp(tot_pair - acs_pair)).astype(BF16))
        decay_parts.append(jnp.where(lo, e_tot[:, j0:j0 + 1], e_tot[:, j1:j1 + 1]))
    xd = jnp.concatenate(xd_parts, axis=1)
    decay = jnp.concatenate(decay_parts, axis=1)
    bt = bm.astype(F32).T.astype(BF16)
    state_ref[...] = state * decay + jnp.dot(bt, xd, preferred_element_type=F32)


def ssd_scan(xbc, dt_col, acs_col, acs_row, n_batch, seq, ctx_len):
    c = SSD_CHUNK
    lat_chunks, ctx_chunks = seq // c, ctx_len // c
    ctx_base = n_batch * lat_chunks
    n_steps = lat_chunks + ctx_chunks
    gw = SSD_GROUP_W
    heads = SSD_HEADS // SSD_GROUPS
    b_col0 = SSD_INNER // SSD_STATE
    c_col0 = (SSD_INNER + SSD_GN) // SSD_STATE

    def specs(reverse):
        def chunk(b, k):
            kc = (ctx_chunks - 1 - k) if reverse else k
            kl = (n_steps - 1 - k) if reverse else (k - ctx_chunks)
            return jnp.where(k < ctx_chunks, ctx_base + b * ctx_chunks + kc, b * lat_chunks + kl)

        d = int(reverse)
        ins = [pl.BlockSpec((c, gw), lambda b, g, k: (chunk(b, k), g)),
               pl.BlockSpec((c, SSD_STATE), lambda b, g, k: (chunk(b, k), b_col0 + g)),
               pl.BlockSpec((c, SSD_STATE), lambda b, g, k: (chunk(b, k), c_col0 + g)),
               pl.BlockSpec((None, None, c, heads), lambda b, g, k: (d, g, chunk(b, k), 0)),
               pl.BlockSpec((None, None, c, heads), lambda b, g, k: (d, g, chunk(b, k), 0)),
               pl.BlockSpec((None, None, heads, c), lambda b, g, k: (d, g, 0, chunk(b, k)))]
        return ins, pl.BlockSpec((c, gw), lambda b, g, k: (chunk(b, k), g))

    (in_f, out_f), (in_b, out_b) = specs(False), specs(True)
    y_shape = jax.ShapeDtypeStruct((xbc.shape[0], SSD_INNER), BF16)
    args = (xbc, xbc, xbc, dt_col, acs_col, acs_row)
    return pl.pallas_call(
        _ssd_scan_kernel,
        grid=(n_batch, SSD_GROUPS, n_steps),
        in_specs=in_f + in_b,
        out_specs=[out_f, out_b],
        out_shape=[y_shape, y_shape],
        scratch_shapes=[pltpu.VMEM((SSD_STATE, gw), F32), pltpu.VMEM((SSD_STATE, gw), F32)],
        compiler_params=_cparams("parallel", "parallel", "arbitrary"),
        name="ssd_scan",
    )(*args, *args)


def _ssd_gate_kernel(yf_ref, yb_ref, x_ref, z_ref, d_ref, w_ref, o_ref):
    z = z_ref[...].astype(F32)
    y = yf_ref[...].astype(F32) + yb_ref[...].astype(F32) + d_ref[...] * x_ref[...].astype(F32)
    g = y * (z * jax.nn.sigmoid(z))
    o = g * lax.rsqrt(jnp.mean(g * g, axis=-1, keepdims=True) + EPS) * w_ref[...]
    o_ref[...] = o.astype(o_ref.dtype)


def ssd_gate(y_f, y_b, xbc, p_zx, d_skip, norm_w):
    rows = y_f.shape[0]
    t, gw = ROW_BLOCK, SSD_GROUP_W
    d_cols = jnp.repeat(d_skip.astype(F32), SSD_HEAD_DIM).reshape(1, SSD_INNER)
    spec = pl.BlockSpec((t, gw), lambda i, g: (i, g))
    vec = pl.BlockSpec((1, gw), lambda i, g: (0, g))
    return pl.pallas_call(
        _ssd_gate_kernel,
        grid=(rows // t, SSD_GROUPS),
        in_specs=[spec, spec, spec, spec, vec, vec],
        out_specs=spec,
        out_shape=jax.ShapeDtypeStruct((rows, SSD_INNER), BF16),
        compiler_params=_cparams("parallel", "parallel"),
        name="ssd_gate",
    )(y_f, y_b, xbc, p_zx, d_cols, norm_w.reshape(1, SSD_INNER))


ROPE_IDENTITY_ROWS = 1024


def _rope_tables(seq):
    t = jnp.arange(seq, dtype=jnp.int32)
    row = (t // GRID_W).astype(F32)
    col = (t % GRID_W).astype(F32)
    axis_dim = HEAD_DIM // 2
    inv = 1.0 / (ROPE_THETA ** (jnp.arange(0, axis_dim, 2, dtype=F32) / axis_dim))
    ang = jnp.concatenate([row[:, None] * inv[None], col[:, None] * inv[None]], axis=-1)
    cos = jnp.repeat(jnp.cos(ang), 2, axis=-1)
    sin = jnp.stack([-jnp.sin(ang), jnp.sin(ang)], axis=-1).reshape(seq, HEAD_DIM)
    identity = jnp.ones((ROPE_IDENTITY_ROWS, HEAD_DIM), F32)
    return jnp.concatenate([cos, identity], axis=0), jnp.concatenate([sin, 0.0 * identity], axis=0)


LOG2E = math.log2(math.e)
KEY_CHUNK = 256
DIFF_Q_TILE = 1024
GQA_Q_TILE = 512


def _scores(q, k, bias):
    s = lax.dot_general(q, k, (((1,), (1,)), ((), ())), preferred_element_type=F32)
    return s if bias is None else s + bias


def _attend_two_pass(q, pieces):
    scores = [_scores(q, k, bias) for k, _, bias in pieces]
    m = functools.reduce(jnp.maximum, [jnp.max(s, axis=-1, keepdims=True) for s in scores])
    denom = acc = None
    for s, (_, v, _) in zip(scores, pieces):
        p = jnp.exp2(s - m)
        p_sum = jnp.sum(p, axis=-1, keepdims=True)
        pv = jnp.dot(p.astype(v.dtype), v, preferred_element_type=F32)
        denom = p_sum if denom is None else denom + p_sum
        acc = pv if acc is None else acc + pv
    return acc / denom


def _attend(q, pieces):
    chunks = []
    for k, v, bias in pieces:
        for c0 in range(0, k.shape[0], KEY_CHUNK):
            c1 = min(c0 + KEY_CHUNK, k.shape[0])
            chunks.append((k[c0:c1], v[c0:c1], None if bias is None else bias[:, c0:c1]))
    m = denom = acc = None
    for k, v, bias in chunks:
        s = lax.dot_general(q, k, (((1,), (1,)), ((), ())), preferred_element_type=F32)
        if bias is not None:
            s = s + bias
        m_chunk = jnp.max(s, axis=-1, keepdims=True)
        m_new = m_chunk if m is None else jnp.maximum(m, m_chunk)
        p = jnp.exp2(s - m_new)
        p_sum = jnp.sum(p, axis=-1, keepdims=True)
        pv = jnp.dot(p.astype(v.dtype), v, preferred_element_type=F32)
        if m is None:
            denom, acc = p_sum, pv
        else:
            alpha = jnp.exp2(m - m_new)
            denom = alpha * denom + p_sum
            acc = alpha * acc + pv
        m = m_new
    return acc / denom


NA_HEADS_PER_STEP = 4


def _na_kernel(q_ref, k0_ref, k1_ref, k2_ref, kc_ref, v0_ref, v1_ref, v2_ref, vc_ref, bias_ref, o_ref):
    t = q_ref.shape[0]
    for hd in range(NA_HEADS_PER_STEP):
        cols = slice(hd * HEAD_DIM, (hd + 1) * HEAD_DIM)
        pieces = [(k_ref[:, cols], v_ref[:, cols], bias_ref[hd, :, i * t:(i + 1) * t])
                  for i, (k_ref, v_ref) in enumerate(((k0_ref, v0_ref), (k1_ref, v1_ref), (k2_ref, v2_ref)))]
        pieces.append((kc_ref[:, cols], vc_ref[:, cols], None))
        o_ref[:, cols] = _attend_two_pass(q_ref[:, cols], pieces).astype(o_ref.dtype)


def _na_bias_table(rpb, rows):
    n_blocks = rows // NA_QROWS
    w0_of = lambda j: min(max(NA_QROWS * j - NA_WIN_ROWS // 2, 0), rows - NA_KROWS)
    geoms, pattern_of = [], []
    for j in range(n_blocks):
        r0s = tuple(min(max(NA_QROWS * j + a - NA_WIN_ROWS // 2, 0), rows - NA_WIN_ROWS) - w0_of(j)
                    for a in range(NA_QROWS))
        key = (NA_QROWS * j - w0_of(j), r0s)
        if key not in geoms:
            geoms.append(key)
        pattern_of.append(geoms.index(key))
    qc = jnp.arange(GRID_W)[:, None]
    kc = jnp.arange(GRID_W)[None, :]
    cs = jnp.clip(qc - NA_WIN_COLS // 2, 0, GRID_W - NA_WIN_COLS)
    col_ok = (kc >= cs) & (kc < cs + NA_WIN_COLS)
    dc = jnp.clip(kc - qc + NA_WIN_COLS - 1, 0, 2 * NA_WIN_COLS - 2)
    pick = (dc[None] == jnp.arange(2 * NA_WIN_COLS - 1)[:, None, None]).astype(F32)
    tiles = jnp.einsum('hrd,dqk->hrqk', rpb.astype(F32), pick, precision=HIGHEST)
    tiles = jnp.where(col_ok, tiles * LOG2E, MASK_VALUE)
    masked = jnp.full((rpb.shape[0], GRID_W, GRID_W), MASK_VALUE, F32)
    tables = []
    for q_row0, r0s in geoms:
        tile_rows = []
        for qi in range(NA_QROWS):
            row = [tiles[:, ki - (q_row0 + qi) + NA_WIN_ROWS - 1] if r0s[qi] <= ki < r0s[qi] + NA_WIN_ROWS
                   else masked for ki in range(NA_KROWS)]
            tile_rows.append(jnp.concatenate(row, axis=2))
        tables.append(jnp.concatenate(tile_rows, axis=1))
    tables.append(jnp.full_like(tables[0], MASK_VALUE))
    pattern_of.append(len(geoms))
    w0_blocks = [w0_of(j) // NA_QROWS for j in range(n_blocks)] + [0]
    return jnp.stack(tables, axis=1), pattern_of, w0_blocks


def _lookup(values, j):
    out = jnp.int32(values[-1])
    for idx in range(len(values) - 2, -1, -1):
        out = jnp.where(j == idx, jnp.int32(values[idx]), out)
    return out


def na_attention(qn, kn, p_na, v_col_off, rpb, n_batch, seq, ctx_len):
    t = NA_QROWS * GRID_W
    assert t == ROW_BLOCK == ctx_len
    rows = seq // GRID_W
    n_blocks = seq // t
    bias, pattern_of, w0_blocks = _na_bias_table(rpb, rows)
    hw = NA_HEADS_PER_STEP * HEAD_DIM
    v_col0 = v_col_off // hw
    ctx_base = n_batch * n_blocks

    def q_blk(j, b):
        return jnp.where(j < n_blocks, b * n_blocks + j, ctx_base + b)

    def kv_blk(piece):
        return lambda h, j, b: (b * n_blocks + _lookup(w0_blocks, j) + piece, h)

    def kv_blk_v(piece):
        return lambda h, j, b: (b * n_blocks + _lookup(w0_blocks, j) + piece, v_col0 + h)

    blk = lambda fn: pl.BlockSpec((t, hw), fn)
    in_specs = ([blk(lambda h, j, b: (q_blk(j, b), h))]
                + [blk(kv_blk(i)) for i in range(3)] + [blk(lambda h, j, b: (ctx_base + b, h))]
                + [blk(kv_blk_v(i)) for i in range(3)] + [blk(lambda h, j, b: (ctx_base + b, v_col0 + h))]
                + [pl.BlockSpec((NA_HEADS_PER_STEP, None, t, 3 * t),
                                lambda h, j, b: (h, _lookup(pattern_of, j), 0, 0))])
    return pl.pallas_call(
        _na_kernel,
        grid=(NA_HEADS // NA_HEADS_PER_STEP, n_blocks + 1, n_batch),
        in_specs=in_specs,
        out_specs=blk(lambda h, j, b: (q_blk(j, b), h)),
        out_shape=jax.ShapeDtypeStruct((qn.shape[0], NA_WIDTH), BF16),
        compiler_params=_cparams("parallel", "parallel", "parallel"),
        name="na_attention",
    )(qn, kn, kn, kn, kn, p_na, p_na, p_na, p_na, bias)


def _gqa_kernel(q_ref, kl_ref, kc_ref, vl_ref, vc_ref, o_ref):
    t = q_ref.shape[0]
    q = jnp.concatenate([q_ref[:, g * HEAD_DIM:(g + 1) * HEAD_DIM] for g in range(GQA_GROUP)], axis=0)
    o = _attend(q, [(kc_ref[...], vc_ref[...], None), (kl_ref[...], vl_ref[...], None)])
    for g in range(GQA_GROUP):
        o_ref[:, g * HEAD_DIM:(g + 1) * HEAD_DIM] = o[g * t:(g + 1) * t].astype(o_ref.dtype)


def gqa_attention(qn, kn, p_kv, v_col_off, n_batch, seq, ctx_len):
    t = GQA_Q_TILE
    q_tiles = seq // t
    ctx_base = n_batch * seq // ctx_len
    v0 = v_col_off // HEAD_DIM
    gw = GQA_GROUP * HEAD_DIM
    return pl.pallas_call(
        _gqa_kernel,
        grid=(n_batch, GQA_KV_HEADS, q_tiles),
        in_specs=[pl.BlockSpec((t, gw), lambda b, h, i: (b * q_tiles + i, h)),
                  pl.BlockSpec((seq, HEAD_DIM), lambda b, h, i: (b, h)),
                  pl.BlockSpec((ctx_len, HEAD_DIM), lambda b, h, i: (ctx_base + b, h)),
                  pl.BlockSpec((seq, HEAD_DIM), lambda b, h, i: (b, v0 + h)),
                  pl.BlockSpec((ctx_len, HEAD_DIM), lambda b, h, i: (ctx_base + b, v0 + h))],
        out_specs=pl.BlockSpec((t, gw), lambda b, h, i: (b * q_tiles + i, h)),
        out_shape=jax.ShapeDtypeStruct((n_batch * seq, GQA_Q), BF16),
        compiler_params=_cparams("parallel", "parallel", "parallel"),
        name="gqa_attention",
    )(qn, kn, kn, p_kv, p_kv)


def _diff_kernel(lam_ref, q_ref, kl_ref, kc_ref, vl_ref, vc_ref, w_ref, o_ref, *, out_scale):
    outs = []
    for comp in range(2):
        cols = slice(comp * HEAD_DIM, (comp + 1) * HEAD_DIM)
        outs.append(_attend(q_ref[:, cols], [(kc_ref[:, cols], vc_ref[...], None),
                                              (kl_ref[:, cols], vl_ref[...], None)]))
    o = outs[0] - lam_ref[0] * outs[1]
    o = o * lax.rsqrt(jnp.mean(o * o, axis=-1, keepdims=True) + EPS) * w_ref[...]
    o_ref[...] = (o * out_scale).astype(o_ref.dtype)


def diff_attention(qn, q_col_off, kn, p_kv, v_col_off, lam, subln_w, out_scale, n_batch, seq, ctx_len):
    t = DIFF_Q_TILE
    q_tiles = seq // t
    ctx_base = n_batch * seq // ctx_len
    v0 = v_col_off // DIFF_V_DIM
    w2 = 2 * HEAD_DIM
    q0 = q_col_off // w2
    return pl.pallas_call(
        functools.partial(_diff_kernel, out_scale=out_scale),
        grid=(n_batch, DIFF_HEADS, q_tiles),
        in_specs=[pl.BlockSpec(memory_space=pltpu.SMEM),
                  pl.BlockSpec((t, w2), lambda b, h, i: (b * q_tiles + i, q0 + h)),
                  pl.BlockSpec((seq, w2), lambda b, h, i: (b, h)),
                  pl.BlockSpec((ctx_len, w2), lambda b, h, i: (ctx_base + b, h)),
                  pl.BlockSpec((seq, DIFF_V_DIM), lambda b, h, i: (b, v0 + h)),
                  pl.BlockSpec((ctx_len, DIFF_V_DIM), lambda b, h, i: (ctx_base + b, v0 + h)),
                  pl.BlockSpec((1, DIFF_V_DIM), lambda b, h, i: (0, 0))],
        out_specs=pl.BlockSpec((t, DIFF_V_DIM), lambda b, h, i: (b * q_tiles + i, h)),
        out_shape=jax.ShapeDtypeStruct((n_batch * seq, DIFF_V), BF16),
        compiler_params=_cparams("parallel", "parallel", "parallel"),
        name="diff_attention",
    )(lam.reshape(1), qn, kn, kn, p_kv, p_kv, subln_w.reshape(1, DIFF_V_DIM))


def _moe_plan_kernel(route_ref, pos_ref, pad_end_ref, slot_ref, counts_ref, start_ref):
    phase, i = pl.program_id(0), pl.program_id(1)
    t = route_ref.shape[0]
    route = route_ref[...]
    lane = lax.broadcasted_iota(jnp.int32, route.shape, 1)
    chosen = [lane == route[:, ROUTE_EXPERT_LANE + k:ROUTE_EXPERT_LANE + k + 1].astype(jnp.int32)
              for k in range(MOE_TOPK)]
    both = functools.reduce(jnp.add, [c.astype(F32) for c in chosen])
    tile_counts = jnp.sum(both, axis=0, keepdims=True)

    @pl.when((phase == 0) & (i == 0))
    def _():
        counts_ref[...] = jnp.zeros_like(counts_ref)

    @pl.when(phase == 0)
    def _():
        counts_ref[...] += tile_counts

    @pl.when((phase == 1) & (i == 0))
    def _():
        padded = jnp.ceil(counts_ref[...] * (1.0 / MOE_BLOCK)) * MOE_BLOCK
        rows = pad_end_ref.shape[0]
        ki = lax.broadcasted_iota(jnp.int32, (LANES, LANES), 0)
        ni = lax.broadcasted_iota(jnp.int32, (LANES, LANES), 1)
        pad_end = jnp.dot(jnp.broadcast_to(padded, (rows, LANES)), (ki <= ni).astype(F32),
                          precision=HIGHEST, preferred_element_type=F32)
        pad_end_ref[...] = pad_end
        start_ref[...] = pad_end[0:1] - padded
        slot_ref[...] = jnp.zeros_like(slot_ref)

    @pl.when(phase == 1)
    def _():
        li = lax.broadcasted_iota(jnp.int32, (t, t), 0)
        si = lax.broadcasted_iota(jnp.int32, (t, t), 1)
        earlier = jnp.dot((si < li).astype(BF16), both.astype(BF16), preferred_element_type=F32)
        row_of = earlier + start_ref[...]
        out = jnp.zeros(route.shape, F32)
        token = (i * t + 1 + lax.broadcasted_iota(jnp.int32, (t, 1), 0)).astype(F32)
        blk_lane = lax.broadcasted_iota(jnp.int32, (t, slot_ref.shape[0]), 1)
        placed = jnp.zeros(slot_ref.shape, F32)
        for k in range(MOE_TOPK):
            p = jnp.sum(jnp.where(chosen[k], row_of, 0.0), axis=-1, keepdims=True)
            out = jnp.where(lane == k, p, out)
            blk = jnp.floor(p * (1.0 / MOE_BLOCK))
            in_blk = (p - blk * MOE_BLOCK).astype(jnp.int32)
            block_hot = (blk_lane == blk.astype(jnp.int32)).astype(F32)
            row_hot = jnp.where(lane == in_blk, token, 0.0)
            placed = placed + jnp.dot(block_hot.T, row_hot, precision=HIGHEST, preferred_element_type=F32)
        slot_ref[...] += placed
        pos_ref[...] = out.astype(jnp.int32)
        start_ref[...] += tile_counts


def moe_plan(route, n_blocks):
    n = route.shape[0]
    t = ROW_BLOCK
    blocks_pad = -(-n_blocks // LANES) * LANES
    return pl.pallas_call(
        _moe_plan_kernel,
        grid=(2, n // t),
        in_specs=[pl.BlockSpec((t, LANES), lambda p, i: (i, 0))],
        out_specs=[pl.BlockSpec((t, LANES), lambda p, i: (i * p, 0)),
                   pl.BlockSpec((8, LANES), lambda p, i: (0, 0)),
                   pl.BlockSpec((blocks_pad, MOE_BLOCK), lambda p, i: (0, 0))],
        out_shape=[jax.ShapeDtypeStruct((n, LANES), jnp.int32),
                   jax.ShapeDtypeStruct((8, LANES), F32),
                   jax.ShapeDtypeStruct((blocks_pad, MOE_BLOCK), F32)],
        scratch_shapes=[pltpu.VMEM((1, LANES), F32), pltpu.VMEM((1, LANES), F32)],
        compiler_params=_cparams("arbitrary", "arbitrary"),
        name="moe_plan",
    )(route)


MOE_DMA_CHUNKS = 4


def _moe_kernel(be_ref, next_ref, seg_ref, nused_ref, x_ref, w1_hbm, w3_hbm, w2_hbm, o_ref,
                s1, s3, s2, w1b, w3b, w2b, sems, *, layer):
    i = pl.program_id(0)
    slot = seg_ref[i] % 2

    def weight_copies(e, slot):
        copies = []
        for n, (w_hbm, stage) in enumerate(((w1_hbm, s1), (w3_hbm, s3), (w2_hbm, s2))):
            rows = stage.shape[1] // MOE_DMA_CHUNKS
            for c in range(MOE_DMA_CHUNKS):
                slab = pl.ds(c * rows, rows)
                copies.append(pltpu.make_async_copy(w_hbm.at[layer, e, slab], stage.at[slot, slab],
                                                    sems.at[slot, n, c]))
        return copies

    @pl.when(i == 0)
    def _():
        for c in weight_copies(be_ref[0], slot):
            c.start()

    prev = be_ref[jnp.maximum(i - 1, 0)]

    @pl.when((i == 0) | (be_ref[i] != prev))
    def _():
        for c in weight_copies(be_ref[i], slot):
            c.wait()

        @pl.when(next_ref[i] >= 0)
        def _():
            for c in weight_copies(next_ref[i], 1 - slot):
                c.start()

        w1b[...] = s1[slot].astype(BF16)
        w3b[...] = s3[slot].astype(BF16)
        w2b[...] = s2[slot].astype(BF16)

    @pl.when(i < nused_ref[0])
    def _():
        x = x_ref[...].astype(BF16)
        a = jnp.dot(x, w1b[...], preferred_element_type=F32)
        b = jnp.dot(x, w3b[...], preferred_element_type=F32)
        hid = (a * jax.nn.sigmoid(a)) * b
        o_ref[...] = jnp.dot(hid.astype(BF16), w2b[...], preferred_element_type=F32)

    @pl.when(i >= nused_ref[0])
    def _():
        o_ref[...] = jnp.zeros_like(o_ref)


def moe_experts(x_sorted, block_e, next_e, n_used, w1, w3, w2, layer):
    n_blocks = block_e.shape[0]
    d, hid = w1.shape[2], w1.shape[3]
    hbm = pl.BlockSpec(memory_space=pl.ANY)
    changed = jnp.concatenate([jnp.zeros((1,), jnp.int32), (block_e[1:] != block_e[:-1]).astype(jnp.int32)])
    segment = jnp.cumsum(changed, dtype=jnp.int32)
    row_spec = pl.BlockSpec((MOE_BLOCK, d), lambda i, be, nx, sg, nu: (i, 0))
    x_spec = pl.BlockSpec((MOE_BLOCK, d), lambda i, be, nx, sg, nu: (jnp.minimum(i, nu[0] - 1), 0))
    grid_spec = pltpu.PrefetchScalarGridSpec(
        num_scalar_prefetch=4,
        grid=(n_blocks,),
        in_specs=[x_spec, hbm, hbm, hbm],
        out_specs=row_spec,
        scratch_shapes=[pltpu.VMEM((2, d, hid), F32), pltpu.VMEM((2, d, hid), F32), pltpu.VMEM((2, hid, d), F32),
                        pltpu.VMEM((d, hid), BF16), pltpu.VMEM((d, hid), BF16), pltpu.VMEM((hid, d), BF16),
                        pltpu.SemaphoreType.DMA((2, 3, MOE_DMA_CHUNKS))],
    )
    return pl.pallas_call(
        functools.partial(_moe_kernel, layer=layer),
        grid_spec=grid_spec,
        out_shape=jax.ShapeDtypeStruct((n_blocks * MOE_BLOCK, d), F32),
        compiler_params=_cparams("arbitrary"),
        name="moe_experts",
    )(block_e, next_e, segment, n_used, x_sorted, w1, w3, w2)


def _combine_kernel(h_ref, g_ref, route_ref, *refs, n_y, next_norm):
    y_refs, rest = refs[:n_y], refs[n_y:]
    acc = None
    for k, y_ref in enumerate(y_refs):
        part = y_ref[...] * route_ref[:, ROUTE_GATE_LANE + k:ROUTE_GATE_LANE + k + 1]
        acc = part if acc is None else acc + part
    h = h_ref[...] + g_ref[...] * acc
    if next_norm:
        w_ref, sc_ref, sh_ref, o_ref, a_ref = rest
        y = h * lax.rsqrt(jnp.mean(h * h, axis=-1, keepdims=True) + EPS) * w_ref[...]
        a_ref[...] = (y * (1.0 + sc_ref[...]) + sh_ref[...]).astype(a_ref.dtype)
    else:
        (o_ref,) = rest
    o_ref[...] = h


def moe_combine(h, mod3, gate_chunk, route, ys, geom, next_norm=None):
    n_lat, seq, n_batch = geom
    n_rows, d = ys[0].shape
    t = ROW_BLOCK
    row = functools.partial(_mod_row, tile=t, n_lat=n_lat, seq=seq, n_batch=n_batch)
    spec = pl.BlockSpec((t, d), lambda i: (i, 0))
    in_specs = [spec, pl.BlockSpec((None, 1, d), lambda i: (row(i), 0, gate_chunk)),
                pl.BlockSpec((t, LANES), lambda i: (i, 0))] + [spec] * len(ys)
    args = [h, mod3, route, *ys]
    out_specs, out_shape = spec, jax.ShapeDtypeStruct((n_rows, d), F32)
    if next_norm is not None:
        w, mod3_next, sc_chunk, sh_chunk = next_norm
        in_specs += [pl.BlockSpec((1, d), lambda i: (0, 0)),
                     pl.BlockSpec((None, 1, d), lambda i: (row(i), 0, sc_chunk)),
                     pl.BlockSpec((None, 1, d), lambda i: (row(i), 0, sh_chunk))]
        args += [w.reshape(1, d), mod3_next, mod3_next]
        out_specs, out_shape = [spec, spec], [out_shape, jax.ShapeDtypeStruct((n_rows, d), BF16)]
    return pl.pallas_call(
        functools.partial(_combine_kernel, n_y=len(ys), next_norm=next_norm is not None),
        grid=(n_rows // t,),
        in_specs=in_specs, out_specs=out_specs, out_shape=out_shape,
        compiler_params=_cparams("parallel"),
        name="moe_combine",
    )(*args)


def moe_layer(h, f, route, mod3, gate_chunk, w1, w3, w2, layer, geom, next_norm=None):
    n = f.shape[0]
    n_assign = n * MOE_TOPK
    n_blocks = -(-n_assign // MOE_BLOCK) + MOE_EXPERTS
    pos, pad_end, slot = moe_plan(route, n_blocks)
    pos = pos[:, :MOE_TOPK]
    pad_end = pad_end[0, :MOE_EXPERTS].astype(jnp.int32)
    n_used = pad_end[-1] // MOE_BLOCK
    blk = jnp.minimum(jnp.arange(n_blocks, dtype=jnp.int32), n_used - 1) * MOE_BLOCK
    block_e = jnp.minimum(jnp.sum(pad_end[None, :] <= blk[:, None], axis=1, dtype=jnp.int32), MOE_EXPERTS - 1)
    seg_end = pad_end[block_e] // MOE_BLOCK
    next_e = jnp.where(seg_end < n_used, block_e[jnp.minimum(seg_end, n_blocks - 1)], -1)
    slot = slot[:n_blocks].reshape(-1).astype(jnp.int32)
    slot_tok = jnp.where(slot > 0, slot - 1, jnp.arange(n_blocks * MOE_BLOCK, dtype=jnp.int32) % n)
    x_sorted = f.at[slot_tok].get(mode='promise_in_bounds')
    yb = moe_experts(x_sorted, block_e, next_e, n_used.reshape(1), w1, w3, w2, layer)
    ys = [yb.at[pos[:, k]].get(mode='promise_in_bounds') for k in range(MOE_TOPK)]
    return moe_combine(h, mod3, gate_chunk, route, ys, geom, next_norm)


def _even_layer_mix(a, in_w, conv_w, conv_b, dt_bias, a_log, d_skip, ssd_norm_w, q_norm, k_norm, rpb,
                    n_batch, seq, ctx_len):
    n_rows = a.shape[0]
    n_lat = n_batch * seq
    zx_end = SSD_INNER + SSD_XBC
    dt_end = zx_end + 2 * SSD_HEADS
    w = jnp.concatenate([in_w[:, :zx_end], in_w[:, dt_end:],
                         jnp.pad(in_w[:, zx_end:dt_end], ((0, 0), (0, LANES - 2 * SSD_HEADS)))],
                        axis=1).astype(BF16)
    na_off = lambda i: zx_end + i * NA_WIDTH
    dt_off = zx_end + 3 * NA_WIDTH
    tile_w = lambda v, s: (jnp.tile(v.astype(F32), NA_HEADS) * s).reshape(1, NA_WIDTH)
    p_zx = matmul([a], [(w, 0, 0)], n_rows, zx_end, BF16, tn=512)
    p_dt = matmul([a], [(w, 0, dt_off)], n_rows, LANES, F32, tn=LANES)
    qn = matmul([a], [(w, 0, na_off(0))], n_rows, NA_WIDTH, BF16, tn=512,
                head_norm=(tile_w(q_norm, HEAD_DIM ** -0.5 * LOG2E), None))
    kn = matmul([a], [(w, 0, na_off(1))], n_rows, NA_WIDTH, BF16, tn=512,
                head_norm=(tile_w(k_norm, 1.0), None))
    p_v = matmul([a], [(w, 0, na_off(2))], n_rows, NA_WIDTH, BF16, tn=512)

    xbc = conv_silu(p_zx, conv_w, conv_b, n_lat, seq, ctx_len)
    dt, acs, acs_t = ssd_dt(p_dt, dt_bias, a_log)
    hg = SSD_HEADS // SSD_GROUPS
    col = lambda v: v[:, :2 * SSD_HEADS].reshape(n_rows, 2, SSD_GROUPS, hg).transpose(1, 2, 0, 3)
    dt_col, acs_col = col(dt), col(acs)
    acs_row = acs_t[:2 * SSD_HEADS].reshape(2, SSD_GROUPS, hg, n_rows)
    y_f, y_b = ssd_scan(xbc, dt_col, acs_col, acs_row, n_batch, seq, ctx_len)
    s_out = ssd_gate(y_f, y_b, xbc, p_zx, d_skip, ssd_norm_w)

    n_out = na_attention(qn, kn, p_v, 0, rpb, n_batch, seq, ctx_len)
    return s_out, n_out


def _odd_layer_mix(a, in_w, gq_norm, gk_norm, dq_norm, dk_norm, lam_vecs, subln_w, lambda_init,
                   n_batch, seq, ctx_len):
    n_rows = a.shape[0]
    n_lat = n_batch * seq
    cos, sin = _rope_tables(seq)
    rope = (cos, sin, seq, n_lat)
    scale = HEAD_DIM ** -0.5 * LOG2E
    rep = lambda w, n: jnp.tile(w.astype(F32), n).reshape(1, n * HEAD_DIM)
    w = in_w.astype(BF16)
    gk_off, gv_off, dk_off, dv_off = ODD_Q, ODD_Q + GQA_KV, ODD_Q + 2 * GQA_KV, ODD_Q + 2 * GQA_KV + DIFF_QK
    q_w = jnp.concatenate([rep(gq_norm, GQA_HEADS), rep(dq_norm, 2 * DIFF_HEADS)], axis=1) * scale
    qn = matmul([a], [(w, 0, 0)], n_lat, ODD_Q, BF16, tn=512, head_norm=(q_w, rope))
    gkn = matmul([a], [(w, 0, gk_off)], n_rows, GQA_KV, BF16, tn=512,
                 head_norm=(rep(gk_norm, GQA_KV_HEADS), rope))
    dkn = matmul([a], [(w, 0, dk_off)], n_rows, DIFF_QK, BF16, tn=512,
                 head_norm=(rep(dk_norm, 2 * DIFF_HEADS), rope))
    gv = matmul([a], [(w, 0, gv_off)], n_rows, GQA_KV, BF16, tn=512)
    dv = matmul([a], [(w, 0, dv_off)], n_rows, DIFF_V, BF16, tn=512)
    lv = lam_vecs.astype(F32)
    lam = jnp.exp(jnp.dot(lv[0], lv[1])) - jnp.exp(jnp.dot(lv[2], lv[3])) + lambda_init
    og = gqa_attention(qn, gkn, gv, 0, n_batch, seq, ctx_len)
    od = diff_attention(qn, GQA_Q, dkn, dv, 0, lam, subln_w, 1.0 - lambda_init, n_batch, seq, ctx_len)
    return og, od


def kernel(x, c, ctx, c_ctx, ada_w, ada_b, norm_w, ev_in_w, ev_conv_w, ev_conv_b, ev_dt_bias, ev_a_log, ev_d_skip, ev_ssd_norm_w, ev_na_q_norm, ev_na_k_norm, ev_na_rpb, ev_out_w, od_in_w, od_gqa_q_norm, od_gqa_k_norm, od_diff_q_norm, od_diff_k_norm, od_lambda, od_diff_subln, od_out_w, moe_group_w, moe_expert_w, moe_w1, moe_w3, moe_w2):
    n_batch, seq, d = x.shape
    ctx_len = ctx.shape[1]
    depth = ada_w.shape[0]
    assert depth == 2 and d == D_MODEL
    n_lat, n_ctx = n_batch * seq, n_batch * ctx_len
    n_all = n_lat + n_ctx
    geom = (n_lat, seq, n_batch)

    cc = jnp.concatenate([c, c_ctx[None, :], jnp.zeros((8 - n_batch - 1, d), F32)], axis=0)
    mod = ada_modulation(cc, ada_w, ada_b)
    router_w = lambda l: jnp.pad(jnp.concatenate([moe_group_w[l], moe_expert_w[l]], axis=1),
                                 ((0, 0), (0, LANES - MOE_GROUPS - MOE_EXPERTS)))

    h = jnp.concatenate([x.reshape(n_lat, d), ctx.reshape(n_ctx, d)], axis=0)

    mod3 = mod[0].reshape(8, 1, 6 * d)
    a = norm_modulate(h, norm_w[0, 0], mod3, 1, 0, n_all, geom)
    s_out, n_out = _even_layer_mix(a, ev_in_w[0], ev_conv_w[0], ev_conv_b[0], ev_dt_bias[0], ev_a_log[0],
                                   ev_d_skip[0], ev_ssd_norm_w[0], ev_na_q_norm[0], ev_na_k_norm[0],
                                   ev_na_rpb[0], n_batch, seq, ctx_len)
    out_w = ev_out_w[0].astype(BF16)
    h = matmul([s_out, n_out], [(out_w, 0, 0), (out_w, SSD_INNER, 0)], n_all, d, F32, tn=512,
               residual=(h, mod3, 2, geom))
    f, route = norm_modulate(h, norm_w[0, 1], mod3, 4, 3, n_all, geom, router_w=router_w(0))
    mod3_next = mod[1].reshape(8, 1, 6 * d)
    h, a = moe_layer(h, f, route, mod3, 5, moe_w1, moe_w3, moe_w2, 0, geom,
                     next_norm=(norm_w[1, 0], mod3_next, 1, 0))

    mod3 = mod3_next
    lambda_init = 0.8 - 0.6 * math.exp(-0.3 * 1)
    og, od = _odd_layer_mix(a, od_in_w[0], od_gqa_q_norm[0], od_gqa_k_norm[0], od_diff_q_norm[0],
                            od_diff_k_norm[0], od_lambda[0], od_diff_subln[0], lambda_init,
                            n_batch, seq, ctx_len)
    out_w = od_out_w[0].astype(BF16)
    h = matmul([og, od], [(out_w, 0, 0), (out_w, GQA_Q, 0)], n_lat, d, F32, tn=512,
               residual=(h, mod3, 2, geom))
    f, route = norm_modulate(h, norm_w[1, 1], mod3, 4, 3, n_lat, geom, router_w=router_w(1))
    h = moe_layer(h, f, route, mod3, 5, moe_w1, moe_w3, moe_w2, 1, geom)
    return h.reshape(n_batch, seq, d)
```

```python
import functools
import math

import jax
import jax.numpy as jnp
from jax import lax
from jax.experimental import pallas as pl
from jax.experimental.pallas import tpu as pltpu

F32 = jnp.float32
BF16 = jnp.bfloat16
HIGHEST = lax.Precision.HIGHEST

D_MODEL = 2048
GRID_W = 64
EPS = 1e-6
ROPE_THETA = 10000.0
HEAD_DIM = 128

SSD_HEADS = 32
SSD_HEAD_DIM = 64
SSD_INNER = SSD_HEADS * SSD_HEAD_DIM
SSD_GROUPS = 4
SSD_STATE = 128
SSD_GN = SSD_GROUPS * SSD_STATE
SSD_XBC = SSD_INNER + 2 * SSD_GN
SSD_CONV = 5
SSD_CHUNK = 128
SSD_GROUP_W = SSD_INNER // SSD_GROUPS

NA_HEADS = 16
NA_WIDTH = NA_HEADS * HEAD_DIM
NA_WIN_ROWS = 8
NA_WIN_COLS = 16
NA_QROWS = 4
NA_KROWS = 12

GQA_HEADS = 16
GQA_KV_HEADS = 4
GQA_GROUP = GQA_HEADS // GQA_KV_HEADS
GQA_Q = GQA_HEADS * HEAD_DIM
GQA_KV = GQA_KV_HEADS * HEAD_DIM
DIFF_HEADS = 8
DIFF_QK = DIFF_HEADS * 2 * HEAD_DIM
DIFF_V_DIM = 2 * HEAD_DIM
DIFF_V = DIFF_HEADS * DIFF_V_DIM
ODD_Q = GQA_Q + DIFF_QK

MOE_GROUPS = 8
MOE_EXPERTS_PER_GROUP = 8
MOE_EXPERTS = MOE_GROUPS * MOE_EXPERTS_PER_GROUP
MOE_HIDDEN = 512
MOE_TOPK = 2
MOE_BLOCK = 128

ROW_BLOCK = 256
LANES = 128
MASK_VALUE = -1e30
VMEM_LIMIT = 52 * 1024 * 1024


def _cparams(*sem):
    return pltpu.CompilerParams(dimension_semantics=sem, vmem_limit_bytes=VMEM_LIMIT)


def _mod_row(i, tile, n_lat, seq, n_batch):
    return jnp.where(i < n_lat // tile, i // (seq // tile), n_batch)


def _ada_kernel(c_ref, w_ref, b_ref, o_ref):
    c = c_ref[...]
    s = c * jax.nn.sigmoid(c)
    o_ref[...] = jnp.dot(s, w_ref[...], precision=HIGHEST, preferred_element_type=F32) + b_ref[...]


def ada_modulation(cc, ada_w, ada_b):
    n_layers, d, n_out = ada_w.shape
    rows = cc.shape[0]
    tn = 1024
    return pl.pallas_call(
        _ada_kernel,
        grid=(n_layers, n_out // tn),
        in_specs=[pl.BlockSpec((rows, d), lambda l, j: (0, 0)),
                  pl.BlockSpec((None, d, tn), lambda l, j: (l, 0, j)),
                  pl.BlockSpec((None, 1, tn), lambda l, j: (l, 0, j))],
        out_specs=pl.BlockSpec((None, rows, tn), lambda l, j: (l, 0, j)),
        out_shape=jax.ShapeDtypeStruct((n_layers, rows, n_out), F32),
        compiler_params=_cparams("parallel", "parallel"),
        name="ada_modulation",
    )(cc, ada_w, ada_b.reshape(n_layers, 1, n_out))


def _normmod_kernel(h_ref, w_ref, sc_ref, sh_ref, *rest, with_router):
    x = h_ref[...]
    y = x * lax.rsqrt(jnp.mean(x * x, axis=-1, keepdims=True) + EPS) * w_ref[...]
    a = y * (1.0 + sc_ref[...]) + sh_ref[...]
    if with_router:
        rw_ref, o_ref, route_ref = rest
        logits = jnp.dot(a, rw_ref[...], precision=HIGHEST, preferred_element_type=F32)
        route_ref[...] = _route(logits)
    else:
        (o_ref,) = rest
    o_ref[...] = a.astype(o_ref.dtype)


ROUTE_EXPERT_LANE = 0
ROUTE_GATE_LANE = MOE_TOPK


def _route(logits):
    assert MOE_TOPK == 2
    neg = jnp.float32(-3.0e38)
    lane = lax.broadcasted_iota(jnp.int32, logits.shape, 1)
    is_group = lane < MOE_GROUPS
    gl = jnp.where(is_group, logits, neg)
    g_max = jnp.max(gl, axis=-1, keepdims=True)
    g_idx = jnp.min(jnp.where(gl == g_max, lane, LANES), axis=-1, keepdims=True)
    g_gate = 1.0 / jnp.sum(jnp.where(is_group, jnp.exp(logits - g_max), 0.0), axis=-1, keepdims=True)
    first = MOE_GROUPS + g_idx * MOE_EXPERTS_PER_GROUP
    in_group = (lane >= first) & (lane < first + MOE_EXPERTS_PER_GROUP)
    el = jnp.where(in_group, logits, neg)
    v1 = jnp.max(el, axis=-1, keepdims=True)
    i1 = jnp.min(jnp.where(in_group & (el == v1), lane, LANES), axis=-1, keepdims=True)
    rest = in_group & (lane != i1)
    el2 = jnp.where(rest, logits, neg)
    v2 = jnp.max(el2, axis=-1, keepdims=True)
    i2 = jnp.min(jnp.where(rest & (el2 == v2), lane, LANES), axis=-1, keepdims=True)
    e21 = jnp.exp(v2 - v1)
    p1 = 1.0 / (1.0 + e21)
    p2 = e21 / (1.0 + e21)
    vals = [(i1 - MOE_GROUPS).astype(F32), (i2 - MOE_GROUPS).astype(F32), p1 * g_gate, p2 * g_gate]
    out = jnp.zeros(logits.shape, F32)
    for k, v in enumerate(vals):
        out = jnp.where(lane == k, v, out)
    return out


def norm_modulate(h, w, mod3, sc_chunk, sh_chunk, n_rows, geom, router_w=None):
    n_lat, seq, n_batch = geom
    d = h.shape[1]
    t = ROW_BLOCK
    row = functools.partial(_mod_row, tile=t, n_lat=n_lat, seq=seq, n_batch=n_batch)
    in_specs = [pl.BlockSpec((t, d), lambda i: (i, 0)),
                pl.BlockSpec((1, d), lambda i: (0, 0)),
                pl.BlockSpec((None, 1, d), lambda i: (row(i), 0, sc_chunk)),
                pl.BlockSpec((None, 1, d), lambda i: (row(i), 0, sh_chunk))]
    args = [h, w.reshape(1, d), mod3, mod3]
    out_specs = pl.BlockSpec((t, d), lambda i: (i, 0))
    out_shape = jax.ShapeDtypeStruct((n_rows, d), BF16)
    if router_w is not None:
        in_specs.append(pl.BlockSpec(router_w.shape, lambda i: (0, 0)))
        args.append(router_w)
        out_specs = [out_specs, pl.BlockSpec((t, router_w.shape[1]), lambda i: (i, 0))]
        out_shape = [jax.ShapeDtypeStruct((n_rows, d), F32),
                     jax.ShapeDtypeStruct((n_rows, router_w.shape[1]), F32)]
    return pl.pallas_call(
        functools.partial(_normmod_kernel, with_router=router_w is not None),
        grid=(n_rows // t,),
        in_specs=in_specs, out_specs=out_specs, out_shape=out_shape,
        compiler_params=_cparams("parallel"),
        name="norm_modulate",
    )(*args)


def _matmul_kernel(*refs, n_in, epilogue):
    acc = None
    for x_ref, w_ref in zip(refs[:n_in], refs[n_in:2 * n_in]):
        part = jnp.dot(x_ref[...], w_ref[...], preferred_element_type=F32)
        acc = part if acc is None else acc + part
    rest = refs[2 * n_in:]
    if epilogue == "residual":
        h_ref, g_ref, o_ref = rest
        o_ref[...] = h_ref[...] + g_ref[...] * acc
    elif epilogue in ("head_norm", "head_norm_rope"):
        w_ref, o_ref = rest[0], rest[-1]
        if epilogue == "head_norm_rope":
            cos, sin = rest[1][...], rest[2][...]
            ri = lax.broadcasted_iota(jnp.int32, (HEAD_DIM, HEAD_DIM), 0)
            ci = lax.broadcasted_iota(jnp.int32, (HEAD_DIM, HEAD_DIM), 1)
            swap = (ri == ci + 1 - 2 * (ci % 2)).astype(BF16)
        for hd in range(acc.shape[1] // HEAD_DIM):
            cols = slice(hd * HEAD_DIM, (hd + 1) * HEAD_DIM)
            y = acc[:, cols]
            y = y * lax.rsqrt(jnp.mean(y * y, axis=-1, keepdims=True) + EPS) * w_ref[:, cols]
            if epilogue == "head_norm_rope":
                partner = jnp.dot(y.astype(BF16), swap, preferred_element_type=F32)
                y = y * cos + partner * sin
            o_ref[:, cols] = y.astype(o_ref.dtype)
    else:
        (o_ref,) = rest
        o_ref[...] = acc.astype(o_ref.dtype)


def matmul(xs, ws, n_rows, n_out, out_dtype, tn, residual=None, head_norm=None):
    tm = next(t for t in (1024, 512, ROW_BLOCK) if n_rows % t == 0)
    in_specs = [pl.BlockSpec((tm, x.shape[1]), lambda i, j: (i, 0)) for x in xs]
    for x, (_, row_off, col_off) in zip(xs, ws):
        k = x.shape[1]
        assert row_off % k == 0 and col_off % tn == 0 and n_out % tn == 0
        in_specs.append(pl.BlockSpec((k, tn), functools.partial(
            lambda i, j, rb, cb: (rb, j + cb), rb=row_off // k, cb=col_off // tn)))
    args = list(xs) + [w for w, _, _ in ws]
    epilogue = "plain"
    if residual is not None:
        epilogue = "residual"
        h, mod3, gate_chunk, (n_lat, seq, n_batch) = residual
        row = functools.partial(_mod_row, tile=tm, n_lat=n_lat, seq=seq, n_batch=n_batch)
        nj = mod3.shape[2] // 6 // tn
        in_specs += [pl.BlockSpec((tm, tn), lambda i, j: (i, j)),
                     pl.BlockSpec((None, 1, tn), lambda i, j: (row(i), 0, gate_chunk * nj + j))]
        args += [h, mod3]
    elif head_norm is not None:
        epilogue = "head_norm"
        w_cols, rope = head_norm
        in_specs.append(pl.BlockSpec((1, tn), lambda i, j: (0, j)))
        args.append(w_cols)
        if rope is not None:
            epilogue = "head_norm_rope"
            cos, sin, seq, n_lat = rope
            per_seq, lat_tiles = seq // tm, n_lat // tm
            tab = pl.BlockSpec((tm, HEAD_DIM), lambda i, j: (jnp.where(i < lat_tiles, i % per_seq, per_seq), 0))
            in_specs += [tab, tab]
            args += [cos, sin]
    return pl.pallas_call(
        functools.partial(_matmul_kernel, n_in=len(xs), epilogue=epilogue),
        grid=(n_rows // tm, n_out // tn),
        in_specs=in_specs,
        out_specs=pl.BlockSpec((tm, tn), lambda i, j: (i, j)),
        out_shape=jax.ShapeDtypeStruct((n_rows, n_out), out_dtype),
        compiler_params=_cparams("parallel", "parallel"),
        name="matmul_" + epilogue,
    )(*args)


CONV_HALO = 16


def _conv_silu_kernel(x_ref, prev_ref, next_ref, w_ref, b_ref, o_ref, *, per_seq, lat_blocks):
    i = pl.program_id(0)
    rows, width = o_ref.shape
    is_lat = i < lat_blocks
    has_prev = is_lat & (i % per_seq != 0)
    has_next = is_lat & (i % per_seq != per_seq - 1)
    x = x_ref[...]
    prev = jnp.where(has_prev, prev_ref[...].astype(F32), 0.0)
    nxt = jnp.where(has_next, next_ref[...].astype(F32), 0.0)
    out_row = lax.broadcasted_iota(jnp.int32, (rows, rows), 0)
    src_row = lax.broadcasted_iota(jnp.int32, (rows, rows), 1)
    edge = 8
    row8 = lax.broadcasted_iota(jnp.int32, (edge, 1), 0)
    acc = jnp.broadcast_to(b_ref[...], (rows, width))
    for k in range(SSD_CONV):
        shift = k - SSD_CONV // 2
        if shift == 0:
            tap = x.astype(F32)
        else:
            tap = jnp.dot((src_row == out_row + shift).astype(x.dtype), x, preferred_element_type=F32)
            if shift < 0:
                fix = tap[:edge]
                for r in range(-shift):
                    fix = jnp.where(row8 == r, prev[CONV_HALO + shift + r:CONV_HALO + shift + r + 1], fix)
                tap = jnp.concatenate([fix, tap[edge:]], axis=0)
            else:
                fix = tap[rows - edge:]
                for r in range(shift):
                    fix = jnp.where(row8 == edge - shift + r, nxt[r:r + 1], fix)
                tap = jnp.concatenate([tap[:rows - edge], fix], axis=0)
        acc = acc + w_ref[pl.ds(k, 1), :] * tap
    o_ref[...] = (acc * jax.nn.sigmoid(acc)).astype(o_ref.dtype)


def conv_silu(p_zx, conv_w, conv_b, n_lat, seq, ctx_len):
    t, tc = ROW_BLOCK, 1024
    assert ctx_len == t and seq % t == 0
    n_rows = p_zx.shape[0]
    col0 = SSD_INNER // tc
    halo_per_block = t // CONV_HALO
    last_halo = n_rows // CONV_HALO - 1
    return pl.pallas_call(
        functools.partial(_conv_silu_kernel, per_seq=seq // t, lat_blocks=n_lat // t),
        grid=(n_rows // t, SSD_XBC // tc),
        in_specs=[pl.BlockSpec((t, tc), lambda i, j: (i, col0 + j)),
                  pl.BlockSpec((CONV_HALO, tc), lambda i, j: (jnp.maximum(i * halo_per_block - 1, 0), col0 + j)),
                  pl.BlockSpec((CONV_HALO, tc),
                               lambda i, j: (jnp.minimum((i + 1) * halo_per_block, last_halo), col0 + j)),
                  pl.BlockSpec((SSD_CONV, tc), lambda i, j: (0, j)),
                  pl.BlockSpec((1, tc), lambda i, j: (0, j))],
        out_specs=pl.BlockSpec((t, tc), lambda i, j: (i, j)),
        out_shape=jax.ShapeDtypeStruct((n_rows, SSD_XBC), BF16),
        compiler_params=_cparams("parallel", "parallel"),
        name="conv_silu",
    )(p_zx, p_zx, p_zx, conv_w, conv_b.reshape(1, SSD_XBC))


SSD_DT_CHUNKS = 4


def _ssd_dt_kernel(p_ref, bias_ref, a_ref, dt_ref, acs_ref, acst_ref):
    n = SSD_CHUNK
    li = lax.broadcasted_iota(jnp.int32, (n, n), 0)
    si = lax.broadcasted_iota(jnp.int32, (n, n), 1)
    lower = (si <= li).astype(F32)
    upper = (si >= li).astype(F32)
    lane = lax.broadcasted_iota(jnp.int32, (n, LANES), 1)
    for c in range(p_ref.shape[0] // n):
        rows = pl.ds(c * n, n)
        pre = p_ref[rows, :] + bias_ref[...]
        dt = jnp.maximum(pre, 0.0) + jnp.log1p(jnp.exp(-jnp.abs(pre)))
        a = dt * a_ref[...]
        fwd = jnp.dot(lower, a, precision=HIGHEST, preferred_element_type=F32)
        bwd = jnp.dot(upper, a, precision=HIGHEST, preferred_element_type=F32)
        acs = jnp.where(lane < SSD_HEADS, fwd, bwd)
        dt_ref[rows, :] = dt
        acs_ref[rows, :] = acs
        acst_ref[:, rows] = acs.T


def ssd_dt(p_dt, dt_bias, a_log):
    rows = p_dt.shape[0]
    pad = LANES - 2 * SSD_HEADS
    bias = jnp.pad(dt_bias.reshape(1, 2 * SSD_HEADS), ((0, 0), (0, pad)))
    a_neg = jnp.pad(-jnp.exp(a_log.astype(F32)).reshape(1, 2 * SSD_HEADS), ((0, 0), (0, pad)))
    c = SSD_CHUNK * SSD_DT_CHUNKS
    assert rows % c == 0
    return pl.pallas_call(
        _ssd_dt_kernel,
        grid=(rows // c,),
        in_specs=[pl.BlockSpec((c, LANES), lambda i: (i, 0)),
                  pl.BlockSpec((1, LANES), lambda i: (0, 0)),
                  pl.BlockSpec((1, LANES), lambda i: (0, 0))],
        out_specs=[pl.BlockSpec((c, LANES), lambda i: (i, 0)),
                   pl.BlockSpec((c, LANES), lambda i: (i, 0)),
                   pl.BlockSpec((LANES, c), lambda i: (0, i))],
        out_shape=[jax.ShapeDtypeStruct((rows, LANES), F32),
                   jax.ShapeDtypeStruct((rows, LANES), F32),
                   jax.ShapeDtypeStruct((LANES, rows), F32)],
        compiler_params=_cparams("parallel"),
        name="ssd_dt",
    )(p_dt, bias, a_neg)


def _ssd_scan_kernel(*refs):
    fwd, bwd, outs = refs[:6], refs[6:12], refs[12:]

    @pl.when(pl.program_id(2) == 0)
    def _():
        for state_ref in outs[2:]:
            state_ref[...] = jnp.zeros_like(state_ref)

    _ssd_chunk(*fwd, outs[0], outs[2], reverse=False)
    _ssd_chunk(*bwd, outs[1], outs[3], reverse=True)


def _ssd_chunk(x_ref, b_ref, c_ref, dt_ref, acs_ref, acsr_ref, y_ref, state_ref, *, reverse):
    n = x_ref.shape[0]
    heads = dt_ref.shape[1]
    pair_w = 2 * SSD_HEAD_DIM
    x = x_ref[...].astype(F32)
    bm = b_ref[...]
    cm = c_ref[...]
    dt = dt_ref[...]
    acs = acs_ref[...]
    acs_row = acsr_ref[...]
    end = 0 if reverse else n - 1
    total = acs[end:end + 1, :]
    e_tot = jnp.exp(total)

    li = lax.broadcasted_iota(jnp.int32, (n, n), 0)
    si = lax.broadcasted_iota(jnp.int32, (n, n), 1)
    visible = (si >= li) if reverse else (si <= li)
    lo = lax.broadcasted_iota(jnp.int32, (1, pair_w), 1) < SSD_HEAD_DIM

    cb = lax.dot_general(cm, bm, (((1,), (1,)), ((), ())), preferred_element_type=F32)
    state = state_ref[...]
    y_off = jnp.dot(cm, state.astype(BF16), preferred_element_type=F32)

    xd_parts, decay_parts = [], []
    for i in range(heads // 2):
        j0, j1 = 2 * i, 2 * i + 1
        cols = slice(i * pair_w, (i + 1) * pair_w)
        xp = x[:, cols] * jnp.where(lo, dt[:, j0:j0 + 1], dt[:, j1:j1 + 1])
        ms, xs, acs_b = [], [], []
        for j, keep in ((j0, lo), (j1, jnp.logical_not(lo))):
            acs_b.append(jnp.broadcast_to(acs[:, j:j + 1], (n, n)))
            diff = acs_b[-1] - acs_row[j:j + 1, :]
            ms.append((cb * jnp.exp(jnp.where(visible, diff, MASK_VALUE))).astype(BF16))
            xs.append(jnp.where(keep, xp, 0.0).astype(BF16))
        y_diag = jnp.dot(jnp.concatenate(ms, axis=1), jnp.concatenate(xs, axis=0), preferred_element_type=F32)
        acs_pair = jnp.where(lo, acs_b[0], acs_b[1])
        tot_pair = jnp.where(lo, total[:, j0:j0 + 1], total[:, j1:j1 + 1])
        y_ref[:, cols] = (y_diag + y_off[:, cols] * jnp.exp(acs_pair)).astype(y_ref.dtype)
        xd_parts.append((xp * jnp.exp(tot_pair - acs_pair)).astype(BF16))
        decay_parts.append(jnp.where(lo, e_tot[:, j0:j0 + 1], e_tot[:, j1:j1 + 1]))
    xd = jnp.concatenate(xd_parts, axis=1)
    decay = jnp.concatenate(decay_parts, axis=1)
    bt = bm.astype(F32).T.astype(BF16)
    state_ref[...] = state * decay + jnp.dot(bt, xd, preferred_element_type=F32)


def ssd_scan(xbc, dt_col, acs_col, acs_row, n_batch, seq, ctx_len):
    c = SSD_CHUNK
    lat_chunks, ctx_chunks = seq // c, ctx_len // c
    ctx_base = n_batch * lat_chunks
    n_steps = lat_chunks + ctx_chunks
    gw = SSD_GROUP_W
    heads = SSD_HEADS // SSD_GROUPS
    b_col0 = SSD_INNER // SSD_STATE
    c_col0 = (SSD_INNER + SSD_GN) // SSD_STATE

    def specs(reverse):
        def chunk(b, k):
            kc = (ctx_chunks - 1 - k) if reverse else k
            kl = (n_steps - 1 - k) if reverse else (k - ctx_chunks)
            return jnp.where(k < ctx_chunks, ctx_base + b * ctx_chunks + kc, b * lat_chunks + kl)

        d = int(reverse)
        ins = [pl.BlockSpec((c, gw), lambda b, g, k: (chunk(b, k), g)),
               pl.BlockSpec((c, SSD_STATE), lambda b, g, k: (chunk(b, k), b_col0 + g)),
               pl.BlockSpec((c, SSD_STATE), lambda b, g, k: (chunk(b, k), c_col0 + g)),
               pl.BlockSpec((None, None, c, heads), lambda b, g, k: (d, g, chunk(b, k), 0)),
               pl.BlockSpec((None, None, c, heads), lambda b, g, k: (d, g, chunk(b, k), 0)),
               pl.BlockSpec((None, None, heads, c), lambda b, g, k: (d, g, 0, chunk(b, k)))]
        return ins, pl.BlockSpec((c, gw), lambda b, g, k: (chunk(b, k), g))

    (in_f, out_f), (in_b, out_b) = specs(False), specs(True)
    y_shape = jax.ShapeDtypeStruct((xbc.shape[0], SSD_INNER), BF16)
    args = (xbc, xbc, xbc, dt_col, acs_col, acs_row)
    return pl.pallas_call(
        _ssd_scan_kernel,
        grid=(n_batch, SSD_GROUPS, n_steps),
        in_specs=in_f + in_b,
        out_specs=[out_f, out_b],
        out_shape=[y_shape, y_shape],
        scratch_shapes=[pltpu.VMEM((SSD_STATE, gw), F32), pltpu.VMEM((SSD_STATE, gw), F32)],
        compiler_params=_cparams("parallel", "parallel", "arbitrary"),
        name="ssd_scan",
    )(*args, *args)


def _ssd_gate_kernel(yf_ref, yb_ref, x_ref, z_ref, d_ref, w_ref, o_ref):
    z = z_ref[...].astype(F32)
    y = yf_ref[...].astype(F32) + yb_ref[...].astype(F32) + d_ref[...] * x_ref[...].astype(F32)
    g = y * (z * jax.nn.sigmoid(z))
    o = g * lax.rsqrt(jnp.mean(g * g, axis=-1, keepdims=True) + EPS) * w_ref[...]
    o_ref[...] = o.astype(o_ref.dtype)


def ssd_gate(y_f, y_b, xbc, p_zx, d_skip, norm_w):
    rows = y_f.shape[0]
    t, gw = ROW_BLOCK, SSD_GROUP_W
    d_cols = jnp.repeat(d_skip.astype(F32), SSD_HEAD_DIM).reshape(1, SSD_INNER)
    spec = pl.BlockSpec((t, gw), lambda i, g: (i, g))
    vec = pl.BlockSpec((1, gw), lambda i, g: (0, g))
    return pl.pallas_call(
        _ssd_gate_kernel,
        grid=(rows // t, SSD_GROUPS),
        in_specs=[spec, spec, spec, spec, vec, vec],
        out_specs=spec,
        out_shape=jax.ShapeDtypeStruct((rows, SSD_INNER), BF16),
        compiler_params=_cparams("parallel", "parallel"),
        name="ssd_gate",
    )(y_f, y_b, xbc, p_zx, d_cols, norm_w.reshape(1, SSD_INNER))


ROPE_IDENTITY_ROWS = 1024


def _rope_tables(seq):
    t = jnp.arange(seq, dtype=jnp.int32)
    row = (t // GRID_W).astype(F32)
    col = (t % GRID_W).astype(F32)
    axis_dim = HEAD_DIM // 2
    inv = 1.0 / (ROPE_THETA ** (jnp.arange(0, axis_dim, 2, dtype=F32) / axis_dim))
    ang = jnp.concatenate([row[:, None] * inv[None], col[:, None] * inv[None]], axis=-1)
    cos = jnp.repeat(jnp.cos(ang), 2, axis=-1)
    sin = jnp.stack([-jnp.sin(ang), jnp.sin(ang)], axis=-1).reshape(seq, HEAD_DIM)
    identity = jnp.ones((ROPE_IDENTITY_ROWS, HEAD_DIM), F32)
    return jnp.concatenate([cos, identity], axis=0), jnp.concatenate([sin, 0.0 * identity], axis=0)


LOG2E = math.log2(math.e)
KEY_CHUNK = 256
DIFF_Q_TILE = 1024
GQA_Q_TILE = 512


def _scores(q, k, bias):
    s = lax.dot_general(q, k, (((1,), (1,)), ((), ())), preferred_element_type=F32)
    return s if bias is None else s + bias


def _attend_two_pass(q, pieces):
    scores = [_scores(q, k, bias) for k, _, bias in pieces]
    m = functools.reduce(jnp.maximum, [jnp.max(s, axis=-1, keepdims=True) for s in scores])
    denom = acc = None
    for s, (_, v, _) in zip(scores, pieces):
        p = jnp.exp2(s - m)
        p_sum = jnp.sum(p, axis=-1, keepdims=True)
        pv = jnp.dot(p.astype(v.dtype), v, preferred_element_type=F32)
        denom = p_sum if denom is None else denom + p_sum
        acc = pv if acc is None else acc + pv
    return acc / denom


def _attend(q, pieces):
    chunks = []
    for k, v, bias in pieces:
        for c0 in range(0, k.shape[0], KEY_CHUNK):
            c1 = min(c0 + KEY_CHUNK, k.shape[0])
            chunks.append((k[c0:c1], v[c0:c1], None if bias is None else bias[:, c0:c1]))
    m = denom = acc = None
    for k, v, bias in chunks:
        s = lax.dot_general(q, k, (((1,), (1,)), ((), ())), preferred_element_type=F32)
        if bias is not None:
            s = s + bias
        m_chunk = jnp.max(s, axis=-1, keepdims=True)
        m_new = m_chunk if m is None else jnp.maximum(m, m_chunk)
        p = jnp.exp2(s - m_new)
        p_sum = jnp.sum(p, axis=-1, keepdims=True)
        pv = jnp.dot(p.astype(v.dtype), v, preferred_element_type=F32)
        if m is None:
            denom, acc = p_sum, pv
        else:
            alpha = jnp.exp2(m - m_new)
            denom = alpha * denom + p_sum
            acc = alpha * acc + pv
        m = m_new
    return acc / denom


NA_HEADS_PER_STEP = 4


def _na_kernel(q_ref, k0_ref, k1_ref, k2_ref, kc_ref, v0_ref, v1_ref, v2_ref, vc_ref, bias_ref, o_ref, *,
               n_win_blocks):
    t = q_ref.shape[0]
    is_window_step = pl.program_id(1) < n_win_blocks

    @pl.when(is_window_step)
    def _():
        for hd in range(NA_HEADS_PER_STEP):
            cols = slice(hd * HEAD_DIM, (hd + 1) * HEAD_DIM)
            pieces = [(k_ref[:, cols], v_ref[:, cols], bias_ref[hd, :, i * t:(i + 1) * t])
                      for i, (k_ref, v_ref) in enumerate(((k0_ref, v0_ref), (k1_ref, v1_ref), (k2_ref, v2_ref)))]
            pieces.append((kc_ref[:, cols], vc_ref[:, cols], None))
            o_ref[:, cols] = _attend_two_pass(q_ref[:, cols], pieces).astype(o_ref.dtype)

    @pl.when(jnp.logical_not(is_window_step))
    def _():
        for hd in range(NA_HEADS_PER_STEP):
            cols = slice(hd * HEAD_DIM, (hd + 1) * HEAD_DIM)
            o_ref[:, cols] = _attend_two_pass(q_ref[:, cols], [(kc_ref[:, cols], vc_ref[:, cols], None)]
                                              ).astype(o_ref.dtype)


def _na_bias_table(rpb, rows):
    n_blocks = rows // NA_QROWS
    w0_of = lambda j: min(max(NA_QROWS * j - NA_WIN_ROWS // 2, 0), rows - NA_KROWS)
    geoms, pattern_of = [], []
    for j in range(n_blocks):
        r0s = tuple(min(max(NA_QROWS * j + a - NA_WIN_ROWS // 2, 0), rows - NA_WIN_ROWS) - w0_of(j)
                    for a in range(NA_QROWS))
        key = (NA_QROWS * j - w0_of(j), r0s)
        if key not in geoms:
            geoms.append(key)
        pattern_of.append(geoms.index(key))
    qc = jnp.arange(GRID_W)[:, None]
    kc = jnp.arange(GRID_W)[None, :]
    cs = jnp.clip(qc - NA_WIN_COLS // 2, 0, GRID_W - NA_WIN_COLS)
    col_ok = (kc >= cs) & (kc < cs + NA_WIN_COLS)
    dc = jnp.clip(kc - qc + NA_WIN_COLS - 1, 0, 2 * NA_WIN_COLS - 2)
    pick = (dc[None] == jnp.arange(2 * NA_WIN_COLS - 1)[:, None, None]).astype(F32)
    tiles = jnp.einsum('hrd,dqk->hrqk', rpb.astype(F32), pick, precision=HIGHEST)
    tiles = jnp.where(col_ok, tiles * LOG2E, MASK_VALUE)
    masked = jnp.full((rpb.shape[0], GRID_W, GRID_W), MASK_VALUE, F32)
    tables = []
    for q_row0, r0s in geoms:
        tile_rows = []
        for qi in range(NA_QROWS):
            row = [tiles[:, ki - (q_row0 + qi) + NA_WIN_ROWS - 1] if r0s[qi] <= ki < r0s[qi] + NA_WIN_ROWS
                   else masked for ki in range(NA_KROWS)]
            tile_rows.append(jnp.concatenate(row, axis=2))
        tables.append(jnp.concatenate(tile_rows, axis=1))
    pattern_of.append(0)
    w0_blocks = [w0_of(j) // NA_QROWS for j in range(n_blocks)] + [0]
    return jnp.stack(tables, axis=1), pattern_of, w0_blocks


def _lookup(values, j):
    out = jnp.int32(values[-1])
    for idx in range(len(values) - 2, -1, -1):
        out = jnp.where(j == idx, jnp.int32(values[idx]), out)
    return out


def na_attention(qn, kn, p_na, v_col_off, rpb, n_batch, seq, ctx_len):
    t = NA_QROWS * GRID_W
    assert t == ROW_BLOCK == ctx_len
    rows = seq // GRID_W
    n_blocks = seq // t
    bias, pattern_of, w0_blocks = _na_bias_table(rpb, rows)
    hw = NA_HEADS_PER_STEP * HEAD_DIM
    v_col0 = v_col_off // hw
    ctx_base = n_batch * n_blocks

    def q_blk(j, b):
        return jnp.where(j < n_blocks, b * n_blocks + j, ctx_base + b)

    def kv_blk(piece):
        return lambda h, j, b: (b * n_blocks + _lookup(w0_blocks, j) + piece, h)

    def kv_blk_v(piece):
        return lambda h, j, b: (b * n_blocks + _lookup(w0_blocks, j) + piece, v_col0 + h)

    blk = lambda fn: pl.BlockSpec((t, hw), fn)
    in_specs = ([blk(lambda h, j, b: (q_blk(j, b), h))]
                + [blk(kv_blk(i)) for i in range(3)] + [blk(lambda h, j, b: (ctx_base + b, h))]
                + [blk(kv_blk_v(i)) for i in range(3)] + [blk(lambda h, j, b: (ctx_base + b, v_col0 + h))]
                + [pl.BlockSpec((NA_HEADS_PER_STEP, None, t, 3 * t),
                                lambda h, j, b: (h, _lookup(pattern_of, j), 0, 0))])
    return pl.pallas_call(
        functools.partial(_na_kernel, n_win_blocks=n_blocks),
        grid=(NA_HEADS // NA_HEADS_PER_STEP, n_blocks + 1, n_batch),
        in_specs=in_specs,
        out_specs=blk(lambda h, j, b: (q_blk(j, b), h)),
        out_shape=jax.ShapeDtypeStruct((qn.shape[0], NA_WIDTH), BF16),
        compiler_params=_cparams("parallel", "parallel", "parallel"),
        name="na_attention",
    )(qn, kn, kn, kn, kn, p_na, p_na, p_na, p_na, bias)


def _gqa_kernel(q_ref, kl_ref, kc_ref, vl_ref, vc_ref, o_ref):
    t = q_ref.shape[0]
    q = jnp.concatenate([q_ref[:, g * HEAD_DIM:(g + 1) * HEAD_DIM] for g in range(GQA_GROUP)], axis=0)
    o = _attend(q, [(kc_ref[...], vc_ref[...], None), (kl_ref[...], vl_ref[...], None)])
    for g in range(GQA_GROUP):
        o_ref[:, g * HEAD_DIM:(g + 1) * HEAD_DIM] = o[g * t:(g + 1) * t].astype(o_ref.dtype)


def gqa_attention(qn, kn, p_kv, v_col_off, n_batch, seq, ctx_len):
    t = GQA_Q_TILE
    q_tiles = seq // t
    ctx_base = n_batch * seq // ctx_len
    v0 = v_col_off // HEAD_DIM
    gw = GQA_GROUP * HEAD_DIM
    return pl.pallas_call(
        _gqa_kernel,
        grid=(n_batch, GQA_KV_HEADS, q_tiles),
        in_specs=[pl.BlockSpec((t, gw), lambda b, h, i: (b * q_tiles + i, h)),
                  pl.BlockSpec((seq, HEAD_DIM), lambda b, h, i: (b, h)),
                  pl.BlockSpec((ctx_len, HEAD_DIM), lambda b, h, i: (ctx_base + b, h)),
                  pl.BlockSpec((seq, HEAD_DIM), lambda b, h, i: (b, v0 + h)),
                  pl.BlockSpec((ctx_len, HEAD_DIM), lambda b, h, i: (ctx_base + b, v0 + h))],
        out_specs=pl.BlockSpec((t, gw), lambda b, h, i: (b * q_tiles + i, h)),
        out_shape=jax.ShapeDtypeStruct((n_batch * seq, GQA_Q), BF16),
        compiler_params=_cparams("parallel", "parallel", "parallel"),
        name="gqa_attention",
    )(qn, kn, kn, p_kv, p_kv)


def _diff_kernel(lam_ref, q_ref, kl_ref, kc_ref, vl_ref, vc_ref, w_ref, o_ref, *, out_scale):
    outs = []
    for comp in range(2):
        cols = slice(comp * HEAD_DIM, (comp + 1) * HEAD_DIM)
        outs.append(_attend(q_ref[:, cols], [(kc_ref[:, cols], vc_ref[...], None),
                                              (kl_ref[:, cols], vl_ref[...], None)]))
    o = outs[0] - lam_ref[0] * outs[1]
    o = o * lax.rsqrt(jnp.mean(o * o, axis=-1, keepdims=True) + EPS) * w_ref[...]
    o_ref[...] = (o * out_scale).astype(o_ref.dtype)


def diff_attention(qn, q_col_off, kn, p_kv, v_col_off, lam, subln_w, out_scale, n_batch, seq, ctx_len):
    t = DIFF_Q_TILE
    q_tiles = seq // t
    ctx_base = n_batch * seq // ctx_len
    v0 = v_col_off // DIFF_V_DIM
    w2 = 2 * HEAD_DIM
    q0 = q_col_off // w2
    return pl.pallas_call(
        functools.partial(_diff_kernel, out_scale=out_scale),
        grid=(n_batch, DIFF_HEADS, q_tiles),
        in_specs=[pl.BlockSpec(memory_space=pltpu.SMEM),
                  pl.BlockSpec((t, w2), lambda b, h, i: (b * q_tiles + i, q0 + h)),
                  pl.BlockSpec((seq, w2), lambda b, h, i: (b, h)),
                  pl.BlockSpec((ctx_len, w2), lambda b, h, i: (ctx_base + b, h)),
                  pl.BlockSpec((seq, DIFF_V_DIM), lambda b, h, i: (b, v0 + h)),
                  pl.BlockSpec((ctx_len, DIFF_V_DIM), lambda b, h, i: (ctx_base + b, v0 + h)),
                  pl.BlockSpec((1, DIFF_V_DIM), lambda b, h, i: (0, 0))],
        out_specs=pl.BlockSpec((t, DIFF_V_DIM), lambda b, h, i: (b * q_tiles + i, h)),
        out_shape=jax.ShapeDtypeStruct((n_batch * seq, DIFF_V), BF16),
        compiler_params=_cparams("parallel", "parallel", "parallel"),
        name="diff_attention",
    )(lam.reshape(1), qn, kn, kn, p_kv, p_kv, subln_w.reshape(1, DIFF_V_DIM))


def _moe_plan_kernel(route_ref, pos_ref, pad_end_ref, slot_ref, counts_ref, start_ref):
    phase, i = pl.program_id(0), pl.program_id(1)
    t = route_ref.shape[0]
    route = route_ref[...]
    lane = lax.broadcasted_iota(jnp.int32, route.shape, 1)
    chosen = [lane == route[:, ROUTE_EXPERT_LANE + k:ROUTE_EXPERT_LANE + k + 1].astype(jnp.int32)
              for k in range(MOE_TOPK)]
    both = functools.reduce(jnp.add, [c.astype(F32) for c in chosen])
    tile_counts = jnp.sum(both, axis=0, keepdims=True)

    @pl.when((phase == 0) & (i == 0))
    def _():
        counts_ref[...] = jnp.zeros_like(counts_ref)

    @pl.when(phase == 0)
    def _():
        counts_ref[...] += tile_counts

    @pl.when((phase == 1) & (i == 0))
    def _():
        padded = jnp.ceil(counts_ref[...] * (1.0 / MOE_BLOCK)) * MOE_BLOCK
        rows = pad_end_ref.shape[0]
        ki = lax.broadcasted_iota(jnp.int32, (LANES, LANES), 0)
        ni = lax.broadcasted_iota(jnp.int32, (LANES, LANES), 1)
        pad_end = jnp.dot(jnp.broadcast_to(padded, (rows, LANES)), (ki <= ni).astype(F32),
                          precision=HIGHEST, preferred_element_type=F32)
        pad_end_ref[...] = pad_end
        start_ref[...] = pad_end[0:1] - padded
        slot_ref[...] = jnp.zeros_like(slot_ref)

    @pl.when(phase == 1)
    def _():
        li = lax.broadcasted_iota(jnp.int32, (t, t), 0)
        si = lax.broadcasted_iota(jnp.int32, (t, t), 1)
        earlier = jnp.dot((si < li).astype(BF16), both.astype(BF16), preferred_element_type=F32)
        row_of = earlier + start_ref[...]
        out = jnp.zeros(route.shape, F32)
        token = (i * t + 1 + lax.broadcasted_iota(jnp.int32, (t, 1), 0)).astype(F32)
        blk_lane = lax.broadcasted_iota(jnp.int32, (t, slot_ref.shape[0]), 1)
        placed = jnp.zeros(slot_ref.shape, F32)
        for k in range(MOE_TOPK):
            p = jnp.sum(jnp.where(chosen[k], row_of, 0.0), axis=-1, keepdims=True)
            out = jnp.where(lane == k, p, out)
            blk = jnp.floor(p * (1.0 / MOE_BLOCK))
            in_blk = (p - blk * MOE_BLOCK).astype(jnp.int32)
            block_hot = (blk_lane == blk.astype(jnp.int32)).astype(F32)
            row_hot = jnp.where(lane == in_blk, token, 0.0)
            placed = placed + jnp.dot(block_hot.T, row_hot, precision=HIGHEST, preferred_element_type=F32)
        slot_ref[...] += placed
        pos_ref[...] = out.astype(jnp.int32)
        start_ref[...] += tile_counts


def moe_plan(route, n_blocks):
    n = route.shape[0]
    t = ROW_BLOCK
    blocks_pad = -(-n_blocks // LANES) * LANES
    return pl.pallas_call(
        _moe_plan_kernel,
        grid=(2, n // t),
        in_specs=[pl.BlockSpec((t, LANES), lambda p, i: (i, 0))],
        out_specs=[pl.BlockSpec((t, LANES), lambda p, i: (i * p, 0)),
                   pl.BlockSpec((8, LANES), lambda p, i: (0, 0)),
                   pl.BlockSpec((blocks_pad, MOE_BLOCK), lambda p, i: (0, 0))],
        out_shape=[jax.ShapeDtypeStruct((n, LANES), jnp.int32),
                   jax.ShapeDtypeStruct((8, LANES), F32),
                   jax.ShapeDtypeStruct((blocks_pad, MOE_BLOCK), F32)],
        scratch_shapes=[pltpu.VMEM((1, LANES), F32), pltpu.VMEM((1, LANES), F32)],
        compiler_params=_cparams("arbitrary", "arbitrary"),
        name="moe_plan",
    )(route)


MOE_DMA_CHUNKS = 4


def _moe_kernel(be_ref, next_ref, seg_ref, nused_ref, x_ref, w1_hbm, w3_hbm, w2_hbm, o_ref,
                s1, s3, s2, w1b, w3b, w2b, sems, *, layer):
    i = pl.program_id(0)
    slot = seg_ref[i] % 2

    def weight_copies(e, slot):
        copies = []
        for n, (w_hbm, stage) in enumerate(((w1_hbm, s1), (w3_hbm, s3), (w2_hbm, s2))):
            rows = stage.shape[1] // MOE_DMA_CHUNKS
            for c in range(MOE_DMA_CHUNKS):
                slab = pl.ds(c * rows, rows)
                copies.append(pltpu.make_async_copy(w_hbm.at[layer, e, slab], stage.at[slot, slab],
                                                    sems.at[slot, n, c]))
        return copies

    @pl.when(i == 0)
    def _():
        for c in weight_copies(be_ref[0], slot):
            c.start()

    prev = be_ref[jnp.maximum(i - 1, 0)]

    @pl.when((i == 0) | (be_ref[i] != prev))
    def _():
        for c in weight_copies(be_ref[i], slot):
            c.wait()

        @pl.when(next_ref[i] >= 0)
        def _():
            for c in weight_copies(next_ref[i], 1 - slot):
                c.start()

        w1b[...] = s1[slot].astype(BF16)
        w3b[...] = s3[slot].astype(BF16)
        w2b[...] = s2[slot].astype(BF16)

    @pl.when(i < nused_ref[0])
    def _():
        x = x_ref[...].astype(BF16)
        a = jnp.dot(x, w1b[...], preferred_element_type=F32)
        b = jnp.dot(x, w3b[...], preferred_element_type=F32)
        hid = (a * jax.nn.sigmoid(a)) * b
        o_ref[...] = jnp.dot(hid.astype(BF16), w2b[...], preferred_element_type=F32)

    @pl.when(i >= nused_ref[0])
    def _():
        o_ref[...] = jnp.zeros_like(o_ref)


def moe_experts(x_sorted, block_e, next_e, n_used, w1, w3, w2, layer):
    n_blocks = block_e.shape[0]
    d, hid = w1.shape[2], w1.shape[3]
    hbm = pl.BlockSpec(memory_space=pl.ANY)
    changed = jnp.concatenate([jnp.zeros((1,), jnp.int32), (block_e[1:] != block_e[:-1]).astype(jnp.int32)])
    segment = jnp.cumsum(changed, dtype=jnp.int32)
    row_spec = pl.BlockSpec((MOE_BLOCK, d), lambda i, be, nx, sg, nu: (i, 0))
    x_spec = pl.BlockSpec((MOE_BLOCK, d), lambda i, be, nx, sg, nu: (jnp.minimum(i, nu[0] - 1), 0))
    grid_spec = pltpu.PrefetchScalarGridSpec(
        num_scalar_prefetch=4,
        grid=(n_blocks,),
        in_specs=[x_spec, hbm, hbm, hbm],
        out_specs=row_spec,
        scratch_shapes=[pltpu.VMEM((2, d, hid), F32), pltpu.VMEM((2, d, hid), F32), pltpu.VMEM((2, hid, d), F32),
                        pltpu.VMEM((d, hid), BF16), pltpu.VMEM((d, hid), BF16), pltpu.VMEM((hid, d), BF16),
                        pltpu.SemaphoreType.DMA((2, 3, MOE_DMA_CHUNKS))],
    )
    return pl.pallas_call(
        functools.partial(_moe_kernel, layer=layer),
        grid_spec=grid_spec,
        out_shape=jax.ShapeDtypeStruct((n_blocks * MOE_BLOCK, d), F32),
        compiler_params=_cparams("arbitrary"),
        name="moe_experts",
    )(block_e, next_e, segment, n_used, x_sorted, w1, w3, w2)


def _combine_kernel(h_ref, g_ref, route_ref, *refs, n_y, next_norm):
    y_refs, rest = refs[:n_y], refs[n_y:]
    acc = None
    for k, y_ref in enumerate(y_refs):
        part = y_ref[...] * route_ref[:, ROUTE_GATE_LANE + k:ROUTE_GATE_LANE + k + 1]
        acc = part if acc is None else acc + part
    h = h_ref[...] + g_ref[...] * acc
    if next_norm:
        w_ref, sc_ref, sh_ref, o_ref, a_ref = rest
        y = h * lax.rsqrt(jnp.mean(h * h, axis=-1, keepdims=True) + EPS) * w_ref[...]
        a_ref[...] = (y * (1.0 + sc_ref[...]) + sh_ref[...]).astype(a_ref.dtype)
    else:
        (o_ref,) = rest
    o_ref[...] = h


def moe_combine(h, mod3, gate_chunk, route, ys, geom, next_norm=None):
    n_lat, seq, n_batch = geom
    n_rows, d = ys[0].shape
    t = ROW_BLOCK
    row = functools.partial(_mod_row, tile=t, n_lat=n_lat, seq=seq, n_batch=n_batch)
    spec = pl.BlockSpec((t, d), lambda i: (i, 0))
    in_specs = [spec, pl.BlockSpec((None, 1, d), lambda i: (row(i), 0, gate_chunk)),
                pl.BlockSpec((t, LANES), lambda i: (i, 0))] + [spec] * len(ys)
    args = [h, mod3, route, *ys]
    out_specs, out_shape = spec, jax.ShapeDtypeStruct((n_rows, d), F32)
    if next_norm is not None:
        w, mod3_next, sc_chunk, sh_chunk = next_norm
        in_specs += [pl.BlockSpec((1, d), lambda i: (0, 0)),
                     pl.BlockSpec((None, 1, d), lambda i: (row(i), 0, sc_chunk)),
                     pl.BlockSpec((None, 1, d), lambda i: (row(i), 0, sh_chunk))]
        args += [w.reshape(1, d), mod3_next, mod3_next]
        out_specs, out_shape = [spec, spec], [out_shape, jax.ShapeDtypeStruct((n_rows, d), BF16)]
    return pl.pallas_call(
        functools.partial(_combine_kernel, n_y=len(ys), next_norm=next_norm is not None),
        grid=(n_rows // t,),
        in_specs=in_specs, out_specs=out_specs, out_shape=out_shape,
        compiler_params=_cparams("parallel"),
        name="moe_combine",
    )(*args)


def moe_layer(h, f, route, mod3, gate_chunk, w1, w3, w2, layer, geom, next_norm=None):
    n = f.shape[0]
    n_assign = n * MOE_TOPK
    n_blocks = -(-n_assign // MOE_BLOCK) + MOE_EXPERTS
    pos, pad_end, slot = moe_plan(route, n_blocks)
    pos = pos[:, :MOE_TOPK]
    pad_end = pad_end[0, :MOE_EXPERTS].astype(jnp.int32)
    n_used = pad_end[-1] // MOE_BLOCK
    blk = jnp.minimum(jnp.arange(n_blocks, dtype=jnp.int32), n_used - 1) * MOE_BLOCK
    block_e = jnp.minimum(jnp.sum(pad_end[None, :] <= blk[:, None], axis=1, dtype=jnp.int32), MOE_EXPERTS - 1)
    seg_end = pad_end[block_e] // MOE_BLOCK
    next_e = jnp.where(seg_end < n_used, block_e[jnp.minimum(seg_end, n_blocks - 1)], -1)
    slot = slot[:n_blocks].reshape(-1).astype(jnp.int32)
    slot_tok = jnp.where(slot > 0, slot - 1, jnp.arange(n_blocks * MOE_BLOCK, dtype=jnp.int32) % n)
    x_sorted = f.at[slot_tok].get(mode='promise_in_bounds')
    yb = moe_experts(x_sorted, block_e, next_e, n_used.reshape(1), w1, w3, w2, layer)
    ys = [yb.at[pos[:, k]].get(mode='promise_in_bounds') for k in range(MOE_TOPK)]
    return moe_combine(h, mod3, gate_chunk, route, ys, geom, next_norm)


def _even_layer_mix(a, in_w, conv_w, conv_b, dt_bias, a_log, d_skip, ssd_norm_w, q_norm, k_norm, rpb,
                    n_batch, seq, ctx_len):
    n_rows = a.shape[0]
    n_lat = n_batch * seq
    zx_end = SSD_INNER + SSD_XBC
    dt_end = zx_end + 2 * SSD_HEADS
    w = jnp.concatenate([in_w[:, :zx_end], in_w[:, dt_end:],
                         jnp.pad(in_w[:, zx_end:dt_end], ((0, 0), (0, LANES - 2 * SSD_HEADS)))],
                        axis=1).astype(BF16)
    na_off = lambda i: zx_end + i * NA_WIDTH
    dt_off = zx_end + 3 * NA_WIDTH
    tile_w = lambda v, s: (jnp.tile(v.astype(F32), NA_HEADS) * s).reshape(1, NA_WIDTH)
    p_zx = matmul([a], [(w, 0, 0)], n_rows, zx_end, BF16, tn=512)
    p_dt = matmul([a], [(w, 0, dt_off)], n_rows, LANES, F32, tn=LANES)
    qn = matmul([a], [(w, 0, na_off(0))], n_rows, NA_WIDTH, BF16, tn=512,
                head_norm=(tile_w(q_norm, HEAD_DIM ** -0.5 * LOG2E), None))
    kn = matmul([a], [(w, 0, na_off(1))], n_rows, NA_WIDTH, BF16, tn=512,
                head_norm=(tile_w(k_norm, 1.0), None))
    p_v = matmul([a], [(w, 0, na_off(2))], n_rows, NA_WIDTH, BF16, tn=512)

    xbc = conv_silu(p_zx, conv_w, conv_b, n_lat, seq, ctx_len)
    dt, acs, acs_t = ssd_dt(p_dt, dt_bias, a_log)
    hg = SSD_HEADS // SSD_GROUPS
    col = lambda v: v[:, :2 * SSD_HEADS].reshape(n_rows, 2, SSD_GROUPS, hg).transpose(1, 2, 0, 3)
    dt_col, acs_col = col(dt), col(acs)
    acs_row = acs_t[:2 * SSD_HEADS].reshape(2, SSD_GROUPS, hg, n_rows)
    y_f, y_b = ssd_scan(xbc, dt_col, acs_col, acs_row, n_batch, seq, ctx_len)
    s_out = ssd_gate(y_f, y_b, xbc, p_zx, d_skip, ssd_norm_w)

    n_out = na_attention(qn, kn, p_v, 0, rpb, n_batch, seq, ctx_len)
    return s_out, n_out


def _odd_layer_mix(a, in_w, gq_norm, gk_norm, dq_norm, dk_norm, lam_vecs, subln_w, lambda_init,
                   n_batch, seq, ctx_len):
    n_rows = a.shape[0]
    n_lat = n_batch * seq
    cos, sin = _rope_tables(seq)
    rope = (cos, sin, seq, n_lat)
    scale = HEAD_DIM ** -0.5 * LOG2E
    rep = lambda w, n: jnp.tile(w.astype(F32), n).reshape(1, n * HEAD_DIM)
    w = in_w.astype(BF16)
    gk_off, gv_off, dk_off, dv_off = ODD_Q, ODD_Q + GQA_KV, ODD_Q + 2 * GQA_KV, ODD_Q + 2 * GQA_KV + DIFF_QK
    q_w = jnp.concatenate([rep(gq_norm, GQA_HEADS), rep(dq_norm, 2 * DIFF_HEADS)], axis=1) * scale
    qn = matmul([a], [(w, 0, 0)], n_lat, ODD_Q, BF16, tn=512, head_norm=(q_w, rope))
    gkn = matmul([a], [(w, 0, gk_off)], n_rows, GQA_KV, BF16, tn=512,
                 head_norm=(rep(gk_norm, GQA_KV_HEADS), rope))
    dkn = matmul([a], [(w, 0, dk_off)], n_rows, DIFF_QK, BF16, tn=512,
                 head_norm=(rep(dk_norm, 2 * DIFF_HEADS), rope))
    gv = matmul([a], [(w, 0, gv_off)], n_rows, GQA_KV, BF16, tn=512)
    dv = matmul([a], [(w, 0, dv_off)], n_rows, DIFF_V, BF16, tn=512)
    lv = lam_vecs.astype(F32)
    lam = jnp.exp(jnp.dot(lv[0], lv[1])) - jnp.exp(jnp.dot(lv[2], lv[3])) + lambda_init
    og = gqa_attention(qn, gkn, gv, 0, n_batch, seq, ctx_len)
    od = diff_attention(qn, GQA_Q, dkn, dv, 0, lam, subln_w, 1.0 - lambda_init, n_batch, seq, ctx_len)
    return og, od


def kernel(x, c, ctx, c_ctx, ada_w, ada_b, norm_w, ev_in_w, ev_conv_w, ev_conv_b, ev_dt_bias, ev_a_log, ev_d_skip, ev_ssd_norm_w, ev_na_q_norm, ev_na_k_norm, ev_na_rpb, ev_out_w, od_in_w, od_gqa_q_norm, od_gqa_k_norm, od_diff_q_norm, od_diff_k_norm, od_lambda, od_diff_subln, od_out_w, moe_group_w, moe_expert_w, moe_w1, moe_w3, moe_w2):
    n_batch, seq, d = x.shape
    ctx_len = ctx.shape[1]
    depth = ada_w.shape[0]
    assert depth == 2 and d == D_MODEL
    n_lat, n_ctx = n_batch * seq, n_batch * ctx_len
    n_all = n_lat + n_ctx
    geom = (n_lat, seq, n_batch)

    cc = jnp.concatenate([c, c_ctx[None, :], jnp.zeros((8 - n_batch - 1, d), F32)], axis=0)
    mod = ada_modulation(cc, ada_w, ada_b)
    router_w = lambda l: jnp.pad(jnp.concatenate([moe_group_w[l], moe_expert_w[l]], axis=1),
                                 ((0, 0), (0, LANES - MOE_GROUPS - MOE_EXPERTS)))

    h = jnp.concatenate([x.reshape(n_lat, d), ctx.reshape(n_ctx, d)], axis=0)

    mod3 = mod[0].reshape(8, 1, 6 * d)
    a = norm_modulate(h, norm_w[0, 0], mod3, 1, 0, n_all, geom)
    s_out, n_out = _even_layer_mix(a, ev_in_w[0], ev_conv_w[0], ev_conv_b[0], ev_dt_bias[0], ev_a_log[0],
                                   ev_d_skip[0], ev_ssd_norm_w[0], ev_na_q_norm[0], ev_na_k_norm[0],
                                   ev_na_rpb[0], n_batch, seq, ctx_len)
    out_w = ev_out_w[0].astype(BF16)
    h = matmul([s_out, n_out], [(out_w, 0, 0), (out_w, SSD_INNER, 0)], n_all, d, F32, tn=512,
               residual=(h, mod3, 2, geom))
    f, route = norm_modulate(h, norm_w[0, 1], mod3, 4, 3, n_all, geom, router_w=router_w(0))
    mod3_next = mod[1].reshape(8, 1, 6 * d)
    h, a = moe_layer(h, f, route, mod3, 5, moe_w1, moe_w3, moe_w2, 0, geom,
                     next_norm=(norm_w[1, 0], mod3_next, 1, 0))

    mod3 = mod3_next
    lambda_init = 0.8 - 0.6 * math.exp(-0.3 * 1)
    og, od = _odd_layer_mix(a, od_in_w[0], od_gqa_q_norm[0], od_gqa_k_norm[0], od_diff_q_norm[0],
                            od_diff_k_norm[0], od_lambda[0], od_diff_subln[0], lambda_init,
                            n_batch, seq, ctx_len)
    out_w = od_out_w[0].astype(BF16)
    h = matmul([og, od], [(out_w, 0, 0), (out_w, GQA_Q, 0)], n_lat, d, F32, tn=512,
               residual=(h, mod3, 2, geom))
    f, route = norm_modulate(h, norm_w[1, 1], mod3, 4, 3, n_lat, geom, router_w=router_w(1))
    h = moe_layer(h, f, route, mod3, 5, moe_w1, moe_w3, moe_w2, 1, geom)
    return h.reshape(n_batch, seq, d)
```
